```python
import math
import numpy as np
import jax
import jax.numpy as jnp
from jax import lax

D_MODEL = 4096
BATCH = 1
SEQ = 8192
DEPTH = 2

GRID_W = 64
CTX_LEN = 256
N_EVEN = (DEPTH + 1) // 2
N_ODD = DEPTH // 2
MIX_W = D_MODEL
ADA_CHUNKS = 6
NORM_EPS = 1e-6

SSD_W = MIX_W // 2
SSD_P = 64
SSD_HEADS = SSD_W // SSD_P
SSD_G = 4
SSD_N = 128
SSD_CONV = 4
SSD_CHUNK = 128
XBC_W = SSD_W + 2 * SSD_G * SSD_N

NA_W = MIX_W - SSD_W
NA_DH = 128
NA_HEADS = NA_W // NA_DH
NA_WIN_H = 8
NA_WIN_W = 16
NA_QCOLS = 16
NA_KCOLS = NA_QCOLS + NA_WIN_W

EV_IN_COLS = SSD_W + XBC_W + 2 * SSD_HEADS + 3 * NA_W
FF_DENSE = 11008

RG_W = MIX_W // 2
RG_BLOCKS = 16
RG_BW = RG_W // RG_BLOCKS
RG_CONV = 4
RG_C = 8.0

HY_W = MIX_W - RG_W
HY_ORDER = 2
HY_CONV = 3
HY_EMB = 33
HY_BANDS = (HY_EMB - 1) // 2
HY_FFN = 64
HY_INNER = 2
HY_FILT = HY_ORDER * 2 * HY_W
HY_TARGET = 1e-2
HY_PCT_SHORT = 0.3
HY_PCT_LONG = 1.5
OD_IN_COLS = 2 * RG_W + 3 * HY_W

N_EXPERTS = 8
TOP_K = 2
FF_EXPERT = 3584

kernel_name = 'hybrid_ssd_natten_rglru_hyena_moe_dit'


def _rmsnorm(x, w):
    xf = x.astype(jnp.float32)
    y = xf * lax.rsqrt(jnp.mean(xf * xf, axis=-1, keepdims=True) + NORM_EPS)
    return (y * w.astype(jnp.float32)).astype(x.dtype)


def _modulated_norm(x, w, shift, scale):
    return _rmsnorm(x, w) * (1 + scale[:, None]) + shift[:, None]


def _dwconv(u, w, b, pad_left):
    k, ch = w.shape
    y = lax.conv_general_dilated(u, w.astype(u.dtype)[:, None, :], window_strides=(1,),
                                 padding=[(pad_left, k - 1 - pad_left)],
                                 dimension_numbers=('NWC', 'WIO', 'NWC'), feature_group_count=ch)
    return y + b.astype(u.dtype)


def _swiglu(h, w1, w3, w2):
    return (jax.nn.silu(h @ w1) * (h @ w3)) @ w2


def _segsum(a):
    t = a.shape[-1]
    ab = jnp.broadcast_to(a[..., :, None], a.shape + (t,))
    strict = jnp.tril(jnp.ones((t, t), dtype=bool), -1)
    cs = jnp.cumsum(jnp.where(strict, ab, 0.0), axis=-2)
    return jnp.where(jnp.tril(jnp.ones((t, t), dtype=bool)), cs, -jnp.inf)


def _ssd_chunked(xs, dt, a, bm, cm, h0):
    b, L, H, P = xs.shape
    G, N = bm.shape[2], bm.shape[3]
    E = H // G
    Q = SSD_CHUNK
    nc = L // Q
    xd = (xs.astype(jnp.float32) * dt[..., None]).reshape(b, nc, Q, G, E, P)
    da = (dt * a).reshape(b, nc, Q, G, E).transpose(0, 3, 4, 1, 2)
    bc = bm.astype(jnp.float32).reshape(b, nc, Q, G, N)
    cc = cm.astype(jnp.float32).reshape(b, nc, Q, G, N)
    da_cum = jnp.cumsum(da, axis=-1)
    decay_in = jnp.exp(_segsum(da))
    cb = jnp.einsum('bclgn,bcsgn->bgcls', cc, bc)
    y_diag = jnp.einsum('bgcls,bgecls,bcsgep->bclgep', cb, decay_in, xd)
    decay_to_end = jnp.exp(da_cum[..., -1:] - da_cum)
    chunk_states = jnp.einsum('bclgn,bgecl,bclgep->bcgepn', bc, decay_to_end, xd)
    states = jnp.concatenate([h0.reshape(b, 1, G, E, P, N), chunk_states], axis=1)
    chunk_decay = jnp.exp(_segsum(jnp.pad(da_cum[..., -1], [(0, 0), (0, 0), (0, 0), (1, 0)])))
    states = jnp.einsum('bgezc,bcgepn->bzgepn', chunk_decay, states)
    y_off = jnp.einsum('bclgn,bcgepn,bgecl->bclgep', cc, states[:, :-1], jnp.exp(da_cum))
    y = (y_diag + y_off).reshape(b, L, H, P)
    return y, states[:, -1].reshape(b, H, P, N)


def _ssd_mixer(z, xbc, dt_raw, conv_w, conv_b, dt_bias, a_log, d_skip, norm_w, h0_fwd, h0_bwd):
    b, L, _ = z.shape
    xbc = jax.nn.silu(_dwconv(xbc, conv_w, conv_b, SSD_CONV // 2))
    gn = SSD_G * SSD_N
    xs = xbc[..., :SSD_W].reshape(b, L, SSD_HEADS, SSD_P)
    bm = xbc[..., SSD_W:SSD_W + gn].reshape(b, L, SSD_G, SSD_N)
    cm = xbc[..., SSD_W + gn:].reshape(b, L, SSD_G, SSD_N)
    dt = jax.nn.softplus(dt_raw.astype(jnp.float32).reshape(b, L, 2, SSD_HEADS)
                         + dt_bias.astype(jnp.float32))
    a = -jnp.exp(a_log.astype(jnp.float32))
    y_f, s_f = _ssd_chunked(xs, dt[:, :, 0], a[0], bm, cm, h0_fwd)
    y_b, s_b = _ssd_chunked(xs[:, ::-1], dt[:, ::-1, 1], a[1], bm[:, ::-1], cm[:, ::-1], h0_bwd)
    y = y_f + y_b[:, ::-1] + xs.astype(jnp.float32) * d_skip.astype(jnp.float32)[:, None]
    y = y.reshape(b, L, SSD_W) * jax.nn.silu(z.astype(jnp.float32))
    yg = y.reshape(b, L, SSD_G, SSD_W // SSD_G)
    yg = yg * lax.rsqrt(jnp.mean(yg * yg, axis=-1, keepdims=True) + NORM_EPS)
    y = yg.reshape(b, L, SSD_W) * norm_w.astype(jnp.float32)
    return y.astype(z.dtype), s_f, s_b


def _context_attention(q, k, v):
    s = jnp.einsum('bqhd,bkhd->bhqk', q, k, preferred_element_type=jnp.float32) * NA_DH ** -0.5
    p = jax.nn.softmax(s, axis=-1).astype(v.dtype)
    o = jnp.einsum('bhqk,bkhd->bqhd', p, v)
    return o.reshape(q.shape[0], q.shape[1], NA_W)


def _neighbourhood_attention(q, k, v, k_ctx, v_ctx, rpb):
    b, L, H, dh = q.shape
    rows = L // GRID_W
    kh = min(NA_WIN_H, rows)
    n_cb = GRID_W // NA_QCOLS
    scale = dh ** -0.5
    qg = q.reshape(b, rows, GRID_W, H, dh)
    kg = k.reshape(b, rows, GRID_W, H, dh)
    vg = v.reshape(b, rows, GRID_W, H, dh)
    jb = np.arange(n_cb)[:, None]
    qcols = jb * NA_QCOLS + np.arange(NA_QCOLS)[None]
    kcols = (np.clip(jb * NA_QCOLS - NA_WIN_W // 2, 0, GRID_W - NA_KCOLS)
             + np.arange(NA_KCOLS)[None])
    cstart = np.clip(qcols - NA_WIN_W // 2, 0, GRID_W - NA_WIN_W)
    col_ok = ((kcols[:, None, :] >= cstart[..., None])
              & (kcols[:, None, :] < cstart[..., None] + NA_WIN_W))
    dcol = np.clip(kcols[:, None, :] - qcols[..., None] + NA_WIN_W - 1, 0, 2 * NA_WIN_W - 2)
    col_bias = rpb.astype(jnp.float32)[:, :, dcol]
    n_loc = kh * NA_KCOLS

    def row_block(r):
        rs = jnp.clip(r - kh // 2, 0, rows - kh)
        q_r = lax.dynamic_index_in_dim(qg, r, axis=1, keepdims=False).reshape(b, n_cb, NA_QCOLS, H, dh)
        k_blk = lax.dynamic_slice_in_dim(kg, rs, kh, axis=1)[:, :, kcols]
        v_blk = lax.dynamic_slice_in_dim(vg, rs, kh, axis=1)[:, :, kcols]
        drow = rs + jnp.arange(kh) - r + NA_WIN_H - 1
        bias = jnp.take(col_bias, drow, axis=1).transpose(0, 2, 3, 1, 4)
        s = jnp.einsum('bjqhd,bajkhd->bhjqak', q_r, k_blk,
                       preferred_element_type=jnp.float32) * scale + bias
        s = jnp.where(col_ok[None, None, :, :, None, :], s, -jnp.inf)
        s = s.reshape(b, H, n_cb, NA_QCOLS, n_loc)
        s_ctx = jnp.einsum('bjqhd,bchd->bhjqc', q_r, k_ctx,
                           preferred_element_type=jnp.float32) * scale
        p = jax.nn.softmax(jnp.concatenate([s, s_ctx], axis=-1), axis=-1).astype(v.dtype)
        p_loc = p[..., :n_loc].reshape(b, H, n_cb, NA_QCOLS, kh, NA_KCOLS)
        o = (jnp.einsum('bhjqak,bajkhd->bjqhd', p_loc, v_blk)
             + jnp.einsum('bhjqc,bchd->bjqhd', p[..., n_loc:], v_ctx))
        return o.reshape(b, GRID_W, H, dh)

    out = lax.map(row_block, jnp.arange(rows))
    return jnp.moveaxis(out, 0, 1).reshape(b, L, H * dh)


def _even_mixer(h, hc, in_w, conv_w, conv_b, dt_bias, a_log, d_skip, norm_w, rpb, out_w, need_ctx):
    o1 = SSD_W
    o2 = o1 + XBC_W
    o3 = o2 + 2 * SSD_HEADS
    splits = [o1, o2, o3, o3 + NA_W, o3 + 2 * NA_W]

    def heads(t):
        return t.reshape(t.shape[0], t.shape[1], NA_HEADS, NA_DH)

    ssd_params = (conv_w, conv_b, dt_bias, a_log, d_skip, norm_w)
    zc, xbcc, dtc, qc, kc, vc = jnp.split(hc @ in_w, splits, axis=-1)
    zero = jnp.zeros((hc.shape[0], SSD_HEADS, SSD_P, SSD_N), jnp.float32)
    y_ssd_c, s_fwd, s_bwd = _ssd_mixer(zc, xbcc, dtc, *ssd_params, zero, zero)
    kc, vc = heads(kc), heads(vc)
    z, xbc, dt_raw, q, k, v = jnp.split(h @ in_w, splits, axis=-1)
    y_ssd, _, _ = _ssd_mixer(z, xbc, dt_raw, *ssd_params, s_fwd, s_bwd)
    y_na = _neighbourhood_attention(heads(q), heads(k), heads(v), kc, vc, rpb)
    mix = jnp.concatenate([y_ssd, y_na], axis=-1) @ out_w
    mix_c = None
    if need_ctx:
        y_na_c = _context_attention(heads(qc), kc, vc)
        mix_c = jnp.concatenate([y_ssd_c, y_na_c], axis=-1) @ out_w
    return mix, mix_c


def _linear_scan(a, b, h0):
    b = b.at[:, 0].add(a[:, 0] * h0)
    _, h = lax.associative_scan(lambda l, r: (l[0] * r[0], r[0] * l[1] + r[1]), (a, b), axis=1)
    return h


def _rglru_scans(u, gate_w, gate_b, lam, h0_f, h0_b):
    b, L, _ = u.shape
    uf = u.astype(jnp.float32)
    g = jnp.einsum('blnc,dgncf->bldgnf', uf.reshape(b, L, RG_BLOCKS, RG_BW),
                   gate_w.astype(jnp.float32)) + gate_b.astype(jnp.float32)
    g = jax.nn.sigmoid(g).reshape(b, L, 2, 2, RG_W)
    r, i = g[:, :, :, 0], g[:, :, :, 1]
    log_a = RG_C * r * jax.nn.log_sigmoid(lam.astype(jnp.float32))
    a = jnp.exp(log_a)
    inp = jnp.sqrt(-jnp.expm1(2.0 * log_a)) * i * uf[:, :, None]
    h_f = _linear_scan(a[:, :, 0], inp[:, :, 0], h0_f)
    h_b = _linear_scan(a[:, ::-1, 1], inp[:, ::-1, 1], h0_b)[:, ::-1]
    return h_f, h_b


def _hyena_filters(L, w_in, b_in, w_mid, b_mid, w_out, freq, deltas):
    f32 = jnp.float32
    t = jnp.linspace(0.0, 1.0, L, dtype=f32)[:, None]
    ang = ((2.0 * math.pi / L) * jnp.arange(L, dtype=f32)[:, None]
           * jnp.linspace(1e-4, HY_BANDS - 1, HY_BANDS, dtype=f32)[None])
    feats = jnp.concatenate([t, jnp.cos(ang), -jnp.sin(ang)], axis=-1)
    freq = freq.astype(f32)
    hid = jnp.sin(freq * (feats @ w_in.astype(f32) + b_in.astype(f32)))
    for m in range(HY_INNER):
        hid = jnp.sin(freq * (hid @ w_mid[m].astype(f32) + b_mid[m].astype(f32)))
    filt = (hid @ w_out.astype(f32)) * jnp.exp(-t * jnp.abs(deltas.astype(f32)))
    return filt.reshape(L, HY_ORDER, 2, HY_W)


def _bidir_long_conv(u, f_fwd, f_bwd, skip):
    L, w = f_fwd.shape
    circ = jnp.concatenate([f_fwd[:1] + f_bwd[:1], f_fwd[1:], jnp.zeros((1, w), f_fwd.dtype),
                            f_bwd[:0:-1]], axis=0)
    y = jnp.fft.irfft(jnp.fft.rfft(u, n=2 * L, axis=1) * jnp.fft.rfft(circ, axis=0),
                      n=2 * L, axis=1)[:, :L]
    return y + u * skip


def _hyena(p, conv_w, conv_b, w_in, b_in, w_mid, b_mid, w_out, freq, deltas, skip):
    L = p.shape[1]
    u = _dwconv(p, conv_w, conv_b, (HY_CONV - 1) // 2).astype(jnp.float32)
    x1, x2, z = jnp.split(u, 3, axis=-1)
    filt = _hyena_filters(L, w_in, b_in, w_mid, b_mid, w_out, freq, deltas)
    for n, gate in enumerate((x1, x2)):
        z = gate * _bidir_long_conv(z, filt[:, n, 0], filt[:, n, 1], skip[n].astype(jnp.float32))
    return z.astype(p.dtype)


def _odd_mixer(h, hc, in_w, conv_w, conv_b, gate_w, gate_b, lam, hy_params, out_w, need_ctx):
    splits = [RG_W, 2 * RG_W]
    rg = (gate_w, gate_b, lam)
    pad = RG_CONV // 2
    zero = jnp.zeros((hc.shape[0], RG_W), jnp.float32)
    if need_ctx:
        gate_c, rec_c, hy_c = jnp.split(hc @ in_w, splits, axis=-1)
    else:
        rec_c = hc @ in_w[:, RG_W:2 * RG_W]
    hf_c, hb_c = _rglru_scans(_dwconv(rec_c, conv_w, conv_b, pad), *rg, zero, zero)
    gate, rec, hy = jnp.split(h @ in_w, splits, axis=-1)
    hf, hb = _rglru_scans(_dwconv(rec, conv_w, conv_b, pad), *rg, hf_c[:, -1], hb_c[:, 0])
    y_rg = jax.nn.gelu(gate) * (hf + hb).astype(gate.dtype)
    mix = jnp.concatenate([y_rg, _hyena(hy, *hy_params)], axis=-1) @ out_w
    mix_c = None
    if need_ctx:
        y_rg_c = jax.nn.gelu(gate_c) * (hf_c + hb_c).astype(gate_c.dtype)
        mix_c = jnp.concatenate([y_rg_c, _hyena(hy_c, *hy_params)], axis=-1) @ out_w
    return mix, mix_c


def _moe(h, router_w, w1, w3, w2):
    logits = jnp.einsum('bld,de->ble', h, router_w, preferred_element_type=jnp.float32)
    top_v, top_i = lax.top_k(logits, TOP_K)
    top_p = jax.nn.softmax(top_v, axis=-1)
    gates = jnp.sum(jax.nn.one_hot(top_i, N_EXPERTS, dtype=jnp.float32) * top_p[..., None], axis=-2)
    y = jnp.zeros_like(h)
    for e in range(N_EXPERTS):
        y = y + gates[..., e:e + 1].astype(h.dtype) * _swiglu(h, w1[e], w3[e], w2[e])
    return y


def setup_inputs(seed: int = 0) -> dict:
    key = jax.random.key(seed)
    ks = iter(jax.random.split(key, 64))
    f32 = jnp.float32

    def nrm(shape, std):
        return std * jax.random.normal(next(ks), shape, f32)

    def unif(shape, lo, hi):
        return jax.random.uniform(next(ks), shape, f32, lo, hi)

    dt0 = jnp.exp(unif((N_EVEN, 2, SSD_HEADS), math.log(1e-3), math.log(1e-1)))
    a_base = unif((N_ODD, 2, RG_W), 0.9, 0.999) ** (1.0 / RG_C)
    decay_lo = -math.log(HY_TARGET) / HY_PCT_LONG
    decay_hi = -math.log(HY_TARGET) / HY_PCT_SHORT
    base = jnp.broadcast_to(jnp.linspace(decay_lo, decay_hi, HY_W, dtype=f32), (HY_ORDER * 2, HY_W))
    deltas = base.reshape(1, HY_FILT) * (1.0 + nrm((N_ODD, HY_FILT), 0.05))
    return {
        'x': nrm((BATCH, SEQ, D_MODEL), 1.0),
        'c': nrm((BATCH, D_MODEL), 1.0),
        'ctx': nrm((BATCH, CTX_LEN, D_MODEL), 1.0),
        'c_ctx': nrm((D_MODEL,), 1.0),
        'ada_w': nrm((DEPTH, D_MODEL, ADA_CHUNKS * D_MODEL), 0.5 * D_MODEL ** -0.5),
        'ada_b': nrm((DEPTH, ADA_CHUNKS * D_MODEL), 0.01),
        'norm_mix_w': 1.0 + nrm((DEPTH, D_MODEL), 0.02),
        'norm_ffn_w': 1.0 + nrm((DEPTH, D_MODEL), 0.02),
        'norm_out_w': 1.0 + nrm((D_MODEL,), 0.02),
        'ev_in_w': nrm((N_EVEN, D_MODEL, EV_IN_COLS), D_MODEL ** -0.5),
        'ev_ssd_conv_w': nrm((N_EVEN, SSD_CONV, XBC_W), SSD_CONV ** -0.5),
        'ev_ssd_conv_b': nrm((N_EVEN, XBC_W), 0.01),
        'ev_ssd_dt_bias': dt0 + jnp.log(-jnp.expm1(-dt0)),
        'ev_ssd_a_log': jnp.log(unif((N_EVEN, 2, SSD_HEADS), 1.0, 16.0)),
        'ev_ssd_d': 1.0 + nrm((N_EVEN, SSD_HEADS), 0.1),
        'ev_ssd_norm_w': 1.0 + nrm((N_EVEN, SSD_W), 0.02),
        'ev_na_rpb': nrm((N_EVEN, NA_HEADS, 2 * NA_WIN_H - 1, 2 * NA_WIN_W - 1), 0.1),
        'ev_out_w': nrm((N_EVEN, MIX_W, D_MODEL), MIX_W ** -0.5),
        'ev_ffn_w1': nrm((N_EVEN, D_MODEL, FF_DENSE), D_MODEL ** -0.5),
        'ev_ffn_w3': nrm((N_EVEN, D_MODEL, FF_DENSE), D_MODEL ** -0.5),
        'ev_ffn_w2': nrm((N_EVEN, FF_DENSE, D_MODEL), FF_DENSE ** -0.5),
        'od_in_w': nrm((N_ODD, D_MODEL, OD_IN_COLS), D_MODEL ** -0.5),
        'od_rg_conv_w': nrm((N_ODD, RG_CONV, RG_W), RG_CONV ** -0.5),
        'od_rg_conv_b': nrm((N_ODD, RG_W), 0.01),
        'od_rg_gate_w': nrm((N_ODD, 2, 2, RG_BLOCKS, RG_BW, RG_BW), RG_BW ** -0.5),
        'od_rg_gate_b': nrm((N_ODD, 2, 2, RG_BLOCKS, RG_BW), 0.01),
        'od_rg_lambda': jnp.log(a_base) - jnp.log1p(-a_base),
        'od_hy_conv_w': nrm((N_ODD, HY_CONV, 3 * HY_W), HY_CONV ** -0.5),
        'od_hy_conv_b': nrm((N_ODD, 3 * HY_W), 0.01),
        'od_hy_w_in': nrm((N_ODD, HY_EMB, HY_FFN), HY_EMB ** -0.5),
        'od_hy_b_in': nrm((N_ODD, HY_FFN), 0.1),
        'od_hy_w_mid': nrm((N_ODD, HY_INNER, HY_FFN, HY_FFN), HY_FFN ** -0.5),
        'od_hy_b_mid': nrm((N_ODD, HY_INNER, HY_FFN), 0.1),
        'od_hy_w_out': nrm((N_ODD, HY_FFN, HY_FILT), 0.05 * HY_FFN ** -0.5),
        'od_hy_freq': 1.0 + nrm((N_ODD, HY_FFN), 0.1),
        'od_hy_deltas': deltas,
        'od_hy_skip': nrm((N_ODD, HY_ORDER, HY_W), 0.5),
        'od_out_w': nrm((N_ODD, MIX_W, D_MODEL), MIX_W ** -0.5),
        'od_router_w': nrm((N_ODD, D_MODEL, N_EXPERTS), D_MODEL ** -0.5),
        'od_moe_w1': nrm((N_ODD, N_EXPERTS, D_MODEL, FF_EXPERT), D_MODEL ** -0.5),
        'od_moe_w3': nrm((N_ODD, N_EXPERTS, D_MODEL, FF_EXPERT), D_MODEL ** -0.5),
        'od_moe_w2': nrm((N_ODD, N_EXPERTS, FF_EXPERT, D_MODEL), FF_EXPERT ** -0.5),
    }


def reference(x, c, ctx, c_ctx, ada_w, ada_b, norm_mix_w, norm_ffn_w, norm_out_w,
              ev_in_w, ev_ssd_conv_w, ev_ssd_conv_b, ev_ssd_dt_bias, ev_ssd_a_log, ev_ssd_d,
              ev_ssd_norm_w, ev_na_rpb, ev_out_w, ev_ffn_w1, ev_ffn_w3, ev_ffn_w2,
              od_in_w, od_rg_conv_w, od_rg_conv_b, od_rg_gate_w, od_rg_gate_b, od_rg_lambda,
              od_hy_conv_w, od_hy_conv_b, od_hy_w_in, od_hy_b_in, od_hy_w_mid, od_hy_b_mid,
              od_hy_w_out, od_hy_freq, od_hy_deltas, od_hy_skip, od_out_w,
              od_router_w, od_moe_w1, od_moe_w3, od_moe_w2):
    for i in range(DEPTH):
        j = i // 2
        need_ctx = i < DEPTH - 1
        mod = jnp.split(jax.nn.silu(c) @ ada_w[i] + ada_b[i], ADA_CHUNKS, axis=-1)
        mod_c = jnp.split(jax.nn.silu(c_ctx)[None] @ ada_w[i] + ada_b[i], ADA_CHUNKS, axis=-1)
        h = _modulated_norm(x, norm_mix_w[i], mod[0], mod[1])
        hc = _modulated_norm(ctx, norm_mix_w[i], mod_c[0], mod_c[1])
        if i % 2 == 0:
            mix, mix_c = _even_mixer(h, hc, ev_in_w[j], ev_ssd_conv_w[j], ev_ssd_conv_b[j],
                                     ev_ssd_dt_bias[j], ev_ssd_a_log[j], ev_ssd_d[j],
                                     ev_ssd_norm_w[j], ev_na_rpb[j], ev_out_w[j], need_ctx)
            ffn = lambda t: _swiglu(t, ev_ffn_w1[j], ev_ffn_w3[j], ev_ffn_w2[j])
        else:
            hy_params = (od_hy_conv_w[j], od_hy_conv_b[j], od_hy_w_in[j], od_hy_b_in[j],
                         od_hy_w_mid[j], od_hy_b_mid[j], od_hy_w_out[j], od_hy_freq[j],
                         od_hy_deltas[j], od_hy_skip[j])
            mix, mix_c = _odd_mixer(h, hc, od_in_w[j], od_rg_conv_w[j], od_rg_conv_b[j],
                                    od_rg_gate_w[j], od_rg_gate_b[j], od_rg_lambda[j],
                                    hy_params, od_out_w[j], need_ctx)
            ffn = lambda t: _moe(t, od_router_w[j], od_moe_w1[j], od_moe_w3[j], od_moe_w2[j])
        x = x + mod[2][:, None] * mix
        x = x + mod[5][:, None] * ffn(_modulated_norm(x, norm_ffn_w[i], mod[3], mod[4]))
        if need_ctx:
            ctx = ctx + mod_c[2][:, None] * mix_c
            ctx = ctx + mod_c[5][:, None] * ffn(_modulated_norm(ctx, norm_ffn_w[i], mod_c[3], mod_c[4]))
    return _rmsnorm(x, norm_out_w)
```

```python
import functools
import math

import numpy as np
import jax
import jax.numpy as jnp
from jax import lax
from jax.experimental import pallas as pl
from jax.experimental.pallas import tpu as pltpu

GRID_W = 64
ADA_CHUNKS = 6
NORM_EPS = 1e-6
SSD_P = 64
SSD_G = 4
SSD_N = 128
SSD_CONV = 4
SSD_CHUNK = 128
NA_DH = 128
NA_WIN_H = 8
NA_WIN_W = 16
NA_QCOLS = 16
NA_KCOLS = NA_QCOLS + NA_WIN_W
RG_BLOCKS = 16
RG_CONV = 4
RG_C = 8.0
HY_ORDER = 2
HY_CONV = 3
HY_EMB = 33
HY_BANDS = (HY_EMB - 1) // 2
HY_INNER = 2
TOP_K = 2

LANES = 128
SUBLANES = 8
VMEM_LIMIT_BYTES = 56 * 1024 * 1024
MOE_TILE_ROWS = 512

F32 = jnp.float32
BF16 = jnp.bfloat16


def _tile(n, target, align):
    for t in range(min(n, target), 0, -1):
        if n % t == 0 and t % align == 0:
            return t
    return n


def _params(*semantics):
    return pltpu.CompilerParams(dimension_semantics=semantics, vmem_limit_bytes=VMEM_LIMIT_BYTES)


def _ada_kernel(cb_ref, w_ref, b_ref, o_ref, *, n_rows):
    k, tn = w_ref.shape
    for l in range(tn // LANES):
        w = w_ref[:, l * LANES:(l + 1) * LANES].reshape(k // SUBLANES, SUBLANES, LANES)
        for r in range(n_rows):
            cb = cb_ref[r].reshape(k // SUBLANES, SUBLANES, LANES)
            part = jnp.sum(w * cb, axis=0)
            o_ref[r:r + 1, l * LANES:(l + 1) * LANES] = (
                jnp.sum(part, axis=0, keepdims=True) + b_ref[:, l * LANES:(l + 1) * LANES])


def _ada(cvecs, w, b):
    r, k = cvecs.shape
    n = w.shape[1]
    tn = _tile(n, 512, LANES)
    cb = jnp.broadcast_to(cvecs[:, :, None], (r, k, LANES))
    return pl.pallas_call(
        functools.partial(_ada_kernel, n_rows=r),
        grid=(n // tn,),
        in_specs=[pl.BlockSpec((r, k, LANES), lambda j: (0, 0, 0)),
                  pl.BlockSpec((k, tn), lambda j: (0, j)),
                  pl.BlockSpec((1, tn), lambda j: (0, j))],
        out_specs=pl.BlockSpec((r, tn), lambda j: (0, j)),
        out_shape=jax.ShapeDtypeStruct((r, n), F32),
        compiler_params=_params("parallel"),
        name="ada_matvec",
    )(cb, w, b.reshape(1, n))


def _norm_rows(x, w, shift, scale):
    y = x * lax.rsqrt(jnp.mean(x * x, axis=-1, keepdims=True) + NORM_EPS)
    y = y * w
    if scale is not None:
        y = y * (1.0 + scale) + shift
    return y


def _norm_kernel(x_ref, w_ref, sh_ref, sc_ref, o_ref):
    o_ref[...] = _norm_rows(x_ref[...], w_ref[...], sh_ref[...], sc_ref[...]).astype(o_ref.dtype)


def _final_norm_kernel(x_ref, w_ref, o_ref):
    o_ref[...] = _norm_rows(x_ref[...], w_ref[...], None, None).astype(o_ref.dtype)


def _norm_router_kernel(x_ref, w_ref, sh_ref, sc_ref, rw_ref, o_ref, route_ref, *, n_experts):
    h = _norm_rows(x_ref[...], w_ref[...], sh_ref[...], sc_ref[...])
    o_ref[...] = h.astype(o_ref.dtype)
    logits = jnp.dot(h, rw_ref[...], preferred_element_type=F32, precision=lax.Precision.HIGHEST)
    col = lax.broadcasted_iota(jnp.int32, logits.shape, 1)
    neg = jnp.float32(-jnp.inf)
    l1 = jnp.where(col < n_experts, logits, neg)
    m1 = jnp.max(l1, axis=-1, keepdims=True)
    i1 = jnp.min(jnp.where(l1 == m1, col, LANES), axis=-1, keepdims=True)
    l2 = jnp.where(col == i1, neg, l1)
    m2 = jnp.max(l2, axis=-1, keepdims=True)
    i2 = jnp.min(jnp.where(l2 == m2, col, LANES), axis=-1, keepdims=True)
    e2 = jnp.exp(m2 - m1)
    g1 = 1.0 / (1.0 + e2)
    g2 = e2 * g1
    route = jnp.where(col == 0, i1.astype(F32),
                      jnp.where(col == 1, i2.astype(F32),
                                jnp.where(col == 2, g1, jnp.where(col == 3, g2, 0.0))))
    route_ref[...] = route


def _mod_norm(x, w, shift, scale, out_dtype=BF16):
    m, d = x.shape
    tr = _tile(m, 256, SUBLANES)
    vec = pl.BlockSpec((1, d), lambda i: (0, 0))
    return pl.pallas_call(
        _norm_kernel,
        grid=(m // tr,),
        in_specs=[pl.BlockSpec((tr, d), lambda i: (i, 0)), vec, vec, vec],
        out_specs=pl.BlockSpec((tr, d), lambda i: (i, 0)),
        out_shape=jax.ShapeDtypeStruct((m, d), out_dtype),
        compiler_params=_params("parallel"),
        name="mod_norm",
    )(x, w.reshape(1, d), shift.reshape(1, d), scale.reshape(1, d))


def _final_norm(x, w):
    m, d = x.shape
    tr = _tile(m, 256, SUBLANES)
    return pl.pallas_call(
        _final_norm_kernel,
        grid=(m // tr,),
        in_specs=[pl.BlockSpec((tr, d), lambda i: (i, 0)), pl.BlockSpec((1, d), lambda i: (0, 0))],
        out_specs=pl.BlockSpec((tr, d), lambda i: (i, 0)),
        out_shape=jax.ShapeDtypeStruct((m, d), x.dtype),
        compiler_params=_params("parallel"),
        name="final_norm",
    )(x, w.reshape(1, d))


def _mod_norm_router(x, w, shift, scale, router_w):
    m, d = x.shape
    n_experts = router_w.shape[1]
    tr = _tile(m, 256, SUBLANES)
    rw = jnp.zeros((d, LANES), F32).at[:, :n_experts].set(router_w.astype(F32))
    vec = pl.BlockSpec((1, d), lambda i: (0, 0))
    return pl.pallas_call(
        functools.partial(_norm_router_kernel, n_experts=n_experts),
        grid=(m // tr,),
        in_specs=[pl.BlockSpec((tr, d), lambda i: (i, 0)), vec, vec, vec,
                  pl.BlockSpec((d, LANES), lambda i: (0, 0))],
        out_specs=[pl.BlockSpec((tr, d), lambda i: (i, 0)),
                   pl.BlockSpec((tr, LANES), lambda i: (i, 0))],
        out_shape=[jax.ShapeDtypeStruct((m, d), BF16), jax.ShapeDtypeStruct((m, LANES), F32)],
        compiler_params=_params("parallel"),
        name="mod_norm_router",
    )(x, w.reshape(1, d), shift.reshape(1, d), scale.reshape(1, d), rw)


def _mm_kernel(a_ref, b_ref, o_ref):
    o_ref[...] = jnp.dot(a_ref[...], b_ref[...], preferred_element_type=F32).astype(o_ref.dtype)


def _mm_residual_kernel(a_ref, b_ref, res_ref, gate_ref, o_ref):
    acc = jnp.dot(a_ref[...], b_ref[...], preferred_element_type=F32)
    o_ref[...] = res_ref[...] + gate_ref[...] * acc


def _mm_swiglu_kernel(a_ref, b1_ref, b3_ref, o_ref):
    a = a_ref[...]
    g = jnp.dot(a, b1_ref[...], preferred_element_type=F32)
    u = jnp.dot(a, b3_ref[...], preferred_element_type=F32)
    o_ref[...] = (g * jax.nn.sigmoid(g) * u).astype(o_ref.dtype)


def _mm_tiles(m, k, n, tm, tn):
    tm = _tile(m, tm, SUBLANES)
    tn = _tile(n, tn, LANES)
    return tm, tn


def _mm(a, b, *, out_dtype=F32, tm=512, tn=1024):
    m, k = a.shape
    n = b.shape[1]
    tm, tn = _mm_tiles(m, k, n, tm, tn)
    return pl.pallas_call(
        _mm_kernel,
        grid=(n // tn, m // tm),
        in_specs=[pl.BlockSpec((tm, k), lambda j, i: (i, 0)),
                  pl.BlockSpec((k, tn), lambda j, i: (0, j))],
        out_specs=pl.BlockSpec((tm, tn), lambda j, i: (i, j)),
        out_shape=jax.ShapeDtypeStruct((m, n), out_dtype),
        compiler_params=_params("parallel", "parallel"),
        name="matmul",
    )(a, b)


def _mm_residual(a, b, res, gate, *, tm=512, tn=1024):
    m, k = a.shape
    n = b.shape[1]
    tm, tn = _mm_tiles(m, k, n, tm, tn)
    return pl.pallas_call(
        _mm_residual_kernel,
        grid=(n // tn, m // tm),
        in_specs=[pl.BlockSpec((tm, k), lambda j, i: (i, 0)),
                  pl.BlockSpec((k, tn), lambda j, i: (0, j)),
                  pl.BlockSpec((tm, tn), lambda j, i: (i, j)),
                  pl.BlockSpec((1, tn), lambda j, i: (0, j))],
        out_specs=pl.BlockSpec((tm, tn), lambda j, i: (i, j)),
        out_shape=jax.ShapeDtypeStruct((m, n), F32),
        compiler_params=_params("parallel", "parallel"),
        name="matmul_residual",
    )(a, b, res, gate.reshape(1, n))


def _mm_swiglu(a, b1, b3, *, tm=1024, tn=512):
    m, k = a.shape
    n = b1.shape[1]
    tm, tn = _mm_tiles(m, k, n, tm, tn)
    return pl.pallas_call(
        _mm_swiglu_kernel,
        grid=(m // tm, n // tn),
        in_specs=[pl.BlockSpec((tm, k), lambda i, j: (i, 0)),
                  pl.BlockSpec((k, tn), lambda i, j: (0, j)),
                  pl.BlockSpec((k, tn), lambda i, j: (0, j))],
        out_specs=pl.BlockSpec((tm, tn), lambda i, j: (i, j)),
        out_shape=jax.ShapeDtypeStruct((m, n), BF16),
        compiler_params=_params("parallel", "parallel"),
        name="matmul_swiglu",
    )(a, b1, b3)


def _moe_up_kernel(te_ref, nt_ref, a_ref, b1_ref, b3_ref, o_ref):
    @pl.when(pl.program_id(0) < nt_ref[0])
    def _():
        a = a_ref[...]
        g = jnp.dot(a, b1_ref[...], preferred_element_type=F32)
        u = jnp.dot(a, b3_ref[...], preferred_element_type=F32)
        o_ref[...] = (g * jax.nn.sigmoid(g) * u).astype(o_ref.dtype)


def _moe_down_kernel(te_ref, nt_ref, a_ref, b_ref, gate_ref, o_ref):
    @pl.when(pl.program_id(0) < nt_ref[0])
    def _():
        acc = jnp.dot(a_ref[...], b_ref[...], preferred_element_type=F32)
        o_ref[...] = gate_ref[...] * acc


def _moe_grouped(xg, gate_slot, tile_expert, n_tiles, w1, w3, w2, *, tm):
    s, d = xg.shape
    f = w1.shape[2]
    n_t = s // tm
    tf = _tile(f, 512, LANES)
    n_f = f // tf

    def row(i, nt):
        return jnp.minimum(i, nt[0] - 1)

    def col(i, j, nt, last):
        return jnp.where(i < nt[0], j, last)

    up = pl.pallas_call(
        _moe_up_kernel,
        grid_spec=pltpu.PrefetchScalarGridSpec(
            num_scalar_prefetch=2,
            grid=(n_t, n_f),
            in_specs=[
                pl.BlockSpec((tm, d), lambda i, j, te, nt: (row(i, nt), 0)),
                pl.BlockSpec((None, d, tf), lambda i, j, te, nt: (te[row(i, nt)], 0, col(i, j, nt, n_f - 1))),
                pl.BlockSpec((None, d, tf), lambda i, j, te, nt: (te[row(i, nt)], 0, col(i, j, nt, n_f - 1))),
            ],
            out_specs=pl.BlockSpec((tm, tf), lambda i, j, te, nt: (row(i, nt), col(i, j, nt, n_f - 1))),
        ),
        out_shape=jax.ShapeDtypeStruct((s, f), BF16),
        compiler_params=_params("arbitrary", "arbitrary"),
        name="moe_up",
    )(tile_expert, n_tiles, xg, w1, w3)

    td = _tile(d, 1024, LANES)
    n_d = d // td
    return pl.pallas_call(
        _moe_down_kernel,
        grid_spec=pltpu.PrefetchScalarGridSpec(
            num_scalar_prefetch=2,
            grid=(n_t, n_d),
            in_specs=[
                pl.BlockSpec((tm, f), lambda i, j, te, nt: (row(i, nt), 0)),
                pl.BlockSpec((None, f, td), lambda i, j, te, nt: (te[row(i, nt)], 0, col(i, j, nt, n_d - 1))),
                pl.BlockSpec((tm, 1), lambda i, j, te, nt: (row(i, nt), 0)),
            ],
            out_specs=pl.BlockSpec((tm, td), lambda i, j, te, nt: (row(i, nt), col(i, j, nt, n_d - 1))),
        ),
        out_shape=jax.ShapeDtypeStruct((s, d), F32),
        compiler_params=_params("arbitrary", "arbitrary"),
        name="moe_down",
    )(tile_expert, n_tiles, up, w2, gate_slot)


def _moe(h2, route, w1, w3, w2):
    n, d = h2.shape
    n_e = w1.shape[0]
    tm = min(MOE_TILE_ROWS, n)
    n_t = -(-(TOP_K * n + n_e * (tm - 1)) // tm)
    s = n_t * tm
    expert = route[:, :TOP_K].astype(jnp.int32).T.reshape(-1)
    gate = route[:, TOP_K:2 * TOP_K].T.reshape(-1)
    token = jnp.tile(jnp.arange(n, dtype=jnp.int32), TOP_K)
    onehot = (expert[:, None] == jnp.arange(n_e, dtype=jnp.int32)[None]).astype(jnp.int32)
    csum = jnp.cumsum(onehot, axis=0)
    rank = jnp.sum(onehot * csum, axis=1) - 1
    counts = csum[-1]
    padded = -(-counts // tm) * tm
    ends = jnp.cumsum(padded)
    starts = ends - padded
    slot = starts[expert] + rank
    token_of_slot = jnp.zeros((s,), jnp.int32).at[slot].set(token)
    gate_of_slot = jnp.zeros((s,), F32).at[slot].set(gate)
    n_tiles = (ends[-1] // tm).astype(jnp.int32).reshape(1)
    tile_start = jnp.arange(n_t, dtype=jnp.int32) * tm
    tile_expert = jnp.minimum(jnp.sum((tile_start[:, None] >= ends[None]).astype(jnp.int32), axis=1), n_e - 1)
    xg = jnp.take(h2, token_of_slot, axis=0)
    yo = _moe_grouped(xg, gate_of_slot.reshape(s, 1), tile_expert, n_tiles, w1, w3, w2, tm=tm)
    slots = slot.reshape(TOP_K, n)
    y = jnp.take(yo, slots[0], axis=0)
    for kk in range(1, TOP_K):
        y = y + jnp.take(yo, slots[kk], axis=0)
    return y


def _dwconv(u, w, b, pad_left):
    k, ch = w.shape
    y = lax.conv_general_dilated(u, w.astype(u.dtype)[:, None, :], window_strides=(1,),
                                 padding=[(pad_left, k - 1 - pad_left)],
                                 dimension_numbers=('NWC', 'WIO', 'NWC'), feature_group_count=ch)
    return y + b.astype(u.dtype)


def _segsum(a):
    t = a.shape[-1]
    ab = jnp.broadcast_to(a[..., :, None], a.shape + (t,))
    strict = jnp.tril(jnp.ones((t, t), dtype=bool), -1)
    cs = jnp.cumsum(jnp.where(strict, ab, 0.0), axis=-2)
    return jnp.where(jnp.tril(jnp.ones((t, t), dtype=bool)), cs, -jnp.inf)


def _ssd_chunked(xs, dt, a, bm, cm, h0):
    b, L, H, P = xs.shape
    G, N = bm.shape[2], bm.shape[3]
    E = H // G
    Q = min(SSD_CHUNK, L)
    nc = L // Q
    xd = (xs.astype(F32) * dt[..., None]).reshape(b, nc, Q, G, E, P)
    da = (dt * a).reshape(b, nc, Q, G, E).transpose(0, 3, 4, 1, 2)
    bc = bm.astype(F32).reshape(b, nc, Q, G, N)
    cc = cm.astype(F32).reshape(b, nc, Q, G, N)
    da_cum = jnp.cumsum(da, axis=-1)
    decay_in = jnp.exp(_segsum(da))
    cb = jnp.einsum('bclgn,bcsgn->bgcls', cc, bc)
    y_diag = jnp.einsum('bgcls,bgecls,bcsgep->bclgep', cb, decay_in, xd)
    decay_to_end = jnp.exp(da_cum[..., -1:] - da_cum)
    chunk_states = jnp.einsum('bclgn,bgecl,bclgep->bcgepn', bc, decay_to_end, xd)
    states = jnp.concatenate([h0.reshape(b, 1, G, E, P, N), chunk_states], axis=1)
    chunk_decay = jnp.exp(_segsum(jnp.pad(da_cum[..., -1], [(0, 0), (0, 0), (0, 0), (1, 0)])))
    states = jnp.einsum('bgezc,bcgepn->bzgepn', chunk_decay, states)
    y_off = jnp.einsum('bclgn,bcgepn,bgecl->bclgep', cc, states[:, :-1], jnp.exp(da_cum))
    y = (y_diag + y_off).reshape(b, L, H, P)
    return y, states[:, -1].reshape(b, H, P, N)


def _ssd_mixer(z, xbc, dt_raw, conv_w, conv_b, dt_bias, a_log, d_skip, norm_w, h0_fwd, h0_bwd):
    b, L, _ = z.shape
    ssd_w = z.shape[-1]
    heads = ssd_w // SSD_P
    xbc = jax.nn.silu(_dwconv(xbc, conv_w, conv_b, SSD_CONV // 2))
    gn = SSD_G * SSD_N
    xs = xbc[..., :ssd_w].reshape(b, L, heads, SSD_P)
    bm = xbc[..., ssd_w:ssd_w + gn].reshape(b, L, SSD_G, SSD_N)
    cm = xbc[..., ssd_w + gn:].reshape(b, L, SSD_G, SSD_N)
    dt = jax.nn.softplus(dt_raw.astype(F32).reshape(b, L, 2, heads) + dt_bias.astype(F32))
    a = -jnp.exp(a_log.astype(F32))
    y_f, s_f = _ssd_chunked(xs, dt[:, :, 0], a[0], bm, cm, h0_fwd)
    y_b, s_b = _ssd_chunked(xs[:, ::-1], dt[:, ::-1, 1], a[1], bm[:, ::-1], cm[:, ::-1], h0_bwd)
    y = y_f + y_b[:, ::-1] + xs.astype(F32) * d_skip.astype(F32)[:, None]
    y = y.reshape(b, L, ssd_w) * jax.nn.silu(z.astype(F32))
    yg = y.reshape(b, L, SSD_G, ssd_w // SSD_G)
    yg = yg * lax.rsqrt(jnp.mean(yg * yg, axis=-1, keepdims=True) + NORM_EPS)
    y = yg.reshape(b, L, ssd_w) * norm_w.astype(F32)
    return y.astype(z.dtype), s_f, s_b


def _context_attention(q, k, v):
    s = jnp.einsum('bqhd,bkhd->bhqk', q, k, preferred_element_type=F32) * NA_DH ** -0.5
    p = jax.nn.softmax(s, axis=-1).astype(v.dtype)
    o = jnp.einsum('bhqk,bkhd->bqhd', p, v)
    return o.reshape(q.shape[0], q.shape[1], -1)


def _neighbourhood_attention(q, k, v, k_ctx, v_ctx, rpb):
    b, L, H, dh = q.shape
    rows = L // GRID_W
    kh = min(NA_WIN_H, rows)
    n_cb = GRID_W // NA_QCOLS
    scale = dh ** -0.5
    qg = q.reshape(b, rows, GRID_W, H, dh)
    kg = k.reshape(b, rows, GRID_W, H, dh)
    vg = v.reshape(b, rows, GRID_W, H, dh)
    jb = np.arange(n_cb)[:, None]
    qcols = jb * NA_QCOLS + np.arange(NA_QCOLS)[None]
    kcols = (np.clip(jb * NA_QCOLS - NA_WIN_W // 2, 0, GRID_W - NA_KCOLS) + np.arange(NA_KCOLS)[None])
    cstart = np.clip(qcols - NA_WIN_W // 2, 0, GRID_W - NA_WIN_W)
    col_ok = ((kcols[:, None, :] >= cstart[..., None]) & (kcols[:, None, :] < cstart[..., None] + NA_WIN_W))
    dcol = np.clip(kcols[:, None, :] - qcols[..., None] + NA_WIN_W - 1, 0, 2 * NA_WIN_W - 2)
    col_bias = rpb.astype(F32)[:, :, dcol]
    n_loc = kh * NA_KCOLS

    def row_block(r):
        rs = jnp.clip(r - kh // 2, 0, rows - kh)
        q_r = lax.dynamic_index_in_dim(qg, r, axis=1, keepdims=False).reshape(b, n_cb, NA_QCOLS, H, dh)
        k_blk = lax.dynamic_slice_in_dim(kg, rs, kh, axis=1)[:, :, kcols]
        v_blk = lax.dynamic_slice_in_dim(vg, rs, kh, axis=1)[:, :, kcols]
        drow = rs + jnp.arange(kh) - r + NA_WIN_H - 1
        bias = jnp.take(col_bias, drow, axis=1).transpose(0, 2, 3, 1, 4)
        s = jnp.einsum('bjqhd,bajkhd->bhjqak', q_r, k_blk, preferred_element_type=F32) * scale + bias
        s = jnp.where(col_ok[None, None, :, :, None, :], s, -jnp.inf)
        s = s.reshape(b, H, n_cb, NA_QCOLS, n_loc)
        s_ctx = jnp.einsum('bjqhd,bchd->bhjqc', q_r, k_ctx, preferred_element_type=F32) * scale
        p = jax.nn.softmax(jnp.concatenate([s, s_ctx], axis=-1), axis=-1).astype(v.dtype)
        p_loc = p[..., :n_loc].reshape(b, H, n_cb, NA_QCOLS, kh, NA_KCOLS)
        o = (jnp.einsum('bhjqak,bajkhd->bjqhd', p_loc, v_blk)
             + jnp.einsum('bhjqc,bchd->bjqhd', p[..., n_loc:], v_ctx))
        return o.reshape(b, GRID_W, H, dh)

    out = lax.map(row_block, jnp.arange(rows))
    return jnp.moveaxis(out, 0, 1).reshape(b, L, H * dh)


def _linear_scan(a, b, h0):
    b = b.at[:, 0].add(a[:, 0] * h0)
    _, h = lax.associative_scan(lambda l, r: (l[0] * r[0], r[0] * l[1] + r[1]), (a, b), axis=1)
    return h


def _rglru_scans(u, gate_w, gate_b, lam, h0_f, h0_b):
    b, L, rg_w = u.shape
    bw = rg_w // RG_BLOCKS
    uf = u.astype(F32)
    g = jnp.einsum('blnc,dgncf->bldgnf', uf.reshape(b, L, RG_BLOCKS, bw), gate_w.astype(F32)) + gate_b.astype(F32)
    g = jax.nn.sigmoid(g).reshape(b, L, 2, 2, rg_w)
    r, i = g[:, :, :, 0], g[:, :, :, 1]
    log_a = RG_C * r * jax.nn.log_sigmoid(lam.astype(F32))
    a = jnp.exp(log_a)
    inp = jnp.sqrt(-jnp.expm1(2.0 * log_a)) * i * uf[:, :, None]
    h_f = _linear_scan(a[:, :, 0], inp[:, :, 0], h0_f)
    h_b = _linear_scan(a[:, ::-1, 1], inp[:, ::-1, 1], h0_b)[:, ::-1]
    return h_f, h_b


def _hyena_filters(L, w_in, b_in, w_mid, b_mid, w_out, freq, deltas):
    t = jnp.linspace(0.0, 1.0, L, dtype=F32)[:, None]
    ang = ((2.0 * math.pi / L) * jnp.arange(L, dtype=F32)[:, None]
           * jnp.linspace(1e-4, HY_BANDS - 1, HY_BANDS, dtype=F32)[None])
    feats = jnp.concatenate([t, jnp.cos(ang), -jnp.sin(ang)], axis=-1)
    freq = freq.astype(F32)
    hid = jnp.sin(freq * (feats @ w_in.astype(F32) + b_in.astype(F32)))
    for m in range(HY_INNER):
        hid = jnp.sin(freq * (hid @ w_mid[m].astype(F32) + b_mid[m].astype(F32)))
    filt = (hid @ w_out.astype(F32)) * jnp.exp(-t * jnp.abs(deltas.astype(F32)))
    return filt.reshape(L, HY_ORDER, 2, -1)


def _bidir_long_conv(u, f_fwd, f_bwd, skip):
    L, w = f_fwd.shape
    circ = jnp.concatenate([f_fwd[:1] + f_bwd[:1], f_fwd[1:], jnp.zeros((1, w), f_fwd.dtype), f_bwd[:0:-1]], axis=0)
    y = jnp.fft.irfft(jnp.fft.rfft(u, n=2 * L, axis=1) * jnp.fft.rfft(circ, axis=0), n=2 * L, axis=1)[:, :L]
    return y + u * skip


def _hyena(p, conv_w, conv_b, w_in, b_in, w_mid, b_mid, w_out, freq, deltas, skip):
    L = p.shape[1]
    u = _dwconv(p, conv_w, conv_b, (HY_CONV - 1) // 2).astype(F32)
    x1, x2, z = jnp.split(u, 3, axis=-1)
    filt = _hyena_filters(L, w_in, b_in, w_mid, b_mid, w_out, freq, deltas)
    for n, gate in enumerate((x1, x2)):
        z = gate * _bidir_long_conv(z, filt[:, n, 0], filt[:, n, 1], skip[n].astype(F32))
    return z.astype(p.dtype)


def _even_layer_mix(h, hc, in_w, conv_w, conv_b, dt_bias, a_log, d_skip, norm_w, rpb):
    d = h.shape[1]
    ssd_w = norm_w.shape[0]
    xbc_w = conv_w.shape[1]
    heads = ssd_w // SSD_P
    na_w = d - ssd_w
    na_heads = na_w // NA_DH
    o1 = ssd_w
    o2 = o1 + xbc_w
    o3 = o2 + 2 * heads
    splits = [o1, o2, o3, o3 + na_w, o3 + 2 * na_w]

    def hd(t):
        return t.reshape(1, t.shape[0], na_heads, NA_DH)

    w_zx = in_w[:, :o2].astype(BF16)
    w_dt = jnp.zeros((d, LANES), BF16).at[:, :2 * heads].set(in_w[:, o2:o3].astype(BF16))
    w_qkv = in_w[:, o3:].astype(BF16)

    def project(t):
        zx, dtp, qkv = _mm(t, w_zx)[None], _mm(t, w_dt)[None], _mm(t, w_qkv)[None]
        return (zx[..., :o1], zx[..., o1:], dtp[..., :2 * heads],
                qkv[..., :na_w], qkv[..., na_w:2 * na_w], qkv[..., 2 * na_w:])

    ssd_params = (conv_w, conv_b, dt_bias, a_log, d_skip, norm_w)
    zc, xbcc, dtc, qc, kc, vc = project(hc)
    zero = jnp.zeros((1, heads, SSD_P, SSD_N), F32)
    y_ssd_c, s_fwd, s_bwd = _ssd_mixer(zc, xbcc, dtc, *ssd_params, zero, zero)
    kc, vc = hd(kc[0]), hd(vc[0])
    z, xbc, dt_raw, q, k, v = project(h)
    y_ssd, _, _ = _ssd_mixer(z, xbc, dt_raw, *ssd_params, s_fwd, s_bwd)
    y_na = _neighbourhood_attention(hd(q[0]), hd(k[0]), hd(v[0]), kc, vc, rpb)
    y = jnp.concatenate([y_ssd, y_na], axis=-1)[0].astype(BF16)
    y_na_c = _context_attention(hd(qc[0]), kc, vc)
    y_c = jnp.concatenate([y_ssd_c, y_na_c], axis=-1)[0].astype(BF16)
    return y, y_c


def _odd_layer_mix(h, hc, in_w, conv_w, conv_b, gate_w, gate_b, lam, hy_params):
    rg_w = lam.shape[1]
    w = in_w.astype(BF16)
    pad = RG_CONV // 2
    zero = jnp.zeros((1, rg_w), F32)
    rec_c = _mm(hc, w[:, rg_w:2 * rg_w])[None]
    hf_c, hb_c = _rglru_scans(_dwconv(rec_c, conv_w, conv_b, pad), gate_w, gate_b, lam, zero, zero)
    proj = _mm(h, w)[None]
    gate, rec, hy = proj[..., :rg_w], proj[..., rg_w:2 * rg_w], proj[..., 2 * rg_w:]
    hf, hb = _rglru_scans(_dwconv(rec, conv_w, conv_b, pad), gate_w, gate_b, lam, hf_c[:, -1], hb_c[:, 0])
    y_rg = jax.nn.gelu(gate) * (hf + hb).astype(gate.dtype)
    return jnp.concatenate([y_rg, _hyena(hy, *hy_params)], axis=-1)[0].astype(BF16)


def kernel(x, c, ctx, c_ctx, ada_w, ada_b, norm_mix_w, norm_ffn_w, norm_out_w, ev_in_w, ev_ssd_conv_w, ev_ssd_conv_b, ev_ssd_dt_bias, ev_ssd_a_log, ev_ssd_d, ev_ssd_norm_w, ev_na_rpb, ev_out_w, ev_ffn_w1, ev_ffn_w3, ev_ffn_w2, od_in_w, od_rg_conv_w, od_rg_conv_b, od_rg_gate_w, od_rg_gate_b, od_rg_lambda, od_hy_conv_w, od_hy_conv_b, od_hy_w_in, od_hy_b_in, od_hy_w_mid, od_hy_b_mid, od_hy_w_out, od_hy_freq, od_hy_deltas, od_hy_skip, od_out_w, od_router_w, od_moe_w1, od_moe_w3, od_moe_w2):
    batch, seq, d = x.shape
    assert batch == 1 and ada_w.shape[0] == 2, "kernel is written for one sequence and the even/odd layer pair"
    xs = x[0]
    cs = ctx[0]
    cvecs = jnp.concatenate([jax.nn.silu(c), jax.nn.silu(c_ctx)[None]], axis=0)

    mods = _ada(cvecs, ada_w[0], ada_b[0]).reshape(2, ADA_CHUNKS, d)
    mod, mod_c = mods[0], mods[1]
    h = _mod_norm(xs, norm_mix_w[0], mod[0], mod[1])
    hc = _mod_norm(cs, norm_mix_w[0], mod_c[0], mod_c[1])
    y, y_c = _even_layer_mix(h, hc, ev_in_w[0], ev_ssd_conv_w[0], ev_ssd_conv_b[0], ev_ssd_dt_bias[0],
                             ev_ssd_a_log[0], ev_ssd_d[0], ev_ssd_norm_w[0], ev_na_rpb[0])
    out_w = ev_out_w[0].astype(BF16)
    xs = _mm_residual(y, out_w, xs, mod[2])
    cs = _mm_residual(y_c, out_w, cs, mod_c[2])
    w1, w3, w2 = ev_ffn_w1[0].astype(BF16), ev_ffn_w3[0].astype(BF16), ev_ffn_w2[0].astype(BF16)
    h2 = _mod_norm(xs, norm_ffn_w[0], mod[3], mod[4])
    xs = _mm_residual(_mm_swiglu(h2, w1, w3), w2, xs, mod[5], tm=256, tn=512)
    h2c = _mod_norm(cs, norm_ffn_w[0], mod_c[3], mod_c[4])
    cs = _mm_residual(_mm_swiglu(h2c, w1, w3), w2, cs, mod_c[5], tm=256, tn=512)

    mods = _ada(cvecs, ada_w[1], ada_b[1]).reshape(2, ADA_CHUNKS, d)
    mod, mod_c = mods[0], mods[1]
    h = _mod_norm(xs, norm_mix_w[1], mod[0], mod[1])
    hc = _mod_norm(cs, norm_mix_w[1], mod_c[0], mod_c[1])
    hy_params = (od_hy_conv_w[0], od_hy_conv_b[0], od_hy_w_in[0], od_hy_b_in[0], od_hy_w_mid[0],
                 od_hy_b_mid[0], od_hy_w_out[0], od_hy_freq[0], od_hy_deltas[0], od_hy_skip[0])
    y = _odd_layer_mix(h, hc, od_in_w[0], od_rg_conv_w[0], od_rg_conv_b[0], od_rg_gate_w[0],
                       od_rg_gate_b[0], od_rg_lambda[0], hy_params)
    xs = _mm_residual(y, od_out_w[0].astype(BF16), xs, mod[2])
    h2, route = _mod_norm_router(xs, norm_ffn_w[1], mod[3], mod[4], od_router_w[0])
    ymoe = _moe(h2, route, od_moe_w1[0].astype(BF16), od_moe_w3[0].astype(BF16), od_moe_w2[0].astype(BF16))
    xs = xs + mod[5][None, :] * ymoe
    return _final_norm(xs, norm_out_w)[None]
```

```python
import functools
import math

import numpy as np
import jax
import jax.numpy as jnp
from jax import lax
from jax.experimental import pallas as pl
from jax.experimental.pallas import tpu as pltpu

GRID_W = 64
ADA_CHUNKS = 6
NORM_EPS = 1e-6
SSD_P = 64
SSD_G = 4
SSD_N = 128
SSD_CONV = 4
SSD_CHUNK = 128
NA_DH = 128
NA_WIN_H = 8
NA_WIN_W = 16
NA_QCOLS = 16
NA_KCOLS = NA_QCOLS + NA_WIN_W
RG_BLOCKS = 16
RG_CONV = 4
RG_C = 8.0
HY_ORDER = 2
HY_CONV = 3
HY_EMB = 33
HY_BANDS = (HY_EMB - 1) // 2
HY_INNER = 2
TOP_K = 2

LANES = 128
SUBLANES = 8
VMEM_LIMIT_BYTES = 56 * 1024 * 1024
MOE_TILE_ROWS = 512

F32 = jnp.float32
BF16 = jnp.bfloat16


def _tile(n, target, align):
    for t in range(min(n, target), 0, -1):
        if n % t == 0 and t % align == 0:
            return t
    return n


def _params(*semantics):
    return pltpu.CompilerParams(dimension_semantics=semantics, vmem_limit_bytes=VMEM_LIMIT_BYTES)


def _ada_kernel(cb_ref, w_ref, b_ref, o_ref, *, n_rows):
    k, tn = w_ref.shape
    for l in range(tn // LANES):
        w = w_ref[:, l * LANES:(l + 1) * LANES].reshape(k // SUBLANES, SUBLANES, LANES)
        for r in range(n_rows):
            cb = cb_ref[r].reshape(k // SUBLANES, SUBLANES, LANES)
            part = jnp.sum(w * cb, axis=0)
            o_ref[r:r + 1, l * LANES:(l + 1) * LANES] = (
                jnp.sum(part, axis=0, keepdims=True) + b_ref[:, l * LANES:(l + 1) * LANES])


def _ada(cvecs, w, b):
    r, k = cvecs.shape
    n = w.shape[1]
    tn = _tile(n, 512, LANES)
    cb = jnp.broadcast_to(cvecs[:, :, None], (r, k, LANES))
    return pl.pallas_call(
        functools.partial(_ada_kernel, n_rows=r),
        grid=(n // tn,),
        in_specs=[pl.BlockSpec((r, k, LANES), lambda j: (0, 0, 0)),
                  pl.BlockSpec((k, tn), lambda j: (0, j)),
                  pl.BlockSpec((1, tn), lambda j: (0, j))],
        out_specs=pl.BlockSpec((r, tn), lambda j: (0, j)),
        out_shape=jax.ShapeDtypeStruct((r, n), F32),
        compiler_params=_params("parallel"),
        name="ada_matvec",
    )(cb, w, b.reshape(1, n))


def _norm_rows(x, w, shift, scale):
    y = x * lax.rsqrt(jnp.mean(x * x, axis=-1, keepdims=True) + NORM_EPS)
    y = y * w
    if scale is not None:
        y = y * (1.0 + scale) + shift
    return y


def _norm_kernel(x_ref, w_ref, sh_ref, sc_ref, o_ref):
    o_ref[...] = _norm_rows(x_ref[...], w_ref[...], sh_ref[...], sc_ref[...]).astype(o_ref.dtype)


def _final_norm_kernel(x_ref, w_ref, o_ref):
    o_ref[...] = _norm_rows(x_ref[...], w_ref[...], None, None).astype(o_ref.dtype)


def _norm_router_kernel(x_ref, w_ref, sh_ref, sc_ref, rw_ref, o_ref, route_ref, *, n_experts):
    h = _norm_rows(x_ref[...], w_ref[...], sh_ref[...], sc_ref[...])
    o_ref[...] = h.astype(o_ref.dtype)
    logits = jnp.dot(h, rw_ref[...], preferred_element_type=F32, precision=lax.Precision.HIGHEST)
    col = lax.broadcasted_iota(jnp.int32, logits.shape, 1)
    neg = jnp.float32(-jnp.inf)
    l1 = jnp.where(col < n_experts, logits, neg)
    m1 = jnp.max(l1, axis=-1, keepdims=True)
    i1 = jnp.min(jnp.where(l1 == m1, col, LANES), axis=-1, keepdims=True)
    l2 = jnp.where(col == i1, neg, l1)
    m2 = jnp.max(l2, axis=-1, keepdims=True)
    i2 = jnp.min(jnp.where(l2 == m2, col, LANES), axis=-1, keepdims=True)
    e2 = jnp.exp(m2 - m1)
    g1 = 1.0 / (1.0 + e2)
    g2 = e2 * g1
    route = jnp.where(col == 0, i1.astype(F32),
                      jnp.where(col == 1, i2.astype(F32),
                                jnp.where(col == 2, g1, jnp.where(col == 3, g2, 0.0))))
    route_ref[...] = route


def _mod_norm(x, w, shift, scale, out_dtype=BF16):
    m, d = x.shape
    tr = _tile(m, 256, SUBLANES)
    vec = pl.BlockSpec((1, d), lambda i: (0, 0))
    return pl.pallas_call(
        _norm_kernel,
        grid=(m // tr,),
        in_specs=[pl.BlockSpec((tr, d), lambda i: (i, 0)), vec, vec, vec],
        out_specs=pl.BlockSpec((tr, d), lambda i: (i, 0)),
        out_shape=jax.ShapeDtypeStruct((m, d), out_dtype),
        compiler_params=_params("parallel"),
        name="mod_norm",
    )(x, w.reshape(1, d), shift.reshape(1, d), scale.reshape(1, d))


def _final_norm(x, w):
    m, d = x.shape
    tr = _tile(m, 256, SUBLANES)
    return pl.pallas_call(
        _final_norm_kernel,
        grid=(m // tr,),
        in_specs=[pl.BlockSpec((tr, d), lambda i: (i, 0)), pl.BlockSpec((1, d), lambda i: (0, 0))],
        out_specs=pl.BlockSpec((tr, d), lambda i: (i, 0)),
        out_shape=jax.ShapeDtypeStruct((m, d), x.dtype),
        compiler_params=_params("parallel"),
        name="final_norm",
    )(x, w.reshape(1, d))


def _mod_norm_router(x, w, shift, scale, router_w):
    m, d = x.shape
    n_experts = router_w.shape[1]
    tr = _tile(m, 256, SUBLANES)
    rw = jnp.zeros((d, LANES), F32).at[:, :n_experts].set(router_w.astype(F32))
    vec = pl.BlockSpec((1, d), lambda i: (0, 0))
    return pl.pallas_call(
        functools.partial(_norm_router_kernel, n_experts=n_experts),
        grid=(m // tr,),
        in_specs=[pl.BlockSpec((tr, d), lambda i: (i, 0)), vec, vec, vec,
                  pl.BlockSpec((d, LANES), lambda i: (0, 0))],
        out_specs=[pl.BlockSpec((tr, d), lambda i: (i, 0)),
                   pl.BlockSpec((tr, LANES), lambda i: (i, 0))],
        out_shape=[jax.ShapeDtypeStruct((m, d), BF16), jax.ShapeDtypeStruct((m, LANES), F32)],
        compiler_params=_params("parallel"),
        name="mod_norm_router",
    )(x, w.reshape(1, d), shift.reshape(1, d), scale.reshape(1, d), rw)


def _mm_kernel(a_ref, b_ref, o_ref):
    o_ref[...] = jnp.dot(a_ref[...], b_ref[...], preferred_element_type=F32).astype(o_ref.dtype)


def _mm_residual_kernel(a_ref, b_ref, res_ref, gate_ref, o_ref):
    acc = jnp.dot(a_ref[...], b_ref[...], preferred_element_type=F32)
    o_ref[...] = res_ref[...] + gate_ref[...] * acc


def _mm_swiglu_kernel(a_ref, b1_ref, b3_ref, o_ref):
    a = a_ref[...]
    g = jnp.dot(a, b1_ref[...], preferred_element_type=F32)
    u = jnp.dot(a, b3_ref[...], preferred_element_type=F32)
    o_ref[...] = (g * jax.nn.sigmoid(g) * u).astype(o_ref.dtype)


def _mm_tiles(m, k, n, tm, tn):
    tm = _tile(m, tm, SUBLANES)
    tn = _tile(n, tn, LANES)
    return tm, tn


def _mm(a, b, *, out_dtype=F32, tm=512, tn=1024):
    m, k = a.shape
    n = b.shape[1]
    tm, tn = _mm_tiles(m, k, n, tm, tn)
    return pl.pallas_call(
        _mm_kernel,
        grid=(n // tn, m // tm),
        in_specs=[pl.BlockSpec((tm, k), lambda j, i: (i, 0)),
                  pl.BlockSpec((k, tn), lambda j, i: (0, j))],
        out_specs=pl.BlockSpec((tm, tn), lambda j, i: (i, j)),
        out_shape=jax.ShapeDtypeStruct((m, n), out_dtype),
        compiler_params=_params("parallel", "parallel"),
        name="matmul",
    )(a, b)


def _mm_residual(a, b, res, gate, *, tm=512, tn=1024):
    m, k = a.shape
    n = b.shape[1]
    tm, tn = _mm_tiles(m, k, n, tm, tn)
    return pl.pallas_call(
        _mm_residual_kernel,
        grid=(n // tn, m // tm),
        in_specs=[pl.BlockSpec((tm, k), lambda j, i: (i, 0)),
                  pl.BlockSpec((k, tn), lambda j, i: (0, j)),
                  pl.BlockSpec((tm, tn), lambda j, i: (i, j)),
                  pl.BlockSpec((1, tn), lambda j, i: (0, j))],
        out_specs=pl.BlockSpec((tm, tn), lambda j, i: (i, j)),
        out_shape=jax.ShapeDtypeStruct((m, n), F32),
        compiler_params=_params("parallel", "parallel"),
        name="matmul_residual",
    )(a, b, res, gate.reshape(1, n))


def _mm_residual2_kernel(a1_ref, a2_ref, b_ref, res_ref, gate_ref, o_ref):
    k1 = a1_ref.shape[1]
    acc = jnp.dot(a1_ref[...], b_ref[:k1, :], preferred_element_type=F32)
    acc = acc + jnp.dot(a2_ref[...], b_ref[k1:, :], preferred_element_type=F32)
    o_ref[...] = res_ref[...] + gate_ref[...] * acc


def _mm_residual2(a1, a2, b, res, gate, *, tm=512, tn=1024):
    m, k1 = a1.shape
    k2 = a2.shape[1]
    n = b.shape[1]
    tm, tn = _mm_tiles(m, k1 + k2, n, tm, tn)
    return pl.pallas_call(
        _mm_residual2_kernel,
        grid=(n // tn, m // tm),
        in_specs=[pl.BlockSpec((tm, k1), lambda j, i: (i, 0)),
                  pl.BlockSpec((tm, k2), lambda j, i: (i, 0)),
                  pl.BlockSpec((k1 + k2, tn), lambda j, i: (0, j)),
                  pl.BlockSpec((tm, tn), lambda j, i: (i, j)),
                  pl.BlockSpec((1, tn), lambda j, i: (0, j))],
        out_specs=pl.BlockSpec((tm, tn), lambda j, i: (i, j)),
        out_shape=jax.ShapeDtypeStruct((m, n), F32),
        compiler_params=_params("parallel", "parallel"),
        name="matmul_residual2",
    )(a1, a2, b, res, gate.reshape(1, n))


def _mm_swiglu(a, b1, b3, *, tm=1024, tn=512):
    m, k = a.shape
    n = b1.shape[1]
    tm, tn = _mm_tiles(m, k, n, tm, tn)
    return pl.pallas_call(
        _mm_swiglu_kernel,
        grid=(m // tm, n // tn),
        in_specs=[pl.BlockSpec((tm, k), lambda i, j: (i, 0)),
                  pl.BlockSpec((k, tn), lambda i, j: (0, j)),
                  pl.BlockSpec((k, tn), lambda i, j: (0, j))],
        out_specs=pl.BlockSpec((tm, tn), lambda i, j: (i, j)),
        out_shape=jax.ShapeDtypeStruct((m, n), BF16),
        compiler_params=_params("parallel", "parallel"),
        name="matmul_swiglu",
    )(a, b1, b3)


def _moe_up_kernel(te_ref, nt_ref, a_ref, b1_ref, b3_ref, o_ref):
    @pl.when(pl.program_id(0) < nt_ref[0])
    def _():
        a = a_ref[...]
        g = jnp.dot(a, b1_ref[...], preferred_element_type=F32)
        u = jnp.dot(a, b3_ref[...], preferred_element_type=F32)
        o_ref[...] = (g * jax.nn.sigmoid(g) * u).astype(o_ref.dtype)


def _moe_down_kernel(te_ref, nt_ref, a_ref, b_ref, gate_ref, o_ref):
    @pl.when(pl.program_id(0) < nt_ref[0])
    def _():
        acc = jnp.dot(a_ref[...], b_ref[...], preferred_element_type=F32)
        o_ref[...] = gate_ref[...] * acc


def _moe_grouped(xg, gate_slot, tile_expert, n_tiles, w1, w3, w2, *, tm):
    s, d = xg.shape
    f = w1.shape[2]
    n_t = s // tm
    tf = _tile(f, 512, LANES)
    n_f = f // tf

    def row(i, nt):
        return jnp.minimum(i, nt[0] - 1)

    def col(i, j, nt, last):
        return jnp.where(i < nt[0], j, last)

    up = pl.pallas_call(
        _moe_up_kernel,
        grid_spec=pltpu.PrefetchScalarGridSpec(
            num_scalar_prefetch=2,
            grid=(n_t, n_f),
            in_specs=[
                pl.BlockSpec((tm, d), lambda i, j, te, nt: (row(i, nt), 0)),
                pl.BlockSpec((None, d, tf), lambda i, j, te, nt: (te[row(i, nt)], 0, col(i, j, nt, n_f - 1))),
                pl.BlockSpec((None, d, tf), lambda i, j, te, nt: (te[row(i, nt)], 0, col(i, j, nt, n_f - 1))),
            ],
            out_specs=pl.BlockSpec((tm, tf), lambda i, j, te, nt: (row(i, nt), col(i, j, nt, n_f - 1))),
        ),
        out_shape=jax.ShapeDtypeStruct((s, f), BF16),
        compiler_params=_params("arbitrary", "arbitrary"),
        name="moe_up",
    )(tile_expert, n_tiles, xg, w1, w3)

    td = _tile(d, 1024, LANES)
    n_d = d // td
    return pl.pallas_call(
        _moe_down_kernel,
        grid_spec=pltpu.PrefetchScalarGridSpec(
            num_scalar_prefetch=2,
            grid=(n_t, n_d),
            in_specs=[
                pl.BlockSpec((tm, f), lambda i, j, te, nt: (row(i, nt), 0)),
                pl.BlockSpec((None, f, td), lambda i, j, te, nt: (te[row(i, nt)], 0, col(i, j, nt, n_d - 1))),
                pl.BlockSpec((tm, 1), lambda i, j, te, nt: (row(i, nt), 0)),
            ],
            out_specs=pl.BlockSpec((tm, td), lambda i, j, te, nt: (row(i, nt), col(i, j, nt, n_d - 1))),
        ),
        out_shape=jax.ShapeDtypeStruct((s, d), F32),
        compiler_params=_params("arbitrary", "arbitrary"),
        name="moe_down",
    )(tile_expert, n_tiles, up, w2, gate_slot)


def _moe(h2, route, w1, w3, w2):
    n, d = h2.shape
    n_e = w1.shape[0]
    tm = min(MOE_TILE_ROWS, n)
    n_t = -(-(TOP_K * n + n_e * (tm - 1)) // tm)
    s = n_t * tm
    expert = route[:, :TOP_K].astype(jnp.int32).T.reshape(-1)
    gate = route[:, TOP_K:2 * TOP_K].T.reshape(-1)
    token = jnp.tile(jnp.arange(n, dtype=jnp.int32), TOP_K)
    onehot = (expert[:, None] == jnp.arange(n_e, dtype=jnp.int32)[None]).astype(jnp.int32)
    csum = jnp.cumsum(onehot, axis=0)
    rank = jnp.sum(onehot * csum, axis=1) - 1
    counts = csum[-1]
    padded = -(-counts // tm) * tm
    ends = jnp.cumsum(padded)
    starts = ends - padded
    slot = starts[expert] + rank
    token_of_slot = jnp.zeros((s,), jnp.int32).at[slot].set(token)
    gate_of_slot = jnp.zeros((s,), F32).at[slot].set(gate)
    n_tiles = (ends[-1] // tm).astype(jnp.int32).reshape(1)
    tile_start = jnp.arange(n_t, dtype=jnp.int32) * tm
    tile_expert = jnp.minimum(jnp.sum((tile_start[:, None] >= ends[None]).astype(jnp.int32), axis=1), n_e - 1)
    xg = jnp.take(h2, token_of_slot, axis=0)
    yo = _moe_grouped(xg, gate_of_slot.reshape(s, 1), tile_expert, n_tiles, w1, w3, w2, tm=tm)
    slots = slot.reshape(TOP_K, n)
    y = jnp.take(yo, slots[0], axis=0)
    for kk in range(1, TOP_K):
        y = y + jnp.take(yo, slots[kk], axis=0)
    return y


def _dwconv_kernel(prev_ref, cur_ref, next_ref, w_ref, b_ref, o_ref, *, pad_left, silu):
    i = pl.program_id(1)
    cur = cur_ref[...]
    rows = cur.shape[0]
    prev = jnp.where(i > 0, prev_ref[...], 0.0)
    nxt = jnp.where(i < pl.num_programs(1) - 1, next_ref[...], 0.0)
    ext = jnp.concatenate([prev, cur, nxt], axis=0)
    acc = jnp.zeros_like(cur) + b_ref[...]
    for j in range(w_ref.shape[0]):
        s = j - pad_left
        us = cur if s == 0 else pltpu.roll(ext, (-s) % ext.shape[0], axis=0)[SUBLANES:SUBLANES + rows]
        acc = acc + w_ref[j:j + 1, :] * us
    if silu:
        acc = acc * jax.nn.sigmoid(acc)
    o_ref[...] = acc


def _dwconv(u, w, b, pad_left, *, col0=0, silu=False):
    n = u.shape[0]
    k, c = w.shape
    assert k - 1 <= SUBLANES
    tc = _tile(math.gcd(c, col0) if col0 else c, 256, LANES)
    off = col0 // tc
    tr = _tile(n, 512, SUBLANES)
    per = tr // SUBLANES
    last = n // SUBLANES - 1
    return pl.pallas_call(
        functools.partial(_dwconv_kernel, pad_left=pad_left, silu=silu),
        grid=(c // tc, n // tr),
        in_specs=[pl.BlockSpec((SUBLANES, tc), lambda j, i: (jnp.maximum(i * per - 1, 0), j + off)),
                  pl.BlockSpec((tr, tc), lambda j, i: (i, j + off)),
                  pl.BlockSpec((SUBLANES, tc), lambda j, i: (jnp.minimum((i + 1) * per, last), j + off)),
                  pl.BlockSpec((k, tc), lambda j, i: (0, j)),
                  pl.BlockSpec((1, tc), lambda j, i: (0, j))],
        out_specs=pl.BlockSpec((tr, tc), lambda j, i: (i, j)),
        out_shape=jax.ShapeDtypeStruct((n, c), F32),
        compiler_params=_params("parallel", "parallel"),
        name="dwconv",
    )(u, u, u, w.astype(F32), b.astype(F32).reshape(1, c))


def _ssd_kernel(xf_ref, bf_ref, cf_ref, dtf_ref, xb_ref, bb_ref, cb_ref, dtb_ref, bias_ref, a_ref, s0_ref,
                yf_ref, yb_ref, st_ref, state, *, heads_per_group, head_dim):
    step = pl.program_id(1)

    @pl.when(step == 0)
    def _():
        state[...] = s0_ref[...]

    q = xf_ref.shape[0]
    row = lax.broadcasted_iota(jnp.int32, (q, q), 0)
    col = lax.broadcasted_iota(jnp.int32, (q, q), 1)
    tri = (row >= col).astype(F32)
    neg = jnp.float32(-jnp.inf)
    streams = ((xf_ref, bf_ref, cf_ref, dtf_ref, yf_ref), (xb_ref, bb_ref, cb_ref, dtb_ref, yb_ref))
    for d, (x_ref, b_ref, c_ref, dt_ref, y_ref) in enumerate(streams):
        xs = x_ref[...]
        bm = b_ref[...].astype(BF16)
        cm = c_ref[...].astype(BF16)
        dt = jax.nn.softplus(dt_ref[...] + bias_ref[...])
        da = dt * a_ref[...]
        cum = jnp.dot(tri, da, preferred_element_type=F32, precision=lax.Precision.HIGHEST)
        total = cum[q - 1:q, :]
        pos = cum if d == 0 else cum - da
        pos_t = pos.T
        cb = lax.dot_general(cm, bm, (((1,), (1,)), ((), ())), preferred_element_type=F32)
        mask = (row >= col) if d == 0 else (col >= row)
        for e in range(heads_per_group):
            lane = d * heads_per_group + e
            p_col = pos[:, lane:lane + 1]
            p_row = pos_t[lane:lane + 1, :]
            tot = total[:, lane:lane + 1]
            expo = (p_col - p_row) if d == 0 else (p_row - p_col)
            dec = jnp.exp(jnp.where(mask, expo, neg))
            xd = xs[:, e * head_dim:(e + 1) * head_dim] * dt[:, lane:lane + 1]
            y_diag = jnp.dot((cb * dec).astype(BF16), xd.astype(BF16), preferred_element_type=F32)
            st = state[d, e]
            if d == 0:
                carry_in, carry_out = jnp.exp(p_col), jnp.exp(tot - p_col)
            else:
                carry_in, carry_out = jnp.exp(tot - p_col), jnp.exp(p_col)
            y_off = lax.dot_general(cm, st.astype(BF16), (((1,), (1,)), ((), ())), preferred_element_type=F32)
            y_ref[:, e * head_dim:(e + 1) * head_dim] = y_diag + y_off * carry_in
            upd = lax.dot_general((xd * carry_out).astype(BF16), bm, (((0,), (0,)), ((), ())),
                                  preferred_element_type=F32)
            state[d, e] = jnp.exp(tot) * st + upd

    @pl.when(step == pl.num_programs(1) - 1)
    def _():
        st_ref[...] = state[...]


def _ssd_scan(xbc, dtg, bias_g, a_g, s0, *, ssd_w, groups):
    n = xbc.shape[0]
    e = ssd_w // (SSD_P * groups)
    q = min(SSD_CHUNK, n)
    steps = n // q
    gw = e * SSD_P
    nb = ssd_w // SSD_N

    def fwd(blk):
        return lambda g, s: (s, blk(g))

    def bwd(blk):
        return lambda g, s: (steps - 1 - s, blk(g))

    def stream(order):
        return [pl.BlockSpec((q, gw), order(lambda g: g)),
                pl.BlockSpec((q, SSD_N), order(lambda g: nb + g)),
                pl.BlockSpec((q, SSD_N), order(lambda g: nb + groups + g))]

    grp = pl.BlockSpec((None, 1, LANES), lambda g, s: (g, 0, 0))
    st_spec = pl.BlockSpec((2, None, e, SSD_P, SSD_N), lambda g, s: (0, g, 0, 0, 0))
    return pl.pallas_call(
        functools.partial(_ssd_kernel, heads_per_group=e, head_dim=SSD_P),
        grid=(groups, steps),
        in_specs=(stream(fwd) + [pl.BlockSpec((q, LANES), fwd(lambda g: g))]
                  + stream(bwd) + [pl.BlockSpec((q, LANES), bwd(lambda g: g))]
                  + [grp, grp, st_spec]),
        out_specs=[pl.BlockSpec((q, gw), fwd(lambda g: g)), pl.BlockSpec((q, gw), bwd(lambda g: g)), st_spec],
        out_shape=[jax.ShapeDtypeStruct((n, ssd_w), F32), jax.ShapeDtypeStruct((n, ssd_w), F32),
                   jax.ShapeDtypeStruct(s0.shape, F32)],
        scratch_shapes=[pltpu.VMEM((2, e, SSD_P, SSD_N), F32)],
        compiler_params=_params("parallel", "arbitrary"),
        name="ssd_scan",
    )(xbc, xbc, xbc, dtg, xbc, xbc, xbc, dtg, bias_g, a_g, s0)


def _ssd_gate_norm_kernel(yf_ref, yb_ref, xs_ref, z_ref, d_ref, nw_ref, o_ref, *, groups):
    y = yf_ref[...] + yb_ref[...] + xs_ref[...] * d_ref[...]
    z = z_ref[...]
    y = y * (z * jax.nn.sigmoid(z))
    gw = y.shape[1] // groups
    for g in range(groups):
        yg = y[:, g * gw:(g + 1) * gw]
        yg = yg * lax.rsqrt(jnp.mean(yg * yg, axis=-1, keepdims=True) + NORM_EPS)
        o_ref[:, g * gw:(g + 1) * gw] = (yg * nw_ref[:, g * gw:(g + 1) * gw]).astype(o_ref.dtype)


def _ssd_gate_norm(y_f, y_b, xbc, zx, d_cols, norm_w, *, groups):
    n, w = y_f.shape
    tr = _tile(n, 256, SUBLANES)
    rows = pl.BlockSpec((tr, w), lambda i: (i, 0))
    vec = pl.BlockSpec((1, w), lambda i: (0, 0))
    return pl.pallas_call(
        functools.partial(_ssd_gate_norm_kernel, groups=groups),
        grid=(n // tr,),
        in_specs=[rows, rows, rows, rows, vec, vec],
        out_specs=rows,
        out_shape=jax.ShapeDtypeStruct((n, w), BF16),
        compiler_params=_params("parallel"),
        name="ssd_gate_norm",
    )(y_f, y_b, xbc, zx, d_cols.reshape(1, w), norm_w.astype(F32).reshape(1, w))


def _na_kernel(q_ref, k_ref, v_ref, kc_ref, vc_ref, bias_ref, o_ref, *, grid_rows, win_h, rows_per_block, scale):
    blk = pl.program_id(1)
    kc = kc_ref[...]
    vc = vc_ref[...]
    nt = (((1,), (1,)), ((), ()))

    def one_row(i, carry):
        r = blk * rows_per_block + i
        rs = jnp.clip(r - win_h // 2, 0, grid_rows - win_h)
        q = q_ref[pl.ds(pl.multiple_of(i * GRID_W, GRID_W), GRID_W), :]
        start = pl.multiple_of(rs * GRID_W, GRID_W)
        kw = k_ref[pl.ds(start, win_h * GRID_W), :]
        vw = v_ref[pl.ds(start, win_h * GRID_W), :]
        s = lax.dot_general(q, kw, nt, preferred_element_type=F32) * scale + bias_ref[r - rs]
        sc = lax.dot_general(q, kc, nt, preferred_element_type=F32) * scale
        m = jnp.maximum(jnp.max(s, axis=-1, keepdims=True), jnp.max(sc, axis=-1, keepdims=True))
        p = jnp.exp(s - m)
        pc = jnp.exp(sc - m)
        denom = jnp.sum(p, axis=-1, keepdims=True) + jnp.sum(pc, axis=-1, keepdims=True)
        o = (jnp.dot(p.astype(BF16), vw, preferred_element_type=F32)
             + jnp.dot(pc.astype(BF16), vc, preferred_element_type=F32))
        o_ref[pl.ds(pl.multiple_of(i * GRID_W, GRID_W), GRID_W), :] = (o / denom).astype(o_ref.dtype)
        return carry

    lax.fori_loop(0, rows_per_block, one_row, 0)


def _na_bias_table(rpb, win_h):
    p = np.arange(win_h)[:, None, None, None]
    a = np.arange(win_h)[None, None, :, None]
    qc = np.arange(GRID_W)[None, :, None, None]
    kc = np.arange(GRID_W)[None, None, None, :]
    drow = np.broadcast_to(a - p + NA_WIN_H - 1, (win_h, GRID_W, win_h, GRID_W))
    dcol = np.broadcast_to(np.clip(kc - qc + NA_WIN_W - 1, 0, 2 * NA_WIN_W - 2), drow.shape)
    cstart = np.clip(qc - NA_WIN_W // 2, 0, GRID_W - NA_WIN_W)
    ok = np.broadcast_to((kc >= cstart) & (kc < cstart + NA_WIN_W), drow.shape)
    bias = jnp.where(ok[None], rpb.astype(F32)[:, drow, dcol], -jnp.inf)
    return bias.reshape(rpb.shape[0], win_h, GRID_W, win_h * GRID_W)


def _neighbourhood_attention(qkv, qkv_c, rpb, *, heads):
    n = qkv.shape[0]
    c = qkv_c.shape[0]
    grid_rows = n // GRID_W
    win_h = min(NA_WIN_H, grid_rows)
    rows_per_block = _tile(grid_rows, 8, 1)
    tq = rows_per_block * GRID_W
    bias = _na_bias_table(rpb, win_h)
    return pl.pallas_call(
        functools.partial(_na_kernel, grid_rows=grid_rows, win_h=win_h, rows_per_block=rows_per_block,
                          scale=NA_DH ** -0.5),
        grid=(heads, n // tq),
        in_specs=[pl.BlockSpec((tq, NA_DH), lambda h, i: (i, h)),
                  pl.BlockSpec((n, NA_DH), lambda h, i: (0, heads + h)),
                  pl.BlockSpec((n, NA_DH), lambda h, i: (0, 2 * heads + h)),
                  pl.BlockSpec((c, NA_DH), lambda h, i: (0, heads + h)),
                  pl.BlockSpec((c, NA_DH), lambda h, i: (0, 2 * heads + h)),
                  pl.BlockSpec((None, win_h, GRID_W, win_h * GRID_W), lambda h, i: (h, 0, 0, 0))],
        out_specs=pl.BlockSpec((tq, NA_DH), lambda h, i: (i, h)),
        out_shape=jax.ShapeDtypeStruct((n, heads * NA_DH), BF16),
        compiler_params=_params("parallel", "arbitrary"),
        name="neighbourhood_attention",
    )(qkv, qkv, qkv, qkv_c, qkv_c, bias)


def _ctx_attn_kernel(q_ref, k_ref, v_ref, o_ref, *, scale):
    s = lax.dot_general(q_ref[...], k_ref[...], (((1,), (1,)), ((), ())), preferred_element_type=F32) * scale
    p = jnp.exp(s - jnp.max(s, axis=-1, keepdims=True))
    o = jnp.dot(p.astype(BF16), v_ref[...], preferred_element_type=F32)
    o_ref[...] = (o / jnp.sum(p, axis=-1, keepdims=True)).astype(o_ref.dtype)


def _context_attention(qkv_c, *, heads):
    c = qkv_c.shape[0]
    return pl.pallas_call(
        functools.partial(_ctx_attn_kernel, scale=NA_DH ** -0.5),
        grid=(heads,),
        in_specs=[pl.BlockSpec((c, NA_DH), lambda h: (0, h)),
                  pl.BlockSpec((c, NA_DH), lambda h: (0, heads + h)),
                  pl.BlockSpec((c, NA_DH), lambda h: (0, 2 * heads + h))],
        out_specs=pl.BlockSpec((c, NA_DH), lambda h: (0, h)),
        out_shape=jax.ShapeDtypeStruct((c, heads * NA_DH), BF16),
        compiler_params=_params("parallel"),
        name="context_attention",
    )(qkv_c, qkv_c, qkv_c)


def _rglru_kernel(*refs, chunk, emit_y):
    if emit_y:
        u_ref, wg_ref, bg_ref, lam_ref, h0_ref, gate_ref, y_ref, ht_ref, hsum = refs
    else:
        u_ref, wg_ref, bg_ref, lam_ref, h0_ref, ht_ref = refs
    n, w = u_ref.shape
    n_chunks = n // chunk
    tiles = chunk // SUBLANES
    sub = lax.broadcasted_iota(jnp.int32, (chunk, w), 0) % SUBLANES
    log_sig = jax.nn.log_sigmoid(lam_ref[...])

    def chunk_scan(d, c0, carry):
        u = u_ref[pl.ds(c0, chunk), :]
        ub = u.astype(BF16)
        r = jax.nn.sigmoid(jnp.dot(ub, wg_ref[d, 0], preferred_element_type=F32) + bg_ref[d, 0])
        i = jax.nn.sigmoid(jnp.dot(ub, wg_ref[d, 1], preferred_element_type=F32) + bg_ref[d, 1])
        log_a = RG_C * r * log_sig[d]
        a = jnp.exp(log_a)
        b = jnp.sqrt(jnp.maximum(1.0 - a * a, 0.0)) * i * u
        for sh in (1, 2, 4):
            if d == 0:
                a_s, b_s, edge = pltpu.roll(a, sh, axis=0), pltpu.roll(b, sh, axis=0), sub < sh
            else:
                a_s, b_s = pltpu.roll(a, chunk - sh, axis=0), pltpu.roll(b, chunk - sh, axis=0)
                edge = sub >= SUBLANES - sh
            b = jnp.where(edge, b, a * b_s + b)
            a = jnp.where(edge, a, a * a_s)
        hs = [None] * tiles
        order = range(tiles) if d == 0 else range(tiles - 1, -1, -1)
        for k in order:
            sl = slice(k * SUBLANES, (k + 1) * SUBLANES)
            h = a[sl] * carry + b[sl]
            hs[k] = h
            carry = h[SUBLANES - 1:SUBLANES] if d == 0 else h[0:1]
        return jnp.concatenate(hs, axis=0), carry

    def step(j, carries, second_pass):
        cf, cb = carries
        rows_f = pl.ds(pl.multiple_of(j * chunk, chunk), chunk)
        rows_b = pl.ds(pl.multiple_of((n_chunks - 1 - j) * chunk, chunk), chunk)
        hf, cf = chunk_scan(0, pl.multiple_of(j * chunk, chunk), cf)
        hb, cb = chunk_scan(1, pl.multiple_of((n_chunks - 1 - j) * chunk, chunk), cb)
        if emit_y:
            if second_pass:
                y_ref[rows_f, :] = (jax.nn.gelu(gate_ref[rows_f, :]) * (hsum[rows_f, :] + hf)).astype(y_ref.dtype)
                y_ref[rows_b, :] = (jax.nn.gelu(gate_ref[rows_b, :]) * (hsum[rows_b, :] + hb)).astype(y_ref.dtype)
            else:
                hsum[rows_f, :] = hf
                hsum[rows_b, :] = hb
        return cf, cb

    carries = (h0_ref[0], h0_ref[1])
    half = n_chunks // 2
    carries = lax.fori_loop(0, half, lambda j, c: step(j, c, False), carries)
    carries = lax.fori_loop(half, n_chunks, lambda j, c: step(j, c, True), carries)
    ht_ref[0] = carries[0]
    ht_ref[1] = carries[1]


def _rglru(u, gate_src, gate_w, gate_b, lam, h0):
    n, w = u.shape
    nb, bw = gate_w.shape[2], gate_w.shape[3]
    chunk = _tile(n // 2, 256, SUBLANES)
    assert (n // chunk) % 2 == 0, "the two sweeps hand over at the middle chunk boundary"
    emit_y = gate_src is not None
    col = lambda j: (0, j)
    in_specs = [pl.BlockSpec((n, bw), col),
                pl.BlockSpec((2, 2, None, bw, bw), lambda j: (0, 0, j, 0, 0)),
                pl.BlockSpec((2, 2, None, 1, bw), lambda j: (0, 0, j, 0, 0)),
                pl.BlockSpec((2, 1, bw), lambda j: (0, 0, j)),
                pl.BlockSpec((2, 1, bw), lambda j: (0, 0, j))]
    args = [u, gate_w.astype(BF16), gate_b.astype(F32).reshape(2, 2, nb, 1, bw), lam.astype(F32).reshape(2, 1, w),
            h0.reshape(2, 1, w)]
    ht_spec = pl.BlockSpec((2, 1, bw), lambda j: (0, 0, j))
    ht_shape = jax.ShapeDtypeStruct((2, 1, w), F32)
    if emit_y:
        in_specs.append(pl.BlockSpec((n, bw), col))
        args.append(gate_src)
        out_specs = [pl.BlockSpec((n, bw), col), ht_spec]
        out_shape = [jax.ShapeDtypeStruct((n, w), BF16), ht_shape]
        scratch = [pltpu.VMEM((n, bw), F32)]
    else:
        out_specs, out_shape, scratch = ht_spec, ht_shape, []
    out = pl.pallas_call(
        functools.partial(_rglru_kernel, chunk=chunk, emit_y=emit_y),
        grid=(nb,),
        in_specs=in_specs,
        out_specs=out_specs,
        out_shape=out_shape,
        scratch_shapes=scratch,
        compiler_params=_params("parallel"),
        name="rglru",
    )(*args)
    if emit_y:
        return out[0], out[1].reshape(2, w)
    return None, out.reshape(2, w)


def _dft_tables(n):
    n1 = 1 << (int(math.log2(n)) // 2)
    n2 = n // n1
    assert n1 * n2 == n and n2 % 2 == 0
    i1, i2 = np.arange(n1), np.arange(n2)
    f1 = np.exp(-2j * np.pi * np.outer(i1, i1) / n1)
    f2 = np.exp(-2j * np.pi * np.outer(i2, i2) / n2)
    tw = np.exp(-2j * np.pi * np.outer(i2, i1) / n)
    return n1, n2, f1, f2, tw


def _stack_ri(m, sign=1.0):
    return jnp.asarray(np.concatenate([m.real, sign * m.imag], axis=0), BF16)


def _hy_hidden_kernel(feat_ref, w_in_ref, b_in_ref, w_mid_ref, b_mid_ref, freq_ref, o_ref):
    hp = lax.Precision.HIGHEST
    freq = freq_ref[...]
    hid = jnp.sin(freq * (jnp.dot(feat_ref[...], w_in_ref[...], preferred_element_type=F32, precision=hp)
                          + b_in_ref[...]))
    for m in range(w_mid_ref.shape[0]):
        hid = jnp.sin(freq * (jnp.dot(hid, w_mid_ref[m], preferred_element_type=F32, precision=hp)
                              + b_mid_ref[m]))
    o_ref[...] = hid


def _hy_filter_kernel(hid_ref, t_ref, w_out_ref, delta_ref, o_ref):
    acc = jnp.dot(hid_ref[...].astype(BF16), w_out_ref[...], preferred_element_type=F32)
    o_ref[...] = acc * jnp.exp(-t_ref[...] * jnp.abs(delta_ref[...]))


def _hyena_filters(L, w_in, b_in, w_mid, b_mid, w_out, freq, deltas):
    t = jnp.linspace(0.0, 1.0, L, dtype=F32)[:, None]
    ang = ((2.0 * math.pi / L) * jnp.arange(L, dtype=F32)[:, None]
           * jnp.linspace(1e-4, HY_BANDS - 1, HY_BANDS, dtype=F32)[None])
    emb = w_in.shape[0]
    ffn = w_in.shape[1]
    feats = jnp.zeros((L, LANES), F32).at[:, :emb].set(jnp.concatenate([t, jnp.cos(ang), -jnp.sin(ang)], axis=-1))
    w_in_p = jnp.zeros((LANES, ffn), F32).at[:emb].set(w_in.astype(F32))
    tr = _tile(L, 512, SUBLANES)
    full = lambda *shape: pl.BlockSpec(shape, lambda *_: (0,) * len(shape))
    hid = pl.pallas_call(
        _hy_hidden_kernel,
        grid=(L // tr,),
        in_specs=[pl.BlockSpec((tr, LANES), lambda i: (i, 0)), full(LANES, ffn), full(1, ffn),
                  full(HY_INNER, ffn, ffn), full(HY_INNER, 1, ffn), full(1, ffn)],
        out_specs=pl.BlockSpec((tr, ffn), lambda i: (i, 0)),
        out_shape=jax.ShapeDtypeStruct((L, ffn), F32),
        compiler_params=_params("parallel"),
        name="hyena_filter_hidden",
    )(feats, w_in_p, b_in.astype(F32).reshape(1, ffn), w_mid.astype(F32),
      b_mid.astype(F32).reshape(HY_INNER, 1, ffn), freq.astype(F32).reshape(1, ffn))
    n_out = w_out.shape[1]
    tn = _tile(n_out, 2048, LANES)
    return pl.pallas_call(
        _hy_filter_kernel,
        grid=(L // tr, n_out // tn),
        in_specs=[pl.BlockSpec((tr, ffn), lambda i, j: (i, 0)), pl.BlockSpec((tr, 1), lambda i, j: (i, 0)),
                  pl.BlockSpec((ffn, tn), lambda i, j: (0, j)), pl.BlockSpec((1, tn), lambda i, j: (0, j))],
        out_specs=pl.BlockSpec((tr, tn), lambda i, j: (i, j)),
        out_shape=jax.ShapeDtypeStruct((L, n_out), F32),
        compiler_params=_params("parallel", "parallel"),
        name="hyena_filter_out",
    )(hid, t, w_out.astype(BF16), deltas.astype(F32).reshape(1, n_out))


def _hy_stage1_kernel(x_ref, f2_ref, twr_ref, twi_ref, o_ref):
    n2 = o_ref.shape[1]
    p = jnp.dot(f2_ref[...], x_ref[...].astype(BF16), preferred_element_type=F32)
    ar, ai = p[:n2], p[n2:]
    tr, ti = twr_ref[...], twi_ref[...]
    o_ref[0] = (ar * tr - ai * ti).astype(o_ref.dtype)
    o_ref[1] = (ar * ti + ai * tr).astype(o_ref.dtype)


def _hy_stage1(src, part, n_parts, c, tabs, *, conj):
    n1, n2, _, f2, tw = tabs
    sign = -1.0 if conj else 1.0
    half = n2 // 2
    src2d = src.reshape(half, n1 * n_parts * c)
    f2s = _stack_ri(f2[:, :half], sign)
    twr = jnp.asarray(tw.real.T.reshape(n1, n2, 1), F32)
    twi = jnp.asarray(sign * tw.imag.T.reshape(n1, n2, 1), F32)
    twspec = pl.BlockSpec((None, n2, 1), lambda i: (i, 0, 0))
    return pl.pallas_call(
        _hy_stage1_kernel,
        grid=(n1,),
        in_specs=[pl.BlockSpec((half, c), lambda i: (0, i * n_parts + part)),
                  pl.BlockSpec((2 * n2, half), lambda i: (0, 0)), twspec, twspec],
        out_specs=pl.BlockSpec((2, n2, c), lambda i: (0, 0, i)),
        out_shape=jax.ShapeDtypeStruct((2, n2, n1 * c), BF16),
        compiler_params=_params("parallel"),
        name="hyena_dft_stage1",
    )(src2d, f2s, twr, twi).reshape(2, n2, n1, c)


def _hy_filter_spectrum_kernel(af_ref, ab_ref, f1_ref, h_ref):
    n1 = h_ref.shape[1]
    f1 = f1_ref[...]
    p = jnp.dot(f1, af_ref[0], preferred_element_type=F32)
    q = jnp.dot(f1, af_ref[1], preferred_element_type=F32)
    r = jnp.dot(f1, ab_ref[0], preferred_element_type=F32)
    s = jnp.dot(f1, ab_ref[1], preferred_element_type=F32)
    h_ref[0] = p[:n1] - q[n1:] + r[:n1] + s[n1:]
    h_ref[1] = p[n1:] + q[:n1] + s[:n1] - r[n1:]


def _hy_mid_kernel(a_ref, h_ref, f1_ref, twr_ref, twi_ref, o_ref):
    n1 = o_ref.shape[1]
    f1 = f1_ref[...]
    p = jnp.dot(f1, a_ref[0], preferred_element_type=F32)
    q = jnp.dot(f1, a_ref[1], preferred_element_type=F32)
    xr = p[:n1] - q[n1:]
    xi = p[n1:] + q[:n1]
    hr, hi = h_ref[0], h_ref[1]
    yr = (xr * hr - xi * hi).astype(BF16)
    yi = (xr * hi + xi * hr).astype(BF16)
    p = jnp.dot(f1, yr, preferred_element_type=F32)
    q = jnp.dot(f1, yi, preferred_element_type=F32)
    br = p[:n1] + q[n1:]
    bi = q[:n1] - p[n1:]
    tr, ti = twr_ref[...], twi_ref[...]
    o_ref[0] = (br * tr + bi * ti).astype(o_ref.dtype)
    o_ref[1] = (bi * tr - br * ti).astype(o_ref.dtype)


def _hy_stage3_kernel(b_ref, f2_ref, z_ref, gate_ref, skip_ref, o_ref, *, inv_n):
    _, n2, c = b_ref.shape
    acc = jnp.dot(f2_ref[...], b_ref[...].reshape(2 * n2, c), preferred_element_type=F32)
    z = z_ref[...]
    o_ref[...] = (gate_ref[...] * (acc * inv_n + z * skip_ref[...])).astype(o_ref.dtype)


def _hy_long_conv(z_src, z_part, z_parts, gate_src, gate_part, gate_parts, filt, order, skip, tabs, out_dtype):
    n1, n2, f1, f2, tw = tabs
    L = z_src.shape[0]
    c = skip.shape[0]
    half = n2 // 2
    f1s = _stack_ri(f1)
    slab = pl.BlockSpec((2, None, n1, c), lambda k: (0, k, 0, 0))
    f1spec = pl.BlockSpec((2 * n1, n1), lambda k: (0, 0))
    af = _hy_stage1(filt, 2 * order, 2 * HY_ORDER, c, tabs, conj=False)
    ab = _hy_stage1(filt, 2 * order + 1, 2 * HY_ORDER, c, tabs, conj=True)
    spec = pl.pallas_call(
        _hy_filter_spectrum_kernel,
        grid=(n2,),
        in_specs=[slab, slab, f1spec],
        out_specs=slab,
        out_shape=jax.ShapeDtypeStruct((2, n2, n1, c), F32),
        compiler_params=_params("parallel"),
        name="hyena_filter_spectrum",
    )(af, ab, f1s)
    a = _hy_stage1(z_src, z_part, z_parts, c, tabs, conj=False)
    twspec = pl.BlockSpec((None, n1, 1), lambda k: (k, 0, 0))
    b = pl.pallas_call(
        _hy_mid_kernel,
        grid=(n2,),
        in_specs=[slab, slab, f1spec, twspec, twspec],
        out_specs=slab,
        out_shape=jax.ShapeDtypeStruct((2, n2, n1, c), BF16),
        compiler_params=_params("parallel"),
        name="hyena_spectral_product",
    )(a, spec, f1s, jnp.asarray(tw.real.reshape(n2, n1, 1), F32), jnp.asarray(tw.imag.reshape(n2, n1, 1), F32))
    f2c = jnp.asarray(np.concatenate([f2.real[:half], f2.imag[:half]], axis=1), BF16)
    out = pl.pallas_call(
        functools.partial(_hy_stage3_kernel, inv_n=1.0 / (n1 * n2)),
        grid=(n1,),
        in_specs=[pl.BlockSpec((2, n2, c), lambda i: (0, 0, i)),
                  pl.BlockSpec((half, 2 * n2), lambda i: (0, 0)),
                  pl.BlockSpec((half, c), lambda i: (0, i * z_parts + z_part)),
                  pl.BlockSpec((half, c), lambda i: (0, i * gate_parts + gate_part)),
                  pl.BlockSpec((1, c), lambda i: (0, 0))],
        out_specs=pl.BlockSpec((half, c), lambda i: (0, i)),
        out_shape=jax.ShapeDtypeStruct((half, n1 * c), out_dtype),
        compiler_params=_params("parallel"),
        name="hyena_dft_stage3",
    )(b.reshape(2, n2, n1 * c), f2c, z_src.reshape(half, n1 * z_parts * c),
      gate_src.reshape(half, n1 * gate_parts * c), skip.astype(F32).reshape(1, c))
    return out.reshape(L, c)


def _hyena(p, col0, conv_w, conv_b, w_in, b_in, w_mid, b_mid, w_out, freq, deltas, skip):
    assert HY_ORDER == 2
    L = p.shape[0]
    c = skip.shape[1]
    u = _dwconv(p, conv_w, conv_b, (HY_CONV - 1) // 2, col0=col0)
    filt = _hyena_filters(L, w_in, b_in, w_mid, b_mid, w_out, freq, deltas)
    tabs = _dft_tables(2 * L)
    z = _hy_long_conv(u, 2, 3, u, 0, 3, filt, 0, skip[0], tabs, F32)
    return _hy_long_conv(z, 0, 1, u, 1, 3, filt, 1, skip[1], tabs, BF16)


def _even_layer_mix(h, hc, in_w, conv_w, conv_b, dt_bias, a_log, d_skip, norm_w, rpb):
    d = h.shape[1]
    ssd_w = norm_w.shape[0]
    heads = ssd_w // SSD_P
    e = heads // SSD_G
    na_heads = (d - ssd_w) // NA_DH
    o1 = ssd_w
    o2 = o1 + conv_w.shape[1]
    o3 = o2 + 2 * heads
    w_zx = in_w[:, :o2].astype(BF16)
    w_qkv = in_w[:, o3:].astype(BF16)
    dt_cols = in_w[:, o2:o3].reshape(d, 2, SSD_G, e).transpose(0, 2, 1, 3).reshape(d, SSD_G, 2 * e)
    w_dt = jnp.zeros((d, SSD_G, LANES), BF16).at[:, :, :2 * e].set(dt_cols.astype(BF16)).reshape(d, SSD_G * LANES)

    def grouped(v):
        vg = v.astype(F32).reshape(2, SSD_G, e).transpose(1, 0, 2).reshape(SSD_G, 2 * e)
        return jnp.zeros((SSD_G, LANES), F32).at[:, :2 * e].set(vg).reshape(SSD_G, 1, LANES)

    bias_g = grouped(dt_bias)
    a_g = grouped(-jnp.exp(a_log.astype(F32)))
    d_cols = jnp.repeat(d_skip.astype(F32), SSD_P)

    def ssd(zx, dtg, s0):
        xbc = _dwconv(zx, conv_w, conv_b, SSD_CONV // 2, col0=o1, silu=True)
        y_f, y_b, s_t = _ssd_scan(xbc, dtg, bias_g, a_g, s0, ssd_w=ssd_w, groups=SSD_G)
        return _ssd_gate_norm(y_f, y_b, xbc, zx, d_cols, norm_w, groups=SSD_G), s_t

    zero = jnp.zeros((2, SSD_G, e, SSD_P, SSD_N), F32)
    qkv_c = _mm(hc, w_qkv, out_dtype=BF16)
    y_ssd_c, s_c = ssd(_mm(hc, w_zx), _mm(hc, w_dt), zero)
    qkv = _mm(h, w_qkv, out_dtype=BF16)
    y_ssd, _ = ssd(_mm(h, w_zx), _mm(h, w_dt), s_c)
    y_na = _neighbourhood_attention(qkv, qkv_c, rpb, heads=na_heads)
    y_na_c = _context_attention(qkv_c, heads=na_heads)
    return (y_ssd, y_na), (y_ssd_c, y_na_c)


def _odd_layer_mix(h, hc, in_w, conv_w, conv_b, gate_w, gate_b, lam, hy_params):
    rg_w = lam.shape[1]
    w = in_w.astype(BF16)
    pad = RG_CONV // 2
    u_c = _dwconv(_mm(hc, w[:, rg_w:2 * rg_w]), conv_w, conv_b, pad)
    _, h_c = _rglru(u_c, None, gate_w, gate_b, lam, jnp.zeros((2, rg_w), F32))
    proj = _mm(h, w)
    u = _dwconv(proj, conv_w, conv_b, pad, col0=rg_w)
    y_rg, _ = _rglru(u, proj, gate_w, gate_b, lam, h_c)
    return y_rg, _hyena(proj, 2 * rg_w, *hy_params)


def kernel(x, c, ctx, c_ctx, ada_w, ada_b, norm_mix_w, norm_ffn_w, norm_out_w, ev_in_w, ev_ssd_conv_w, ev_ssd_conv_b, ev_ssd_dt_bias, ev_ssd_a_log, ev_ssd_d, ev_ssd_norm_w, ev_na_rpb, ev_out_w, ev_ffn_w1, ev_ffn_w3, ev_ffn_w2, od_in_w, od_rg_conv_w, od_rg_conv_b, od_rg_gate_w, od_rg_gate_b, od_rg_lambda, od_hy_conv_w, od_hy_conv_b, od_hy_w_in, od_hy_b_in, od_hy_w_mid, od_hy_b_mid, od_hy_w_out, od_hy_freq, od_hy_deltas, od_hy_skip, od_out_w, od_router_w, od_moe_w1, od_moe_w3, od_moe_w2):
    batch, seq, d = x.shape
    assert batch == 1 and ada_w.shape[0] == 2, "kernel is written for one sequence and the even/odd layer pair"
    xs = x[0]
    cs = ctx[0]
    cvecs = jnp.concatenate([jax.nn.silu(c), jax.nn.silu(c_ctx)[None]], axis=0)

    mods = _ada(cvecs, ada_w[0], ada_b[0]).reshape(2, ADA_CHUNKS, d)
    mod, mod_c = mods[0], mods[1]
    h = _mod_norm(xs, norm_mix_w[0], mod[0], mod[1])
    hc = _mod_norm(cs, norm_mix_w[0], mod_c[0], mod_c[1])
    y, y_c = _even_layer_mix(h, hc, ev_in_w[0], ev_ssd_conv_w[0], ev_ssd_conv_b[0], ev_ssd_dt_bias[0],
                             ev_ssd_a_log[0], ev_ssd_d[0], ev_ssd_norm_w[0], ev_na_rpb[0])
    out_w = ev_out_w[0].astype(BF16)
    xs = _mm_residual2(*y, out_w, xs, mod[2])
    cs = _mm_residual2(*y_c, out_w, cs, mod_c[2])
    w1, w3, w2 = ev_ffn_w1[0].astype(BF16), ev_ffn_w3[0].astype(BF16), ev_ffn_w2[0].astype(BF16)
    h2 = _mod_norm(xs, norm_ffn_w[0], mod[3], mod[4])
    xs = _mm_residual(_mm_swiglu(h2, w1, w3), w2, xs, mod[5], tm=256, tn=512)
    h2c = _mod_norm(cs, norm_ffn_w[0], mod_c[3], mod_c[4])
    cs = _mm_residual(_mm_swiglu(h2c, w1, w3), w2, cs, mod_c[5], tm=256, tn=512)

    mods = _ada(cvecs, ada_w[1], ada_b[1]).reshape(2, ADA_CHUNKS, d)
    mod, mod_c = mods[0], mods[1]
    h = _mod_norm(xs, norm_mix_w[1], mod[0], mod[1])
    hc = _mod_norm(cs, norm_mix_w[1], mod_c[0], mod_c[1])
    hy_params = (od_hy_conv_w[0], od_hy_conv_b[0], od_hy_w_in[0], od_hy_b_in[0], od_hy_w_mid[0],
                 od_hy_b_mid[0], od_hy_w_out[0], od_hy_freq[0], od_hy_deltas[0], od_hy_skip[0])
    y = _odd_layer_mix(h, hc, od_in_w[0], od_rg_conv_w[0], od_rg_conv_b[0], od_rg_gate_w[0],
                       od_rg_gate_b[0], od_rg_lambda[0], hy_params)
    xs = _mm_residual2(*y, od_out_w[0].astype(BF16), xs, mod[2])
    h2, route = _mod_norm_router(xs, norm_ffn_w[1], mod[3], mod[4], od_router_w[0])
    ymoe = _moe(h2, route, od_moe_w1[0].astype(BF16), od_moe_w3[0].astype(BF16), od_moe_w2[0].astype(BF16))
    xs = xs + mod[5][None, :] * ymoe
    return _final_norm(xs, norm_out_w)[None]
```

```python
import functools
import math

import numpy as np
import jax
import jax.numpy as jnp
from jax import lax
from jax.experimental import pallas as pl
from jax.experimental.pallas import tpu as pltpu

GRID_W = 64
ADA_CHUNKS = 6
NORM_EPS = 1e-6
SSD_P = 64
SSD_G = 4
SSD_N = 128
SSD_CONV = 4
SSD_CHUNK = 128
NA_DH = 128
NA_WIN_H = 8
NA_WIN_W = 16
NA_QCOLS = 16
NA_KCOLS = NA_QCOLS + NA_WIN_W
RG_BLOCKS = 16
RG_CONV = 4
RG_C = 8.0
HY_ORDER = 2
HY_CONV = 3
HY_EMB = 33
HY_BANDS = (HY_EMB - 1) // 2
HY_INNER = 2
TOP_K = 2

LANES = 128
SUBLANES = 8
VMEM_LIMIT_BYTES = 56 * 1024 * 1024
MOE_TILE_ROWS = 512

F32 = jnp.float32
BF16 = jnp.bfloat16


def _tile(n, target, align):
    for t in range(min(n, target), 0, -1):
        if n % t == 0 and t % align == 0:
            return t
    return n


def _params(*semantics):
    return pltpu.CompilerParams(dimension_semantics=semantics, vmem_limit_bytes=VMEM_LIMIT_BYTES)


def _ada_kernel(cb_ref, w_ref, b_ref, o_ref, *, n_rows):
    k, tn = w_ref.shape
    for l in range(tn // LANES):
        w = w_ref[:, l * LANES:(l + 1) * LANES].reshape(k // SUBLANES, SUBLANES, LANES)
        for r in range(n_rows):
            cb = cb_ref[r].reshape(k // SUBLANES, SUBLANES, LANES)
            part = jnp.sum(w * cb, axis=0)
            o_ref[r:r + 1, l * LANES:(l + 1) * LANES] = (
                jnp.sum(part, axis=0, keepdims=True) + b_ref[:, l * LANES:(l + 1) * LANES])


def _ada(cvecs, w, b):
    r, k = cvecs.shape
    layers, _, n = w.shape
    tn = _tile(n, 512, LANES)
    cb = jnp.broadcast_to(cvecs[:, :, None], (r, k, LANES))
    return pl.pallas_call(
        functools.partial(_ada_kernel, n_rows=r),
        grid=(layers, n // tn),
        in_specs=[pl.BlockSpec((r, k, LANES), lambda l, j: (0, 0, 0)),
                  pl.BlockSpec((None, k, tn), lambda l, j: (l, 0, j)),
                  pl.BlockSpec((None, 1, tn), lambda l, j: (l, 0, j))],
        out_specs=pl.BlockSpec((None, r, tn), lambda l, j: (l, 0, j)),
        out_shape=jax.ShapeDtypeStruct((layers, r, n), F32),
        compiler_params=_params("parallel", "parallel"),
        name="ada_matvec",
    )(cb, w, b.reshape(layers, 1, n))


def _norm_rows(x, w, shift, scale):
    y = x * lax.rsqrt(jnp.mean(x * x, axis=-1, keepdims=True) + NORM_EPS)
    y = y * w
    if scale is not None:
        y = y * (1.0 + scale) + shift
    return y


def _norm_kernel(x_ref, w_ref, sh_ref, sc_ref, o_ref):
    o_ref[...] = _norm_rows(x_ref[...], w_ref[...], sh_ref[...], sc_ref[...]).astype(o_ref.dtype)


def _final_norm_kernel(x_ref, y1_ref, y2_ref, gate_ref, w_ref, o_ref):
    x = x_ref[...] + gate_ref[...] * (y1_ref[...] + y2_ref[...])
    o_ref[...] = _norm_rows(x, w_ref[...], None, None).astype(o_ref.dtype)


def _norm_router_kernel(x_ref, w_ref, sh_ref, sc_ref, rw_ref, o_ref, route_ref, *, n_experts):
    h = _norm_rows(x_ref[...], w_ref[...], sh_ref[...], sc_ref[...])
    o_ref[...] = h.astype(o_ref.dtype)
    logits = jnp.dot(h, rw_ref[...], preferred_element_type=F32, precision=lax.Precision.HIGHEST)
    col = lax.broadcasted_iota(jnp.int32, logits.shape, 1)
    neg = jnp.float32(-jnp.inf)
    l1 = jnp.where(col < n_experts, logits, neg)
    m1 = jnp.max(l1, axis=-1, keepdims=True)
    i1 = jnp.min(jnp.where(l1 == m1, col, LANES), axis=-1, keepdims=True)
    l2 = jnp.where(col == i1, neg, l1)
    m2 = jnp.max(l2, axis=-1, keepdims=True)
    i2 = jnp.min(jnp.where(l2 == m2, col, LANES), axis=-1, keepdims=True)
    e2 = jnp.exp(m2 - m1)
    g1 = 1.0 / (1.0 + e2)
    g2 = e2 * g1
    route = jnp.where(col == 0, i1.astype(F32),
                      jnp.where(col == 1, i2.astype(F32),
                                jnp.where(col == 2, g1, jnp.where(col == 3, g2, 0.0))))
    route_ref[...] = route


def _mod_norm(x, w, shift, scale, out_dtype=BF16):
    m, d = x.shape
    tr = _tile(m, 256, SUBLANES)
    vec = pl.BlockSpec((1, d), lambda i: (0, 0))
    return pl.pallas_call(
        _norm_kernel,
        grid=(m // tr,),
        in_specs=[pl.BlockSpec((tr, d), lambda i: (i, 0)), vec, vec, vec],
        out_specs=pl.BlockSpec((tr, d), lambda i: (i, 0)),
        out_shape=jax.ShapeDtypeStruct((m, d), out_dtype),
        compiler_params=_params("parallel"),
        name="mod_norm",
    )(x, w.reshape(1, d), shift.reshape(1, d), scale.reshape(1, d))


def _final_norm(x, y1, y2, gate, w):
    m, d = x.shape
    tr = _tile(m, 256, SUBLANES)
    rows = pl.BlockSpec((tr, d), lambda i: (i, 0))
    vec = pl.BlockSpec((1, d), lambda i: (0, 0))
    return pl.pallas_call(
        _final_norm_kernel,
        grid=(m // tr,),
        in_specs=[rows, rows, rows, vec, vec],
        out_specs=rows,
        out_shape=jax.ShapeDtypeStruct((m, d), x.dtype),
        compiler_params=_params("parallel"),
        name="final_norm",
    )(x, y1, y2, gate.reshape(1, d), w.reshape(1, d))


def _mod_norm_router(x, w, shift, scale, router_w):
    m, d = x.shape
    n_experts = router_w.shape[1]
    tr = _tile(m, 256, SUBLANES)
    rw = jnp.zeros((d, LANES), F32).at[:, :n_experts].set(router_w.astype(F32))
    vec = pl.BlockSpec((1, d), lambda i: (0, 0))
    return pl.pallas_call(
        functools.partial(_norm_router_kernel, n_experts=n_experts),
        grid=(m // tr,),
        in_specs=[pl.BlockSpec((tr, d), lambda i: (i, 0)), vec, vec, vec,
                  pl.BlockSpec((d, LANES), lambda i: (0, 0))],
        out_specs=[pl.BlockSpec((tr, d), lambda i: (i, 0)),
                   pl.BlockSpec((tr, LANES), lambda i: (i, 0))],
        out_shape=[jax.ShapeDtypeStruct((m, d), BF16), jax.ShapeDtypeStruct((m, LANES), F32)],
        compiler_params=_params("parallel"),
        name="mod_norm_router",
    )(x, w.reshape(1, d), shift.reshape(1, d), scale.reshape(1, d), rw)


def _bf16(ref):
    return ref[...].astype(BF16)


def _mm_kernel(a_ref, b_ref, o_ref):
    o_ref[...] = jnp.dot(a_ref[...], _bf16(b_ref), preferred_element_type=F32).astype(o_ref.dtype)


def _mm_residual_kernel(a_ref, b_ref, res_ref, gate_ref, o_ref):
    acc = jnp.dot(a_ref[...], _bf16(b_ref), preferred_element_type=F32)
    o_ref[...] = res_ref[...] + gate_ref[...] * acc


def _mm_swiglu_kernel(a_ref, b1_ref, b3_ref, o_ref):
    a = a_ref[...]
    g = jnp.dot(a, _bf16(b1_ref), preferred_element_type=F32)
    u = jnp.dot(a, _bf16(b3_ref), preferred_element_type=F32)
    o_ref[...] = (g * jax.nn.sigmoid(g) * u).astype(o_ref.dtype)


def _mm_tiles(m, k, n, tm, tn):
    tm = _tile(m, tm, SUBLANES)
    tn = _tile(n, tn, LANES)
    return tm, tn


def _mm(a, b, *, out_dtype=F32, tm=1024, tn=512, col0=0, n=None):
    m, k = a.shape
    n = b.shape[1] - col0 if n is None else n
    tm, tn = _mm_tiles(m, k, math.gcd(n, col0) if col0 else n, tm, tn)
    off = col0 // tn
    return pl.pallas_call(
        _mm_kernel,
        grid=(n // tn, m // tm),
        in_specs=[pl.BlockSpec((tm, k), lambda j, i: (i, 0)),
                  pl.BlockSpec((k, tn), lambda j, i: (0, j + off))],
        out_specs=pl.BlockSpec((tm, tn), lambda j, i: (i, j)),
        out_shape=jax.ShapeDtypeStruct((m, n), out_dtype),
        compiler_params=_params("parallel", "parallel"),
        name="matmul",
    )(a, b)


def _mm_residual(a, b, res, gate, *, tm=512, tn=1024):
    m, k = a.shape
    n = b.shape[1]
    tm, tn = _mm_tiles(m, k, n, tm, tn)
    return pl.pallas_call(
        _mm_residual_kernel,
        grid=(n // tn, m // tm),
        in_specs=[pl.BlockSpec((tm, k), lambda j, i: (i, 0)),
                  pl.BlockSpec((k, tn), lambda j, i: (0, j)),
                  pl.BlockSpec((tm, tn), lambda j, i: (i, j)),
                  pl.BlockSpec((1, tn), lambda j, i: (0, j))],
        out_specs=pl.BlockSpec((tm, tn), lambda j, i: (i, j)),
        out_shape=jax.ShapeDtypeStruct((m, n), F32),
        compiler_params=_params("parallel", "parallel"),
        name="matmul_residual",
    )(a, b, res, gate.reshape(1, n))


def _mm_residual2_kernel(a1_ref, a2_ref, b_ref, res_ref, gate_ref, o_ref):
    k1 = a1_ref.shape[1]
    acc = jnp.dot(a1_ref[...], b_ref[:k1, :].astype(BF16), preferred_element_type=F32)
    acc = acc + jnp.dot(a2_ref[...], b_ref[k1:, :].astype(BF16), preferred_element_type=F32)
    o_ref[...] = res_ref[...] + gate_ref[...] * acc


def _mm_residual2(a1, a2, b, res, gate, *, tm=1024, tn=512):
    m, k1 = a1.shape
    k2 = a2.shape[1]
    n = b.shape[1]
    tm, tn = _mm_tiles(m, k1 + k2, n, tm, tn)
    return pl.pallas_call(
        _mm_residual2_kernel,
        grid=(n // tn, m // tm),
        in_specs=[pl.BlockSpec((tm, k1), lambda j, i: (i, 0)),
                  pl.BlockSpec((tm, k2), lambda j, i: (i, 0)),
                  pl.BlockSpec((k1 + k2, tn), lambda j, i: (0, j)),
                  pl.BlockSpec((tm, tn), lambda j, i: (i, j)),
                  pl.BlockSpec((1, tn), lambda j, i: (0, j))],
        out_specs=pl.BlockSpec((tm, tn), lambda j, i: (i, j)),
        out_shape=jax.ShapeDtypeStruct((m, n), F32),
        compiler_params=_params("parallel", "parallel"),
        name="matmul_residual2",
    )(a1, a2, b, res, gate.reshape(1, n))


def _mm_swiglu(a, b1, b3, *, tm=1024, tn=256):
    m, k = a.shape
    n = b1.shape[1]
    tm, tn = _mm_tiles(m, k, n, tm, tn)
    return pl.pallas_call(
        _mm_swiglu_kernel,
        grid=(m // tm, n // tn),
        in_specs=[pl.BlockSpec((tm, k), lambda i, j: (i, 0)),
                  pl.BlockSpec((k, tn), lambda i, j: (0, j)),
                  pl.BlockSpec((k, tn), lambda i, j: (0, j))],
        out_specs=pl.BlockSpec((tm, tn), lambda i, j: (i, j)),
        out_shape=jax.ShapeDtypeStruct((m, n), BF16),
        compiler_params=_params("parallel", "parallel"),
        name="matmul_swiglu",
    )(a, b1, b3)


def _moe_up_kernel(te_ref, nt_ref, a_ref, b1_ref, b3_ref, o_ref):
    @pl.when(pl.program_id(1) < nt_ref[0])
    def _():
        a = a_ref[...]
        g = jnp.dot(a, _bf16(b1_ref), preferred_element_type=F32)
        u = jnp.dot(a, _bf16(b3_ref), preferred_element_type=F32)
        o_ref[...] = (g * jax.nn.sigmoid(g) * u).astype(o_ref.dtype)


def _moe_down_kernel(te_ref, nt_ref, a_ref, b_ref, gate_ref, o_ref):
    @pl.when(pl.program_id(1) < nt_ref[0])
    def _():
        acc = jnp.dot(a_ref[...], _bf16(b_ref), preferred_element_type=F32)
        o_ref[...] = gate_ref[...] * acc


def _moe_grouped(xg, gate_slot, tile_expert, n_tiles, w1, w3, w2, *, tm):
    s, d = xg.shape
    f = w1.shape[2]
    n_t = s // tm
    tf = _tile(f, 512, LANES)
    td = _tile(d, 1024, LANES)

    def row(i, nt):
        return jnp.minimum(i, nt[0] - 1)

    up = pl.pallas_call(
        _moe_up_kernel,
        grid_spec=pltpu.PrefetchScalarGridSpec(
            num_scalar_prefetch=2,
            grid=(f // tf, n_t),
            in_specs=[
                pl.BlockSpec((tm, d), lambda j, i, te, nt: (row(i, nt), 0)),
                pl.BlockSpec((None, d, tf), lambda j, i, te, nt: (te[row(i, nt)], 0, j)),
                pl.BlockSpec((None, d, tf), lambda j, i, te, nt: (te[row(i, nt)], 0, j)),
            ],
            out_specs=pl.BlockSpec((tm, tf), lambda j, i, te, nt: (row(i, nt), j)),
        ),
        out_shape=jax.ShapeDtypeStruct((s, f), BF16),
        compiler_params=_params("arbitrary", "arbitrary"),
        name="moe_up",
    )(tile_expert, n_tiles, xg, w1, w3)

    return pl.pallas_call(
        _moe_down_kernel,
        grid_spec=pltpu.PrefetchScalarGridSpec(
            num_scalar_prefetch=2,
            grid=(d // td, n_t),
            in_specs=[
                pl.BlockSpec((tm, f), lambda j, i, te, nt: (row(i, nt), 0)),
                pl.BlockSpec((None, f, td), lambda j, i, te, nt: (te[row(i, nt)], 0, j)),
                pl.BlockSpec((tm, 1), lambda j, i, te, nt: (row(i, nt), 0)),
            ],
            out_specs=pl.BlockSpec((tm, td), lambda j, i, te, nt: (row(i, nt), j)),
        ),
        out_shape=jax.ShapeDtypeStruct((s, d), F32),
        compiler_params=_params("arbitrary", "arbitrary"),
        name="moe_down",
    )(tile_expert, n_tiles, up, w2, gate_slot)


def _moe(h2, route, w1, w3, w2):
    n, d = h2.shape
    n_e = w1.shape[0]
    tm = min(MOE_TILE_ROWS, n)
    n_t = -(-(TOP_K * n + n_e * (tm - 1)) // tm)
    s = n_t * tm
    expert = route[:, :TOP_K].astype(jnp.int32).T.reshape(-1)
    gate = route[:, TOP_K:2 * TOP_K].T.reshape(-1)
    token = jnp.tile(jnp.arange(n, dtype=jnp.int32), TOP_K)
    onehot = (expert[:, None] == jnp.arange(n_e, dtype=jnp.int32)[None]).astype(jnp.int32)
    csum = jnp.cumsum(onehot, axis=0)
    rank = jnp.sum(onehot * csum, axis=1) - 1
    counts = csum[-1]
    padded = -(-counts // tm) * tm
    ends = jnp.cumsum(padded)
    starts = ends - padded
    slot = starts[expert] + rank
    token_of_slot = jnp.zeros((s,), jnp.int32).at[slot].set(token)
    gate_of_slot = jnp.zeros((s,), F32).at[slot].set(gate)
    n_tiles = (ends[-1] // tm).astype(jnp.int32).reshape(1)
    tile_start = jnp.arange(n_t, dtype=jnp.int32) * tm
    tile_expert = jnp.minimum(jnp.sum((tile_start[:, None] >= ends[None]).astype(jnp.int32), axis=1), n_e - 1)
    xg = jnp.take(h2, token_of_slot, axis=0)
    yo = _moe_grouped(xg, gate_of_slot.reshape(s, 1), tile_expert, n_tiles, w1, w3, w2, tm=tm)
    slots = slot.reshape(TOP_K, n)
    return [jnp.take(yo, slots[kk], axis=0) for kk in range(TOP_K)]


def _dwconv_kernel(prev_ref, cur_ref, next_ref, w_ref, b_ref, o_ref, *, pad_left, silu):
    i = pl.program_id(1)
    cur = cur_ref[...]
    rows = cur.shape[0]
    prev = jnp.where(i > 0, prev_ref[...], 0.0)
    nxt = jnp.where(i < pl.num_programs(1) - 1, next_ref[...], 0.0)
    ext = jnp.concatenate([prev, cur, nxt], axis=0)
    acc = jnp.zeros_like(cur) + b_ref[...]
    for j in range(w_ref.shape[0]):
        s = j - pad_left
        us = cur if s == 0 else pltpu.roll(ext, (-s) % ext.shape[0], axis=0)[SUBLANES:SUBLANES + rows]
        acc = acc + w_ref[j:j + 1, :] * us
    if silu:
        acc = acc * jax.nn.sigmoid(acc)
    o_ref[...] = acc


def _dwconv(u, w, b, pad_left, *, col0=0, silu=False):
    n = u.shape[0]
    k, c = w.shape
    assert k - 1 <= SUBLANES
    tc = _tile(math.gcd(c, col0) if col0 else c, 256, LANES)
    off = col0 // tc
    tr = _tile(n, 512, SUBLANES)
    per = tr // SUBLANES
    last = n // SUBLANES - 1
    return pl.pallas_call(
        functools.partial(_dwconv_kernel, pad_left=pad_left, silu=silu),
        grid=(c // tc, n // tr),
        in_specs=[pl.BlockSpec((SUBLANES, tc), lambda j, i: (jnp.maximum(i * per - 1, 0), j + off)),
                  pl.BlockSpec((tr, tc), lambda j, i: (i, j + off)),
                  pl.BlockSpec((SUBLANES, tc), lambda j, i: (jnp.minimum((i + 1) * per, last), j + off)),
                  pl.BlockSpec((k, tc), lambda j, i: (0, j)),
                  pl.BlockSpec((1, tc), lambda j, i: (0, j))],
        out_specs=pl.BlockSpec((tr, tc), lambda j, i: (i, j)),
        out_shape=jax.ShapeDtypeStruct((n, c), F32),
        compiler_params=_params("parallel", "parallel"),
        name="dwconv",
    )(u, u, u, w.astype(F32), b.astype(F32).reshape(1, c))


def _ssd_kernel(xf_ref, bf_ref, cf_ref, dtf_ref, xb_ref, bb_ref, cb_ref, dtb_ref, bias_ref, a_ref, s0_ref,
                yf_ref, yb_ref, st_ref, state, *, heads_per_group, head_dim):
    step = pl.program_id(1)

    @pl.when(step == 0)
    def _():
        state[...] = s0_ref[...]

    q, gw = xf_ref.shape
    hp = lax.Precision.HIGHEST
    nt = (((1,), (1,)), ((), ()))
    row = lax.broadcasted_iota(jnp.int32, (q, q), 0)
    col = lax.broadcasted_iota(jnp.int32, (q, q), 1)
    tri = (row >= col).astype(F32)
    neg = jnp.float32(-jnp.inf)
    head_of_col = lax.broadcasted_iota(jnp.int32, (LANES, gw), 1) // head_dim
    lane_id = lax.broadcasted_iota(jnp.int32, (LANES, gw), 0)
    low_half = lax.broadcasted_iota(jnp.int32, (q, LANES), 1) < head_dim
    heads_per_tile = LANES // head_dim
    streams = ((xf_ref, bf_ref, cf_ref, dtf_ref, yf_ref), (xb_ref, bb_ref, cb_ref, dtb_ref, yb_ref))
    for d, (x_ref, b_ref, c_ref, dt_ref, y_ref) in enumerate(streams):
        xs = x_ref[...]
        bm = b_ref[...].astype(BF16)
        cm = c_ref[...].astype(BF16)
        dt = jax.nn.softplus(dt_ref[...] + bias_ref[...])
        da = dt * a_ref[...]
        cum = jnp.dot(tri, da, preferred_element_type=F32, precision=hp)
        total = cum[q - 1:q, :]
        pos = cum if d == 0 else cum - da
        pos_t = pos.T
        if d == 0:
            carry_in, carry_out = jnp.exp(pos), jnp.exp(total - pos)
        else:
            carry_in, carry_out = jnp.exp(total - pos), jnp.exp(pos)
        spread = (lane_id == head_of_col + d * heads_per_group).astype(F32)
        xd = xs * jnp.dot(dt, spread, preferred_element_type=F32, precision=hp)
        st = state[d]
        y_off = (lax.dot_general(cm, st.astype(BF16), nt, preferred_element_type=F32)
                 * jnp.dot(carry_in, spread, preferred_element_type=F32, precision=hp))
        xd_out = (xd * jnp.dot(carry_out, spread, preferred_element_type=F32, precision=hp)).astype(BF16)
        upd = lax.dot_general(xd_out, bm, (((0,), (0,)), ((), ())), preferred_element_type=F32)
        keep = lax.dot_general(spread, jnp.exp(total), (((0,), (1,)), ((), ())),
                               preferred_element_type=F32, precision=hp)
        state[d] = keep * st + upd
        cb = lax.dot_general(cm, bm, nt, preferred_element_type=F32)
        mask = (row >= col) if d == 0 else (col >= row)
        xdb = xd.astype(BF16)
        for tile in range(gw // LANES):
            cols = slice(tile * LANES, (tile + 1) * LANES)
            parts = []
            for e in range(tile * heads_per_tile, (tile + 1) * heads_per_tile):
                lane = d * heads_per_group + e
                p_col = pos[:, lane:lane + 1]
                p_row = pos_t[lane:lane + 1, :]
                expo = (p_col - p_row) if d == 0 else (p_row - p_col)
                dec = jnp.exp(jnp.where(mask, expo, neg))
                parts.append(jnp.dot((cb * dec).astype(BF16), xdb[:, cols], preferred_element_type=F32))
            y_diag = parts[0] if len(parts) == 1 else jnp.where(low_half, parts[0], parts[1])
            y_ref[:, cols] = y_diag + y_off[:, cols]

    @pl.when(step == pl.num_programs(1) - 1)
    def _():
        st_ref[...] = state[...]


def _ssd_scan(xbc, dtg, bias_g, a_g, s0, *, ssd_w, groups):
    n = xbc.shape[0]
    e = ssd_w // (SSD_P * groups)
    q = min(SSD_CHUNK, n)
    steps = n // q
    gw = e * SSD_P
    nb = ssd_w // SSD_N
    assert LANES % SSD_P == 0 and LANES // SSD_P <= 2 and gw % LANES == 0 and 2 * e <= LANES

    def fwd(blk):
        return lambda g, s: (s, blk(g))

    def bwd(blk):
        return lambda g, s: (steps - 1 - s, blk(g))

    def stream(order):
        return [pl.BlockSpec((q, gw), order(lambda g: g)),
                pl.BlockSpec((q, SSD_N), order(lambda g: nb + g)),
                pl.BlockSpec((q, SSD_N), order(lambda g: nb + groups + g))]

    grp = pl.BlockSpec((None, 1, LANES), lambda g, s: (g, 0, 0))
    st_spec = pl.BlockSpec((2, None, gw, SSD_N), lambda g, s: (0, g, 0, 0))
    return pl.pallas_call(
        functools.partial(_ssd_kernel, heads_per_group=e, head_dim=SSD_P),
        grid=(groups, steps),
        in_specs=(stream(fwd) + [pl.BlockSpec((q, LANES), fwd(lambda g: g))]
                  + stream(bwd) + [pl.BlockSpec((q, LANES), bwd(lambda g: g))]
                  + [grp, grp, st_spec]),
        out_specs=[pl.BlockSpec((q, gw), fwd(lambda g: g)), pl.BlockSpec((q, gw), bwd(lambda g: g)), st_spec],
        out_shape=[jax.ShapeDtypeStruct((n, ssd_w), F32), jax.ShapeDtypeStruct((n, ssd_w), F32),
                   jax.ShapeDtypeStruct(s0.shape, F32)],
        scratch_shapes=[pltpu.VMEM((2, gw, SSD_N), F32)],
        compiler_params=_params("parallel", "arbitrary"),
        name="ssd_scan",
    )(xbc, xbc, xbc, dtg, xbc, xbc, xbc, dtg, bias_g, a_g, s0)


def _ssd_gate_norm_kernel(yf_ref, yb_ref, xs_ref, z_ref, d_ref, nw_ref, o_ref, *, groups):
    y = yf_ref[...] + yb_ref[...] + xs_ref[...] * d_ref[...]
    z = z_ref[...]
    y = y * (z * jax.nn.sigmoid(z))
    gw = y.shape[1] // groups
    for g in range(groups):
        yg = y[:, g * gw:(g + 1) * gw]
        yg = yg * lax.rsqrt(jnp.mean(yg * yg, axis=-1, keepdims=True) + NORM_EPS)
        o_ref[:, g * gw:(g + 1) * gw] = (yg * nw_ref[:, g * gw:(g + 1) * gw]).astype(o_ref.dtype)


def _ssd_gate_norm(y_f, y_b, xbc, zx, d_cols, norm_w, *, groups):
    n, w = y_f.shape
    tr = _tile(n, 256, SUBLANES)
    rows = pl.BlockSpec((tr, w), lambda i: (i, 0))
    vec = pl.BlockSpec((1, w), lambda i: (0, 0))
    return pl.pallas_call(
        functools.partial(_ssd_gate_norm_kernel, groups=groups),
        grid=(n // tr,),
        in_specs=[rows, rows, rows, rows, vec, vec],
        out_specs=rows,
        out_shape=jax.ShapeDtypeStruct((n, w), BF16),
        compiler_params=_params("parallel"),
        name="ssd_gate_norm",
    )(y_f, y_b, xbc, zx, d_cols.reshape(1, w), norm_w.astype(F32).reshape(1, w))


def _na_kernel(q_ref, k_ref, v_ref, kc_ref, vc_ref, bias_ref, o_ref, *, grid_rows, win_h, rows_per_block, scale):
    blk = pl.program_id(1)
    kc = kc_ref[...]
    vc = vc_ref[...]
    nt = (((1,), (1,)), ((), ()))

    def one_row(i, carry):
        r = blk * rows_per_block + i
        rs = jnp.clip(r - win_h // 2, 0, grid_rows - win_h)
        q = q_ref[pl.ds(pl.multiple_of(i * GRID_W, GRID_W), GRID_W), :]
        start = pl.multiple_of(rs * GRID_W, GRID_W)
        kw = k_ref[pl.ds(start, win_h * GRID_W), :]
        vw = v_ref[pl.ds(start, win_h * GRID_W), :]
        s = lax.dot_general(q, kw, nt, preferred_element_type=F32) * scale + bias_ref[r - rs]
        sc = lax.dot_general(q, kc, nt, preferred_element_type=F32) * scale
        m = jnp.maximum(jnp.max(s, axis=-1, keepdims=True), jnp.max(sc, axis=-1, keepdims=True))
        p = jnp.exp(s - m)
        pc = jnp.exp(sc - m)
        denom = jnp.sum(p, axis=-1, keepdims=True) + jnp.sum(pc, axis=-1, keepdims=True)
        o = (jnp.dot(p.astype(BF16), vw, preferred_element_type=F32)
             + jnp.dot(pc.astype(BF16), vc, preferred_element_type=F32))
        o_ref[pl.ds(pl.multiple_of(i * GRID_W, GRID_W), GRID_W), :] = (o / denom).astype(o_ref.dtype)
        return carry

    lax.fori_loop(0, rows_per_block, one_row, 0, unroll=True)


def _na_bias_table(rpb, win_h):
    n_dcol = 2 * NA_WIN_W - 1
    p = np.arange(win_h)[:, None]
    a = np.arange(win_h)[None, :]
    rows = rpb.astype(F32)[:, a - p + NA_WIN_H - 1, :]
    qc = np.arange(GRID_W)[:, None]
    kc = np.arange(GRID_W)[None, :]
    dcol = np.clip(kc - qc + NA_WIN_W - 1, 0, n_dcol - 1)
    onehot = (np.arange(n_dcol)[:, None, None] == dcol[None]).astype(np.float32)
    bias = jnp.einsum('hpad,dqk->hpqak', rows, jnp.asarray(onehot), precision=lax.Precision.HIGHEST)
    cstart = np.clip(qc - NA_WIN_W // 2, 0, GRID_W - NA_WIN_W)
    ok = (kc >= cstart) & (kc < cstart + NA_WIN_W)
    bias = jnp.where(ok[None, None, :, None, :], bias, -jnp.inf)
    return bias.reshape(rpb.shape[0], win_h, GRID_W, win_h * GRID_W)


def _neighbourhood_attention(qkv, qkv_c, rpb, *, heads):
    n = qkv.shape[0]
    c = qkv_c.shape[0]
    grid_rows = n // GRID_W
    win_h = min(NA_WIN_H, grid_rows)
    rows_per_block = _tile(grid_rows, 8, 1)
    tq = rows_per_block * GRID_W
    bias = _na_bias_table(rpb, win_h)
    return pl.pallas_call(
        functools.partial(_na_kernel, grid_rows=grid_rows, win_h=win_h, rows_per_block=rows_per_block,
                          scale=NA_DH ** -0.5),
        grid=(heads, n // tq),
        in_specs=[pl.BlockSpec((tq, NA_DH), lambda h, i: (i, h)),
                  pl.BlockSpec((n, NA_DH), lambda h, i: (0, heads + h)),
                  pl.BlockSpec((n, NA_DH), lambda h, i: (0, 2 * heads + h)),
                  pl.BlockSpec((c, NA_DH), lambda h, i: (0, heads + h)),
                  pl.BlockSpec((c, NA_DH), lambda h, i: (0, 2 * heads + h)),
                  pl.BlockSpec((None, win_h, GRID_W, win_h * GRID_W), lambda h, i: (h, 0, 0, 0))],
        out_specs=pl.BlockSpec((tq, NA_DH), lambda h, i: (i, h)),
        out_shape=jax.ShapeDtypeStruct((n, heads * NA_DH), BF16),
        compiler_params=_params("parallel", "arbitrary"),
        name="neighbourhood_attention",
    )(qkv, qkv, qkv, qkv_c, qkv_c, bias)


def _ctx_attn_kernel(q_ref, k_ref, v_ref, o_ref, *, scale):
    s = lax.dot_general(q_ref[...], k_ref[...], (((1,), (1,)), ((), ())), preferred_element_type=F32) * scale
    p = jnp.exp(s - jnp.max(s, axis=-1, keepdims=True))
    o = jnp.dot(p.astype(BF16), v_ref[...], preferred_element_type=F32)
    o_ref[...] = (o / jnp.sum(p, axis=-1, keepdims=True)).astype(o_ref.dtype)


def _context_attention(qkv_c, *, heads):
    c = qkv_c.shape[0]
    return pl.pallas_call(
        functools.partial(_ctx_attn_kernel, scale=NA_DH ** -0.5),
        grid=(heads,),
        in_specs=[pl.BlockSpec((c, NA_DH), lambda h: (0, h)),
                  pl.BlockSpec((c, NA_DH), lambda h: (0, heads + h)),
                  pl.BlockSpec((c, NA_DH), lambda h: (0, 2 * heads + h))],
        out_specs=pl.BlockSpec((c, NA_DH), lambda h: (0, h)),
        out_shape=jax.ShapeDtypeStruct((c, heads * NA_DH), BF16),
        compiler_params=_params("parallel"),
        name="context_attention",
    )(qkv_c, qkv_c, qkv_c)


def _rglru_kernel(*refs, chunk, emit_y):
    if emit_y:
        u_ref, wg_ref, bg_ref, lam_ref, h0_ref, gate_ref, y_ref, ht_ref, hsum = refs
    else:
        u_ref, wg_ref, bg_ref, lam_ref, h0_ref, ht_ref = refs
    n, w = u_ref.shape
    n_chunks = n // chunk
    tiles = chunk // SUBLANES
    sub = lax.broadcasted_iota(jnp.int32, (chunk, w), 0) % SUBLANES
    log_sig = jax.nn.log_sigmoid(lam_ref[...])

    def chunk_scan(d, c0, carry):
        u = u_ref[pl.ds(c0, chunk), :]
        ub = u.astype(BF16)
        r = jax.nn.sigmoid(jnp.dot(ub, wg_ref[d, 0], preferred_element_type=F32) + bg_ref[d, 0])
        i = jax.nn.sigmoid(jnp.dot(ub, wg_ref[d, 1], preferred_element_type=F32) + bg_ref[d, 1])
        log_a = RG_C * r * log_sig[d]
        a = jnp.exp(log_a)
        b = jnp.sqrt(jnp.maximum(1.0 - a * a, 0.0)) * i * u
        for sh in (1, 2, 4):
            if d == 0:
                a_s, b_s, edge = pltpu.roll(a, sh, axis=0), pltpu.roll(b, sh, axis=0), sub < sh
            else:
                a_s, b_s = pltpu.roll(a, chunk - sh, axis=0), pltpu.roll(b, chunk - sh, axis=0)
                edge = sub >= SUBLANES - sh
            b = jnp.where(edge, b, a * b_s + b)
            a = jnp.where(edge, a, a * a_s)
        hs = [None] * tiles
        order = range(tiles) if d == 0 else range(tiles - 1, -1, -1)
        for k in order:
            sl = slice(k * SUBLANES, (k + 1) * SUBLANES)
            h = a[sl] * carry + b[sl]
            hs[k] = h
            carry = h[SUBLANES - 1:SUBLANES] if d == 0 else h[0:1]
        return jnp.concatenate(hs, axis=0), carry

    def step(j, carries, second_pass):
        cf, cb = carries
        rows_f = pl.ds(pl.multiple_of(j * chunk, chunk), chunk)
        rows_b = pl.ds(pl.multiple_of((n_chunks - 1 - j) * chunk, chunk), chunk)
        hf, cf = chunk_scan(0, pl.multiple_of(j * chunk, chunk), cf)
        hb, cb = chunk_scan(1, pl.multiple_of((n_chunks - 1 - j) * chunk, chunk), cb)
        if emit_y:
            if second_pass:
                y_ref[rows_f, :] = (jax.nn.gelu(gate_ref[rows_f, :]) * (hsum[rows_f, :] + hf)).astype(y_ref.dtype)
                y_ref[rows_b, :] = (jax.nn.gelu(gate_ref[rows_b, :]) * (hsum[rows_b, :] + hb)).astype(y_ref.dtype)
            else:
                hsum[rows_f, :] = hf
                hsum[rows_b, :] = hb
        return cf, cb

    carries = (h0_ref[0], h0_ref[1])
    half = n_chunks // 2
    carries = lax.fori_loop(0, half, lambda j, c: step(j, c, False), carries)
    carries = lax.fori_loop(half, n_chunks, lambda j, c: step(j, c, True), carries)
    ht_ref[0] = carries[0]
    ht_ref[1] = carries[1]


def _rglru(u, gate_src, gate_w, gate_b, lam, h0):
    n, w = u.shape
    nb, bw = gate_w.shape[2], gate_w.shape[3]
    chunk = _tile(n // 2, 256, SUBLANES)
    assert (n // chunk) % 2 == 0, "the two sweeps hand over at the middle chunk boundary"
    emit_y = gate_src is not None
    col = lambda j: (0, j)
    in_specs = [pl.BlockSpec((n, bw), col),
                pl.BlockSpec((2, 2, None, bw, bw), lambda j: (0, 0, j, 0, 0)),
                pl.BlockSpec((2, 2, None, 1, bw), lambda j: (0, 0, j, 0, 0)),
                pl.BlockSpec((2, 1, bw), lambda j: (0, 0, j)),
                pl.BlockSpec((2, 1, bw), lambda j: (0, 0, j))]
    args = [u, gate_w.astype(BF16), gate_b.astype(F32).reshape(2, 2, nb, 1, bw), lam.astype(F32).reshape(2, 1, w),
            h0.reshape(2, 1, w)]
    ht_spec = pl.BlockSpec((2, 1, bw), lambda j: (0, 0, j))
    ht_shape = jax.ShapeDtypeStruct((2, 1, w), F32)
    if emit_y:
        in_specs.append(pl.BlockSpec((n, bw), col))
        args.append(gate_src)
        out_specs = [pl.BlockSpec((n, bw), col), ht_spec]
        out_shape = [jax.ShapeDtypeStruct((n, w), BF16), ht_shape]
        scratch = [pltpu.VMEM((n, bw), F32)]
    else:
        out_specs, out_shape, scratch = ht_spec, ht_shape, []
    out = pl.pallas_call(
        functools.partial(_rglru_kernel, chunk=chunk, emit_y=emit_y),
        grid=(nb,),
        in_specs=in_specs,
        out_specs=out_specs,
        out_shape=out_shape,
        scratch_shapes=scratch,
        compiler_params=_params("parallel"),
        name="rglru",
    )(*args)
    if emit_y:
        return out[0], out[1].reshape(2, w)
    return None, out.reshape(2, w)


def _dft_tables(n):
    n1 = 1 << (int(math.log2(n)) // 2)
    n2 = n // n1
    assert n1 * n2 == n and n2 % 2 == 0
    i1, i2 = np.arange(n1), np.arange(n2)
    f1 = np.exp(-2j * np.pi * np.outer(i1, i1) / n1)
    f2 = np.exp(-2j * np.pi * np.outer(i2, i2) / n2)
    tw = np.exp(-2j * np.pi * np.outer(i2, i1) / n)
    return n1, n2, f1, f2, tw


def _stack_ri(m, sign=1.0):
    return jnp.asarray(np.concatenate([m.real, sign * m.imag], axis=0), BF16)


def _hy_hidden_kernel(feat_ref, w_in_ref, b_in_ref, w_mid_ref, b_mid_ref, freq_ref, o_ref):
    hp = lax.Precision.HIGHEST
    freq = freq_ref[...]
    hid = jnp.sin(freq * (jnp.dot(feat_ref[...], w_in_ref[...], preferred_element_type=F32, precision=hp)
                          + b_in_ref[...]))
    for m in range(w_mid_ref.shape[0]):
        hid = jnp.sin(freq * (jnp.dot(hid, w_mid_ref[m], preferred_element_type=F32, precision=hp)
                              + b_mid_ref[m]))
    o_ref[...] = hid


def _hy_hidden(L, w_in, b_in, w_mid, b_mid, freq):
    t = jnp.linspace(0.0, 1.0, L, dtype=F32)[:, None]
    ang = ((2.0 * math.pi / L) * jnp.arange(L, dtype=F32)[:, None]
           * jnp.linspace(1e-4, HY_BANDS - 1, HY_BANDS, dtype=F32)[None])
    emb, ffn = w_in.shape
    assert ffn <= LANES
    feats = jnp.zeros((L, LANES), F32).at[:, :emb].set(jnp.concatenate([t, jnp.cos(ang), -jnp.sin(ang)], axis=-1))

    def padded(v, shape):
        return jnp.zeros(shape, F32).at[tuple(slice(0, n) for n in v.shape)].set(v.astype(F32))

    tr = _tile(L, 512, SUBLANES)
    full = lambda *shape: pl.BlockSpec(shape, lambda *_: (0,) * len(shape))
    return pl.pallas_call(
        _hy_hidden_kernel,
        grid=(L // tr,),
        in_specs=[pl.BlockSpec((tr, LANES), lambda i: (i, 0)), full(LANES, LANES), full(1, LANES),
                  full(HY_INNER, LANES, LANES), full(HY_INNER, 1, LANES), full(1, LANES)],
        out_specs=pl.BlockSpec((tr, LANES), lambda i: (i, 0)),
        out_shape=jax.ShapeDtypeStruct((L, LANES), F32),
        compiler_params=_params("parallel"),
        name="hyena_filter_hidden",
    )(feats, padded(w_in, (LANES, LANES)), padded(b_in[None], (1, LANES)), padded(w_mid, (HY_INNER, LANES, LANES)),
      padded(b_mid[:, None], (HY_INNER, 1, LANES)), padded(freq[None], (1, LANES)))


def _hy_filter_stage1_kernel(hid_ref, w_out_ref, delta_ref, sign_ref, f2_ref, twr_ref, twi_ref, o_ref, *,
                             n1, length):
    i = pl.program_id(0)
    half = hid_ref.shape[0]
    n2 = o_ref.shape[1]
    taps = jnp.dot(hid_ref[...].astype(BF16), w_out_ref[...], preferred_element_type=F32)
    row = lax.broadcasted_iota(jnp.int32, (half, 1), 0)
    t = (i + n1 * row).astype(F32) * (1.0 / (length - 1))
    taps = taps * jnp.exp(-t * jnp.abs(delta_ref[...]))
    p = jnp.dot(f2_ref[...], taps.astype(BF16), preferred_element_type=F32)
    sign = sign_ref[...]
    ar, ai = p[:n2], p[n2:] * sign
    tr = twr_ref[...]
    ti = twi_ref[...] * sign
    o_ref[0] = (ar * tr - ai * ti).astype(o_ref.dtype)
    o_ref[1] = (ar * ti + ai * tr).astype(o_ref.dtype)


def _hy_filter_stage1(hid, w_out, deltas, c, tabs):
    n1, n2, _, f2, tw = tabs
    half = n2 // 2
    L = hid.shape[0]
    cols = w_out.shape[1]
    w_out_p = jnp.zeros((LANES, cols), BF16).at[:w_out.shape[0]].set(w_out.astype(BF16))
    sign = jnp.asarray(np.tile(np.repeat(np.array([1.0, -1.0], np.float32), c), cols // (2 * c)).reshape(1, cols))
    twspec = pl.BlockSpec((None, n2, 1), lambda i: (i, 0, 0))
    full = lambda *shape: pl.BlockSpec(shape, lambda i: (0,) * len(shape))
    return pl.pallas_call(
        functools.partial(_hy_filter_stage1_kernel, n1=n1, length=L),
        grid=(n1,),
        in_specs=[pl.BlockSpec((half, LANES), lambda i: (0, i)), full(LANES, cols), full(1, cols), full(1, cols),
                  full(2 * n2, half), twspec, twspec],
        out_specs=pl.BlockSpec((2, n2, cols), lambda i: (0, 0, i)),
        out_shape=jax.ShapeDtypeStruct((2, n2, n1 * cols), BF16),
        compiler_params=_params("parallel"),
        name="hyena_filter_stage1",
    )(hid.reshape(half, n1 * LANES), w_out_p, deltas.astype(F32).reshape(1, cols), sign, _stack_ri(f2[:, :half]),
      jnp.asarray(tw.real.T.reshape(n1, n2, 1), F32), jnp.asarray(tw.imag.T.reshape(n1, n2, 1), F32)
      ).reshape(2, n2, n1, cols)


def _hy_filter_spectrum_kernel(af_ref, ab_ref, f1_ref, h_ref):
    n1 = h_ref.shape[1]
    f1 = f1_ref[...]
    p = jnp.dot(f1, af_ref[0], preferred_element_type=F32)
    q = jnp.dot(f1, af_ref[1], preferred_element_type=F32)
    r = jnp.dot(f1, ab_ref[0], preferred_element_type=F32)
    s = jnp.dot(f1, ab_ref[1], preferred_element_type=F32)
    h_ref[0] = p[:n1] - q[n1:] + r[:n1] + s[n1:]
    h_ref[1] = p[n1:] + q[:n1] + s[:n1] - r[n1:]


def _hy_filter_spectrum(fa, c, tabs):
    n1, n2, f1, _, _ = tabs
    return pl.pallas_call(
        _hy_filter_spectrum_kernel,
        grid=(HY_ORDER, n2),
        in_specs=[pl.BlockSpec((2, None, n1, c), lambda o, k: (0, k, 0, 2 * o)),
                  pl.BlockSpec((2, None, n1, c), lambda o, k: (0, k, 0, 2 * o + 1)),
                  pl.BlockSpec((2 * n1, n1), lambda o, k: (0, 0))],
        out_specs=pl.BlockSpec((None, 2, None, n1, c), lambda o, k: (o, 0, k, 0, 0)),
        out_shape=jax.ShapeDtypeStruct((HY_ORDER, 2, n2, n1, c), F32),
        compiler_params=_params("parallel", "parallel"),
        name="hyena_filter_spectrum",
    )(fa, fa, _stack_ri(f1))


def _hy_stage1_kernel(x_ref, f2_ref, twr_ref, twi_ref, o_ref):
    n2 = o_ref.shape[1]
    c = x_ref.shape[2]
    for j in range(x_ref.shape[1]):
        p = jnp.dot(f2_ref[...], x_ref[:, j, :].astype(BF16), preferred_element_type=F32)
        ar, ai = p[:n2], p[n2:]
        tr, ti = twr_ref[j], twi_ref[j]
        o_ref[0, :, j * c:(j + 1) * c] = (ar * tr - ai * ti).astype(o_ref.dtype)
        o_ref[1, :, j * c:(j + 1) * c] = (ar * ti + ai * tr).astype(o_ref.dtype)


def _hy_stage1(src, part, c, tabs):
    n1, n2, _, f2, tw = tabs
    half = n2 // 2
    r = SUBLANES
    twspec = pl.BlockSpec((r, n2, 1), lambda i: (i, 0, 0))
    return pl.pallas_call(
        _hy_stage1_kernel,
        grid=(n1 // r,),
        in_specs=[pl.BlockSpec((half, r, c), lambda i: (0, i, part)),
                  pl.BlockSpec((2 * n2, half), lambda i: (0, 0)), twspec, twspec],
        out_specs=pl.BlockSpec((2, n2, r * c), lambda i: (0, 0, i)),
        out_shape=jax.ShapeDtypeStruct((2, n2, n1 * c), BF16),
        compiler_params=_params("parallel"),
        name="hyena_dft_stage1",
    )(src.reshape(half, n1, src.shape[1]), _stack_ri(f2[:, :half]),
      jnp.asarray(tw.real.T.reshape(n1, n2, 1), F32), jnp.asarray(tw.imag.T.reshape(n1, n2, 1), F32)
      ).reshape(2, n2, n1, c)


def _hy_mid_kernel(a_ref, h_ref, f1_ref, twr_ref, twi_ref, o_ref):
    n1 = o_ref.shape[1]
    f1 = f1_ref[...]
    p = jnp.dot(f1, a_ref[0], preferred_element_type=F32)
    q = jnp.dot(f1, a_ref[1], preferred_element_type=F32)
    xr = p[:n1] - q[n1:]
    xi = p[n1:] + q[:n1]
    hr, hi = h_ref[0], h_ref[1]
    yr = (xr * hr - xi * hi).astype(BF16)
    yi = (xr * hi + xi * hr).astype(BF16)
    p = jnp.dot(f1, yr, preferred_element_type=F32)
    q = jnp.dot(f1, yi, preferred_element_type=F32)
    br = p[:n1] + q[n1:]
    bi = q[:n1] - p[n1:]
    tr, ti = twr_ref[...], twi_ref[...]
    o_ref[0] = (br * tr + bi * ti).astype(o_ref.dtype)
    o_ref[1] = (bi * tr - br * ti).astype(o_ref.dtype)


def _hy_stage3_kernel(b_ref, f2_ref, z_ref, gate_ref, skip_ref, o_ref, *, inv_n):
    _, n2, rc = b_ref.shape
    r = z_ref.shape[1]
    c = rc // r
    for j in range(r):
        b = b_ref[:, :, j * c:(j + 1) * c].reshape(2 * n2, c)
        acc = jnp.dot(f2_ref[...], b, preferred_element_type=F32)
        o_ref[:, j, :] = (gate_ref[:, j, :] * (acc * inv_n + z_ref[:, j, :] * skip_ref[...])).astype(o_ref.dtype)


def _hy_long_conv(z_src, z_part, gate_src, gate_part, spec, order, skip, tabs, out_dtype):
    n1, n2, f1, f2, tw = tabs
    L = z_src.shape[0]
    c = skip.shape[0]
    half = n2 // 2
    r = SUBLANES
    slab = pl.BlockSpec((2, None, n1, c), lambda k: (0, k, 0, 0))
    a = _hy_stage1(z_src, z_part, c, tabs)
    twspec = pl.BlockSpec((None, n1, 1), lambda k: (k, 0, 0))
    b = pl.pallas_call(
        _hy_mid_kernel,
        grid=(n2,),
        in_specs=[slab, pl.BlockSpec((None, 2, None, n1, c), lambda k: (order, 0, k, 0, 0)),
                  pl.BlockSpec((2 * n1, n1), lambda k: (0, 0)), twspec, twspec],
        out_specs=slab,
        out_shape=jax.ShapeDtypeStruct((2, n2, n1, c), BF16),
        compiler_params=_params("parallel"),
        name="hyena_spectral_product",
    )(a, spec, _stack_ri(f1), jnp.asarray(tw.real.reshape(n2, n1, 1), F32),
      jnp.asarray(tw.imag.reshape(n2, n1, 1), F32))
    f2c = jnp.asarray(np.concatenate([f2.real[:half], f2.imag[:half]], axis=1), BF16)
    out = pl.pallas_call(
        functools.partial(_hy_stage3_kernel, inv_n=1.0 / (n1 * n2)),
        grid=(n1 // r,),
        in_specs=[pl.BlockSpec((2, n2, r * c), lambda i: (0, 0, i)),
                  pl.BlockSpec((half, 2 * n2), lambda i: (0, 0)),
                  pl.BlockSpec((half, r, c), lambda i: (0, i, z_part)),
                  pl.BlockSpec((half, r, c), lambda i: (0, i, gate_part)),
                  pl.BlockSpec((1, c), lambda i: (0, 0))],
        out_specs=pl.BlockSpec((half, r, c), lambda i: (0, i, 0)),
        out_shape=jax.ShapeDtypeStruct((half, n1, c), out_dtype),
        compiler_params=_params("parallel"),
        name="hyena_dft_stage3",
    )(b.reshape(2, n2, n1 * c), f2c, z_src.reshape(half, n1, z_src.shape[1]),
      gate_src.reshape(half, n1, gate_src.shape[1]), skip.astype(F32).reshape(1, c))
    return out.reshape(L, c)


def _hyena(p, col0, conv_w, conv_b, w_in, b_in, w_mid, b_mid, w_out, freq, deltas, skip):
    assert HY_ORDER == 2
    L = p.shape[0]
    c = skip.shape[1]
    tabs = _dft_tables(2 * L)
    u = _dwconv(p, conv_w, conv_b, (HY_CONV - 1) // 2, col0=col0)
    hid = _hy_hidden(L, w_in, b_in, w_mid, b_mid, freq)
    spec = _hy_filter_spectrum(_hy_filter_stage1(hid, w_out, deltas, c, tabs), c, tabs)
    z = _hy_long_conv(u, 2, u, 0, spec, 0, skip[0], tabs, F32)
    return _hy_long_conv(z, 0, u, 1, spec, 1, skip[1], tabs, BF16)


def _even_layer_mix(h, hc, in_w, conv_w, conv_b, dt_bias, a_log, d_skip, norm_w, rpb):
    d = h.shape[1]
    ssd_w = norm_w.shape[0]
    heads = ssd_w // SSD_P
    e = heads // SSD_G
    na_heads = (d - ssd_w) // NA_DH
    o1 = ssd_w
    o2 = o1 + conv_w.shape[1]
    o3 = o2 + 2 * heads
    w_qkv = in_w[:, o3:].astype(BF16)
    dt_cols = in_w[:, o2:o3].reshape(d, 2, SSD_G, e).transpose(0, 2, 1, 3).reshape(d, SSD_G, 2 * e)
    w_dt = jnp.zeros((d, SSD_G, LANES), BF16).at[:, :, :2 * e].set(dt_cols.astype(BF16)).reshape(d, SSD_G * LANES)

    def grouped(v):
        vg = v.astype(F32).reshape(2, SSD_G, e).transpose(1, 0, 2).reshape(SSD_G, 2 * e)
        return jnp.zeros((SSD_G, LANES), F32).at[:, :2 * e].set(vg).reshape(SSD_G, 1, LANES)

    bias_g = grouped(dt_bias)
    a_g = grouped(-jnp.exp(a_log.astype(F32)))
    d_cols = jnp.repeat(d_skip.astype(F32), SSD_P)

    def ssd(t, s0):
        zx = _mm(t, in_w, n=o2)
        xbc = _dwconv(zx, conv_w, conv_b, SSD_CONV // 2, col0=o1, silu=True)
        y_f, y_b, s_t = _ssd_scan(xbc, _mm(t, w_dt), bias_g, a_g, s0, ssd_w=ssd_w, groups=SSD_G)
        return _ssd_gate_norm(y_f, y_b, xbc, zx, d_cols, norm_w, groups=SSD_G), s_t

    zero = jnp.zeros((2, SSD_G, e * SSD_P, SSD_N), F32)
    qkv_c = _mm(hc, w_qkv, out_dtype=BF16)
    y_ssd_c, s_c = ssd(hc, zero)
    qkv = _mm(h, w_qkv, out_dtype=BF16)
    y_ssd, _ = ssd(h, s_c)
    y_na = _neighbourhood_attention(qkv, qkv_c, rpb, heads=na_heads)
    y_na_c = _context_attention(qkv_c, heads=na_heads)
    return (y_ssd, y_na), (y_ssd_c, y_na_c)


def _odd_layer_mix(h, hc, in_w, conv_w, conv_b, gate_w, gate_b, lam, hy_params):
    rg_w = lam.shape[1]
    pad = RG_CONV // 2
    u_c = _dwconv(_mm(hc, in_w, col0=rg_w, n=rg_w), conv_w, conv_b, pad)
    _, h_c = _rglru(u_c, None, gate_w, gate_b, lam, jnp.zeros((2, rg_w), F32))
    proj = _mm(h, in_w)
    u = _dwconv(proj, conv_w, conv_b, pad, col0=rg_w)
    y_rg, _ = _rglru(u, proj, gate_w, gate_b, lam, h_c)
    return y_rg, _hyena(proj, 2 * rg_w, *hy_params)


def kernel(x, c, ctx, c_ctx, ada_w, ada_b, norm_mix_w, norm_ffn_w, norm_out_w, ev_in_w, ev_ssd_conv_w, ev_ssd_conv_b, ev_ssd_dt_bias, ev_ssd_a_log, ev_ssd_d, ev_ssd_norm_w, ev_na_rpb, ev_out_w, ev_ffn_w1, ev_ffn_w3, ev_ffn_w2, od_in_w, od_rg_conv_w, od_rg_conv_b, od_rg_gate_w, od_rg_gate_b, od_rg_lambda, od_hy_conv_w, od_hy_conv_b, od_hy_w_in, od_hy_b_in, od_hy_w_mid, od_hy_b_mid, od_hy_w_out, od_hy_freq, od_hy_deltas, od_hy_skip, od_out_w, od_router_w, od_moe_w1, od_moe_w3, od_moe_w2):
    batch, seq, d = x.shape
    assert batch == 1 and ada_w.shape[0] == 2, "kernel is written for one sequence and the even/odd layer pair"
    xs = x[0]
    cs = ctx[0]
    cvecs = jnp.concatenate([jax.nn.silu(c), jax.nn.silu(c_ctx)[None]], axis=0)
    mods = _ada(cvecs, ada_w, ada_b).reshape(2, 2, ADA_CHUNKS, d)

    mod, mod_c = mods[0, 0], mods[0, 1]
    h = _mod_norm(xs, norm_mix_w[0], mod[0], mod[1])
    hc = _mod_norm(cs, norm_mix_w[0], mod_c[0], mod_c[1])
    y, y_c = _even_layer_mix(h, hc, ev_in_w[0], ev_ssd_conv_w[0], ev_ssd_conv_b[0], ev_ssd_dt_bias[0],
                             ev_ssd_a_log[0], ev_ssd_d[0], ev_ssd_norm_w[0], ev_na_rpb[0])
    xs = _mm_residual2(*y, ev_out_w[0], xs, mod[2])
    cs = _mm_residual2(*y_c, ev_out_w[0], cs, mod_c[2])
    w1, w3, w2 = ev_ffn_w1[0], ev_ffn_w3[0], ev_ffn_w2[0].astype(BF16)
    h2 = _mod_norm(xs, norm_ffn_w[0], mod[3], mod[4])
    xs = _mm_residual(_mm_swiglu(h2, w1, w3), w2, xs, mod[5], tm=256, tn=512)
    h2c = _mod_norm(cs, norm_ffn_w[0], mod_c[3], mod_c[4])
    cs = _mm_residual(_mm_swiglu(h2c, w1, w3), w2, cs, mod_c[5], tm=256, tn=512)

    mod, mod_c = mods[1, 0], mods[1, 1]
    h = _mod_norm(xs, norm_mix_w[1], mod[0], mod[1])
    hc = _mod_norm(cs, norm_mix_w[1], mod_c[0], mod_c[1])
    hy_params = (od_hy_conv_w[0], od_hy_conv_b[0], od_hy_w_in[0], od_hy_b_in[0], od_hy_w_mid[0],
                 od_hy_b_mid[0], od_hy_w_out[0], od_hy_freq[0], od_hy_deltas[0], od_hy_skip[0])
    y = _odd_layer_mix(h, hc, od_in_w[0], od_rg_conv_w[0], od_rg_conv_b[0], od_rg_gate_w[0],
                       od_rg_gate_b[0], od_rg_lambda[0], hy_params)
    xs = _mm_residual2(*y, od_out_w[0], xs, mod[2])
    h2, route = _mod_norm_router(xs, norm_ffn_w[1], mod[3], mod[4], od_router_w[0])
    y1, y2 = _moe(h2, route, od_moe_w1[0], od_moe_w3[0], od_moe_w2[0])
    return _final_norm(xs, y1, y2, mod[5], norm_out_w)[None]
```

```python
import functools
import math

import numpy as np
import jax
import jax.numpy as jnp
from jax import lax
from jax.experimental import pallas as pl
from jax.experimental.pallas import tpu as pltpu

GRID_W = 64
ADA_CHUNKS = 6
NORM_EPS = 1e-6
SSD_P = 64
SSD_G = 4
SSD_N = 128
SSD_CONV = 4
SSD_CHUNK = 128
NA_DH = 128
NA_WIN_H = 8
NA_WIN_W = 16
NA_QCOLS = 16
NA_KCOLS = NA_QCOLS + NA_WIN_W
RG_BLOCKS = 16
RG_CONV = 4
RG_C = 8.0
HY_ORDER = 2
HY_CONV = 3
HY_EMB = 33
HY_BANDS = (HY_EMB - 1) // 2
HY_INNER = 2
TOP_K = 2

LANES = 128
SUBLANES = 8
VMEM_LIMIT_BYTES = 56 * 1024 * 1024
MOE_TILE_ROWS = 512

F32 = jnp.float32
BF16 = jnp.bfloat16


def _tile(n, target, align):
    for t in range(min(n, target), 0, -1):
        if n % t == 0 and t % align == 0:
            return t
    return n


def _params(*semantics):
    return pltpu.CompilerParams(dimension_semantics=semantics, vmem_limit_bytes=VMEM_LIMIT_BYTES)


def _ada_kernel(cb_ref, w_ref, b_ref, o_ref, *, n_rows):
    k, tn = w_ref.shape
    for l in range(tn // LANES):
        w = w_ref[:, l * LANES:(l + 1) * LANES].reshape(k // SUBLANES, SUBLANES, LANES)
        for r in range(n_rows):
            cb = cb_ref[r].reshape(k // SUBLANES, SUBLANES, LANES)
            part = jnp.sum(w * cb, axis=0)
            o_ref[r:r + 1, l * LANES:(l + 1) * LANES] = (
                jnp.sum(part, axis=0, keepdims=True) + b_ref[:, l * LANES:(l + 1) * LANES])


def _ada(cvecs, w, b):
    r, k = cvecs.shape
    layers, _, n = w.shape
    tn = _tile(n, 512, LANES)
    cb = jnp.broadcast_to(cvecs[:, :, None], (r, k, LANES))
    return pl.pallas_call(
        functools.partial(_ada_kernel, n_rows=r),
        grid=(layers, n // tn),
        in_specs=[pl.BlockSpec((r, k, LANES), lambda l, j: (0, 0, 0)),
                  pl.BlockSpec((None, k, tn), lambda l, j: (l, 0, j)),
                  pl.BlockSpec((None, 1, tn), lambda l, j: (l, 0, j))],
        out_specs=pl.BlockSpec((None, r, tn), lambda l, j: (l, 0, j)),
        out_shape=jax.ShapeDtypeStruct((layers, r, n), F32),
        compiler_params=_params("parallel", "parallel"),
        name="ada_matvec",
    )(cb, w, b.reshape(layers, 1, n))


def _norm_rows(x, w, shift, scale):
    y = x * lax.rsqrt(jnp.mean(x * x, axis=-1, keepdims=True) + NORM_EPS)
    y = y * w
    if scale is not None:
        y = y * (1.0 + scale) + shift
    return y


def _norm_kernel(x_ref, w_ref, sh_ref, sc_ref, o_ref):
    o_ref[...] = _norm_rows(x_ref[...], w_ref[...], sh_ref[...], sc_ref[...]).astype(o_ref.dtype)


def _final_norm_kernel(x_ref, y1_ref, y2_ref, gate_ref, w_ref, o_ref):
    x = x_ref[...] + gate_ref[...] * (y1_ref[...] + y2_ref[...])
    o_ref[...] = _norm_rows(x, w_ref[...], None, None).astype(o_ref.dtype)


def _norm_router_kernel(x_ref, w_ref, sh_ref, sc_ref, rw_ref, o_ref, route_ref, *, n_experts):
    h = _norm_rows(x_ref[...], w_ref[...], sh_ref[...], sc_ref[...])
    o_ref[...] = h.astype(o_ref.dtype)
    logits = jnp.dot(h, rw_ref[...], preferred_element_type=F32, precision=lax.Precision.HIGHEST)
    col = lax.broadcasted_iota(jnp.int32, logits.shape, 1)
    neg = jnp.float32(-jnp.inf)
    l1 = jnp.where(col < n_experts, logits, neg)
    m1 = jnp.max(l1, axis=-1, keepdims=True)
    i1 = jnp.min(jnp.where(l1 == m1, col, LANES), axis=-1, keepdims=True)
    l2 = jnp.where(col == i1, neg, l1)
    m2 = jnp.max(l2, axis=-1, keepdims=True)
    i2 = jnp.min(jnp.where(l2 == m2, col, LANES), axis=-1, keepdims=True)
    e2 = jnp.exp(m2 - m1)
    g1 = 1.0 / (1.0 + e2)
    g2 = e2 * g1
    route = jnp.where(col == 0, i1.astype(F32),
                      jnp.where(col == 1, i2.astype(F32),
                                jnp.where(col == 2, g1, jnp.where(col == 3, g2, 0.0))))
    route_ref[...] = route


def _mod_norm(x, w, shift, scale, out_dtype=BF16):
    m, d = x.shape
    tr = _tile(m, 256, SUBLANES)
    vec = pl.BlockSpec((1, d), lambda i: (0, 0))
    return pl.pallas_call(
        _norm_kernel,
        grid=(m // tr,),
        in_specs=[pl.BlockSpec((tr, d), lambda i: (i, 0)), vec, vec, vec],
        out_specs=pl.BlockSpec((tr, d), lambda i: (i, 0)),
        out_shape=jax.ShapeDtypeStruct((m, d), out_dtype),
        compiler_params=_params("parallel"),
        name="mod_norm",
    )(x, w.reshape(1, d), shift.reshape(1, d), scale.reshape(1, d))


def _final_norm(x, y1, y2, gate, w):
    m, d = x.shape
    tr = _tile(m, 256, SUBLANES)
    rows = pl.BlockSpec((tr, d), lambda i: (i, 0))
    vec = pl.BlockSpec((1, d), lambda i: (0, 0))
    return pl.pallas_call(
        _final_norm_kernel,
        grid=(m // tr,),
        in_specs=[rows, rows, rows, vec, vec],
        out_specs=rows,
        out_shape=jax.ShapeDtypeStruct((m, d), x.dtype),
        compiler_params=_params("parallel"),
        name="final_norm",
    )(x, y1, y2, gate.reshape(1, d), w.reshape(1, d))


def _mod_norm_router(x, w, shift, scale, router_w):
    m, d = x.shape
    n_experts = router_w.shape[1]
    tr = _tile(m, 256, SUBLANES)
    rw = jnp.zeros((d, LANES), F32).at[:, :n_experts].set(router_w.astype(F32))
    vec = pl.BlockSpec((1, d), lambda i: (0, 0))
    return pl.pallas_call(
        functools.partial(_norm_router_kernel, n_experts=n_experts),
        grid=(m // tr,),
        in_specs=[pl.BlockSpec((tr, d), lambda i: (i, 0)), vec, vec, vec,
                  pl.BlockSpec((d, LANES), lambda i: (0, 0))],
        out_specs=[pl.BlockSpec((tr, d), lambda i: (i, 0)),
                   pl.BlockSpec((tr, LANES), lambda i: (i, 0))],
        out_shape=[jax.ShapeDtypeStruct((m, d), BF16), jax.ShapeDtypeStruct((m, LANES), F32)],
        compiler_params=_params("parallel"),
        name="mod_norm_router",
    )(x, w.reshape(1, d), shift.reshape(1, d), scale.reshape(1, d), rw)


def _bf16(ref):
    return ref[...].astype(BF16)


def _mm_kernel(a_ref, b_ref, o_ref):
    o_ref[...] = jnp.dot(a_ref[...], _bf16(b_ref), preferred_element_type=F32).astype(o_ref.dtype)


def _mm_residual_kernel(a_ref, b_ref, res_ref, gate_ref, o_ref):
    acc = jnp.dot(a_ref[...], _bf16(b_ref), preferred_element_type=F32)
    o_ref[...] = res_ref[...] + gate_ref[...] * acc


def _mm_swiglu_kernel(a_ref, b1_ref, b3_ref, o_ref):
    a = a_ref[...]
    g = jnp.dot(a, _bf16(b1_ref), preferred_element_type=F32)
    u = jnp.dot(a, _bf16(b3_ref), preferred_element_type=F32)
    o_ref[...] = (g * jax.nn.sigmoid(g) * u).astype(o_ref.dtype)


def _mm_tiles(m, k, n, tm, tn):
    tm = _tile(m, tm, SUBLANES)
    tn = _tile(n, tn, LANES)
    return tm, tn


def _mm(a, b, *, out_dtype=F32, tm=1024, tn=512, col0=0, n=None):
    m, k = a.shape
    n = b.shape[1] - col0 if n is None else n
    tm, tn = _mm_tiles(m, k, math.gcd(n, col0) if col0 else n, tm, tn)
    off = col0 // tn
    return pl.pallas_call(
        _mm_kernel,
        grid=(n // tn, m // tm),
        in_specs=[pl.BlockSpec((tm, k), lambda j, i: (i, 0)),
                  pl.BlockSpec((k, tn), lambda j, i: (0, j + off))],
        out_specs=pl.BlockSpec((tm, tn), lambda j, i: (i, j)),
        out_shape=jax.ShapeDtypeStruct((m, n), out_dtype),
        compiler_params=_params("parallel", "parallel"),
        name="matmul",
    )(a, b)


def _mm_residual(a, b, res, gate, *, tm=512, tn=1024):
    m, k = a.shape
    n = b.shape[1]
    tm, tn = _mm_tiles(m, k, n, tm, tn)
    return pl.pallas_call(
        _mm_residual_kernel,
        grid=(n // tn, m // tm),
        in_specs=[pl.BlockSpec((tm, k), lambda j, i: (i, 0)),
                  pl.BlockSpec((k, tn), lambda j, i: (0, j)),
                  pl.BlockSpec((tm, tn), lambda j, i: (i, j)),
                  pl.BlockSpec((1, tn), lambda j, i: (0, j))],
        out_specs=pl.BlockSpec((tm, tn), lambda j, i: (i, j)),
        out_shape=jax.ShapeDtypeStruct((m, n), F32),
        compiler_params=_params("parallel", "parallel"),
        name="matmul_residual",
    )(a, b, res, gate.reshape(1, n))


def _mm_residual2_kernel(a1_ref, a2_ref, b_ref, res_ref, gate_ref, o_ref):
    k1 = a1_ref.shape[1]
    acc = jnp.dot(a1_ref[...], b_ref[:k1, :].astype(BF16), preferred_element_type=F32)
    acc = acc + jnp.dot(a2_ref[...], b_ref[k1:, :].astype(BF16), preferred_element_type=F32)
    o_ref[...] = res_ref[...] + gate_ref[...] * acc


def _mm_residual2(a1, a2, b, res, gate, *, tm=1024, tn=512):
    m, k1 = a1.shape
    k2 = a2.shape[1]
    n = b.shape[1]
    tm, tn = _mm_tiles(m, k1 + k2, n, tm, tn)
    return pl.pallas_call(
        _mm_residual2_kernel,
        grid=(n // tn, m // tm),
        in_specs=[pl.BlockSpec((tm, k1), lambda j, i: (i, 0)),
                  pl.BlockSpec((tm, k2), lambda j, i: (i, 0)),
                  pl.BlockSpec((k1 + k2, tn), lambda j, i: (0, j)),
                  pl.BlockSpec((tm, tn), lambda j, i: (i, j)),
                  pl.BlockSpec((1, tn), lambda j, i: (0, j))],
        out_specs=pl.BlockSpec((tm, tn), lambda j, i: (i, j)),
        out_shape=jax.ShapeDtypeStruct((m, n), F32),
        compiler_params=_params("parallel", "parallel"),
        name="matmul_residual2",
    )(a1, a2, b, res, gate.reshape(1, n))


def _mm_swiglu(a, b1, b3, *, tm=1024, tn=256):
    m, k = a.shape
    n = b1.shape[1]
    tm, tn = _mm_tiles(m, k, n, tm, tn)
    return pl.pallas_call(
        _mm_swiglu_kernel,
        grid=(m // tm, n // tn),
        in_specs=[pl.BlockSpec((tm, k), lambda i, j: (i, 0)),
                  pl.BlockSpec((k, tn), lambda i, j: (0, j)),
                  pl.BlockSpec((k, tn), lambda i, j: (0, j))],
        out_specs=pl.BlockSpec((tm, tn), lambda i, j: (i, j)),
        out_shape=jax.ShapeDtypeStruct((m, n), BF16),
        compiler_params=_params("parallel", "parallel"),
        name="matmul_swiglu",
    )(a, b1, b3)


def _moe_up_kernel(te_ref, nt_ref, a_ref, b1_ref, b3_ref, o_ref):
    @pl.when(pl.program_id(1) < nt_ref[0])
    def _():
        a = a_ref[...]
        g = jnp.dot(a, _bf16(b1_ref), preferred_element_type=F32)
        u = jnp.dot(a, _bf16(b3_ref), preferred_element_type=F32)
        o_ref[...] = (g * jax.nn.sigmoid(g) * u).astype(o_ref.dtype)


def _moe_down_kernel(te_ref, nt_ref, a_ref, b_ref, gate_ref, o_ref):
    @pl.when(pl.program_id(1) < nt_ref[0])
    def _():
        acc = jnp.dot(a_ref[...], _bf16(b_ref), preferred_element_type=F32)
        o_ref[...] = gate_ref[...] * acc


def _moe_grouped(xg, gate_slot, tile_expert, n_tiles, w1, w3, w2, *, tm):
    s, d = xg.shape
    f = w1.shape[2]
    n_t = s // tm
    tf = _tile(f, 512, LANES)
    td = _tile(d, 1024, LANES)

    def row(i, nt):
        return jnp.minimum(i, nt[0] - 1)

    up = pl.pallas_call(
        _moe_up_kernel,
        grid_spec=pltpu.PrefetchScalarGridSpec(
            num_scalar_prefetch=2,
            grid=(f // tf, n_t),
            in_specs=[
                pl.BlockSpec((tm, d), lambda j, i, te, nt: (row(i, nt), 0)),
                pl.BlockSpec((None, d, tf), lambda j, i, te, nt: (te[row(i, nt)], 0, j)),
                pl.BlockSpec((None, d, tf), lambda j, i, te, nt: (te[row(i, nt)], 0, j)),
            ],
            out_specs=pl.BlockSpec((tm, tf), lambda j, i, te, nt: (row(i, nt), j)),
        ),
        out_shape=jax.ShapeDtypeStruct((s, f), BF16),
        compiler_params=_params("arbitrary", "arbitrary"),
        name="moe_up",
    )(tile_expert, n_tiles, xg, w1, w3)

    return pl.pallas_call(
        _moe_down_kernel,
        grid_spec=pltpu.PrefetchScalarGridSpec(
            num_scalar_prefetch=2,
            grid=(d // td, n_t),
            in_specs=[
                pl.BlockSpec((tm, f), lambda j, i, te, nt: (row(i, nt), 0)),
                pl.BlockSpec((None, f, td), lambda j, i, te, nt: (te[row(i, nt)], 0, j)),
                pl.BlockSpec((tm, 1), lambda j, i, te, nt: (row(i, nt), 0)),
            ],
            out_specs=pl.BlockSpec((tm, td), lambda j, i, te, nt: (row(i, nt), j)),
        ),
        out_shape=jax.ShapeDtypeStruct((s, d), F32),
        compiler_params=_params("arbitrary", "arbitrary"),
        name="moe_down",
    )(tile_expert, n_tiles, up, w2, gate_slot)


def _moe(h2, route, w1, w3, w2):
    n, d = h2.shape
    n_e = w1.shape[0]
    tm = min(MOE_TILE_ROWS, n)
    n_t = -(-(TOP_K * n + n_e * (tm - 1)) // tm)
    s = n_t * tm
    expert = route[:, :TOP_K].astype(jnp.int32).T.reshape(-1)
    gate = route[:, TOP_K:2 * TOP_K].T.reshape(-1)
    token = jnp.tile(jnp.arange(n, dtype=jnp.int32), TOP_K)
    onehot = (expert[:, None] == jnp.arange(n_e, dtype=jnp.int32)[None]).astype(jnp.int32)
    csum = jnp.cumsum(onehot, axis=0)
    rank = jnp.sum(onehot * csum, axis=1) - 1
    counts = csum[-1]
    padded = -(-counts // tm) * tm
    ends = jnp.cumsum(padded)
    starts = ends - padded
    slot = starts[expert] + rank
    token_of_slot = jnp.zeros((s,), jnp.int32).at[slot].set(token)
    gate_of_slot = jnp.zeros((s,), F32).at[slot].set(gate)
    n_tiles = (ends[-1] // tm).astype(jnp.int32).reshape(1)
    tile_start = jnp.arange(n_t, dtype=jnp.int32) * tm
    tile_expert = jnp.minimum(jnp.sum((tile_start[:, None] >= ends[None]).astype(jnp.int32), axis=1), n_e - 1)
    xg = jnp.take(h2, token_of_slot, axis=0)
    yo = _moe_grouped(xg, gate_of_slot.reshape(s, 1), tile_expert, n_tiles, w1, w3, w2, tm=tm)
    slots = slot.reshape(TOP_K, n)
    return [jnp.take(yo, slots[kk], axis=0) for kk in range(TOP_K)]


def _dwconv_kernel(prev_ref, cur_ref, next_ref, w_ref, b_ref, o_ref, *, pad_left, silu):
    i = pl.program_id(1)
    cur = cur_ref[...]
    rows = cur.shape[0]
    prev = jnp.where(i > 0, prev_ref[...], 0.0)
    nxt = jnp.where(i < pl.num_programs(1) - 1, next_ref[...], 0.0)
    ext = jnp.concatenate([prev, cur, nxt], axis=0)
    acc = jnp.zeros_like(cur) + b_ref[...]
    for j in range(w_ref.shape[0]):
        s = j - pad_left
        us = cur if s == 0 else pltpu.roll(ext, (-s) % ext.shape[0], axis=0)[SUBLANES:SUBLANES + rows]
        acc = acc + w_ref[j:j + 1, :] * us
    if silu:
        acc = acc * jax.nn.sigmoid(acc)
    o_ref[...] = acc


def _dwconv(u, w, b, pad_left, *, col0=0, silu=False):
    n = u.shape[0]
    k, c = w.shape
    assert k - 1 <= SUBLANES
    tc = _tile(math.gcd(c, col0) if col0 else c, 512, LANES)
    off = col0 // tc
    tr = _tile(n, 1024, SUBLANES)
    per = tr // SUBLANES
    last = n // SUBLANES - 1
    return pl.pallas_call(
        functools.partial(_dwconv_kernel, pad_left=pad_left, silu=silu),
        grid=(c // tc, n // tr),
        in_specs=[pl.BlockSpec((SUBLANES, tc), lambda j, i: (jnp.maximum(i * per - 1, 0), j + off)),
                  pl.BlockSpec((tr, tc), lambda j, i: (i, j + off)),
                  pl.BlockSpec((SUBLANES, tc), lambda j, i: (jnp.minimum((i + 1) * per, last), j + off)),
                  pl.BlockSpec((k, tc), lambda j, i: (0, j)),
                  pl.BlockSpec((1, tc), lambda j, i: (0, j))],
        out_specs=pl.BlockSpec((tr, tc), lambda j, i: (i, j)),
        out_shape=jax.ShapeDtypeStruct((n, c), F32),
        compiler_params=_params("parallel", "parallel"),
        name="dwconv",
    )(u, u, u, w.astype(F32), b.astype(F32).reshape(1, c))


def _ssd_kernel(xf_ref, bf_ref, cf_ref, dtf_ref, xb_ref, bb_ref, cb_ref, dtb_ref, bias_ref, a_ref, s0_ref,
                yf_ref, yb_ref, st_ref, state, *, heads_per_group, head_dim):
    step = pl.program_id(1)

    @pl.when(step == 0)
    def _():
        state[...] = s0_ref[...]

    q, gw = xf_ref.shape
    hp = lax.Precision.HIGHEST
    nt = (((1,), (1,)), ((), ()))
    row = lax.broadcasted_iota(jnp.int32, (q, q), 0)
    col = lax.broadcasted_iota(jnp.int32, (q, q), 1)
    tri = (row >= col).astype(F32)
    neg = jnp.float32(-jnp.inf)
    head_of_col = lax.broadcasted_iota(jnp.int32, (LANES, gw), 1) // head_dim
    lane_id = lax.broadcasted_iota(jnp.int32, (LANES, gw), 0)
    low_half = lax.broadcasted_iota(jnp.int32, (q, LANES), 1) < head_dim
    heads_per_tile = LANES // head_dim
    streams = ((xf_ref, bf_ref, cf_ref, dtf_ref, yf_ref), (xb_ref, bb_ref, cb_ref, dtb_ref, yb_ref))
    for d, (x_ref, b_ref, c_ref, dt_ref, y_ref) in enumerate(streams):
        xs = x_ref[...]
        bm = b_ref[...].astype(BF16)
        cm = c_ref[...].astype(BF16)
        dt = jax.nn.softplus(dt_ref[...] + bias_ref[...])
        da = dt * a_ref[...]
        cum = jnp.dot(tri, da, preferred_element_type=F32, precision=hp)
        total = cum[q - 1:q, :]
        pos = cum if d == 0 else cum - da
        pos_t = pos.T
        if d == 0:
            carry_in, carry_out = jnp.exp(pos), jnp.exp(total - pos)
        else:
            carry_in, carry_out = jnp.exp(total - pos), jnp.exp(pos)
        spread = (lane_id == head_of_col + d * heads_per_group).astype(F32)
        xd = xs * jnp.dot(dt, spread, preferred_element_type=F32, precision=hp)
        st = state[d]
        y_off = (lax.dot_general(cm, st.astype(BF16), nt, preferred_element_type=F32)
                 * jnp.dot(carry_in, spread, preferred_element_type=F32, precision=hp))
        xd_out = (xd * jnp.dot(carry_out, spread, preferred_element_type=F32, precision=hp)).astype(BF16)
        upd = lax.dot_general(xd_out, bm, (((0,), (0,)), ((), ())), preferred_element_type=F32)
        keep = lax.dot_general(spread, jnp.exp(total), (((0,), (1,)), ((), ())),
                               preferred_element_type=F32, precision=hp)
        state[d] = keep * st + upd
        cb = lax.dot_general(cm, bm, nt, preferred_element_type=F32)
        mask = (row >= col) if d == 0 else (col >= row)
        xdb = xd.astype(BF16)
        for tile in range(gw // LANES):
            cols = slice(tile * LANES, (tile + 1) * LANES)
            parts = []
            for e in range(tile * heads_per_tile, (tile + 1) * heads_per_tile):
                lane = d * heads_per_group + e
                p_col = pos[:, lane:lane + 1]
                p_row = pos_t[lane:lane + 1, :]
                expo = (p_col - p_row) if d == 0 else (p_row - p_col)
                dec = jnp.exp(jnp.where(mask, expo, neg))
                parts.append(jnp.dot((cb * dec).astype(BF16), xdb[:, cols], preferred_element_type=F32))
            y_diag = parts[0] if len(parts) == 1 else jnp.where(low_half, parts[0], parts[1])
            y_ref[:, cols] = y_diag + y_off[:, cols]

    @pl.when(step == pl.num_programs(1) - 1)
    def _():
        st_ref[...] = state[...]


def _ssd_scan(xbc, dtg, bias_g, a_g, s0, *, ssd_w, groups):
    n = xbc.shape[0]
    e = ssd_w // (SSD_P * groups)
    q = min(SSD_CHUNK, n)
    steps = n // q
    gw = e * SSD_P
    nb = ssd_w // SSD_N
    assert LANES % SSD_P == 0 and LANES // SSD_P <= 2 and gw % LANES == 0 and 2 * e <= LANES

    def fwd(blk):
        return lambda g, s: (s, blk(g))

    def bwd(blk):
        return lambda g, s: (steps - 1 - s, blk(g))

    def stream(order):
        return [pl.BlockSpec((q, gw), order(lambda g: g)),
                pl.BlockSpec((q, SSD_N), order(lambda g: nb + g)),
                pl.BlockSpec((q, SSD_N), order(lambda g: nb + groups + g))]

    grp = pl.BlockSpec((None, 1, LANES), lambda g, s: (g, 0, 0))
    st_spec = pl.BlockSpec((2, None, gw, SSD_N), lambda g, s: (0, g, 0, 0))
    return pl.pallas_call(
        functools.partial(_ssd_kernel, heads_per_group=e, head_dim=SSD_P),
        grid=(groups, steps),
        in_specs=(stream(fwd) + [pl.BlockSpec((q, LANES), fwd(lambda g: g))]
                  + stream(bwd) + [pl.BlockSpec((q, LANES), bwd(lambda g: g))]
                  + [grp, grp, st_spec]),
        out_specs=[pl.BlockSpec((q, gw), fwd(lambda g: g)), pl.BlockSpec((q, gw), bwd(lambda g: g)), st_spec],
        out_shape=[jax.ShapeDtypeStruct((n, ssd_w), F32), jax.ShapeDtypeStruct((n, ssd_w), F32),
                   jax.ShapeDtypeStruct(s0.shape, F32)],
        scratch_shapes=[pltpu.VMEM((2, gw, SSD_N), F32)],
        compiler_params=_params("parallel", "arbitrary"),
        name="ssd_scan",
    )(xbc, xbc, xbc, dtg, xbc, xbc, xbc, dtg, bias_g, a_g, s0)


def _ssd_gate_norm_kernel(yf_ref, yb_ref, xs_ref, z_ref, d_ref, nw_ref, o_ref, *, groups):
    y = yf_ref[...] + yb_ref[...] + xs_ref[...] * d_ref[...]
    z = z_ref[...]
    y = y * (z * jax.nn.sigmoid(z))
    gw = y.shape[1] // groups
    for g in range(groups):
        yg = y[:, g * gw:(g + 1) * gw]
        yg = yg * lax.rsqrt(jnp.mean(yg * yg, axis=-1, keepdims=True) + NORM_EPS)
        o_ref[:, g * gw:(g + 1) * gw] = (yg * nw_ref[:, g * gw:(g + 1) * gw]).astype(o_ref.dtype)


def _ssd_gate_norm(y_f, y_b, xbc, zx, d_cols, norm_w, *, groups):
    n, w = y_f.shape
    tr = _tile(n, 256, SUBLANES)
    rows = pl.BlockSpec((tr, w), lambda i: (i, 0))
    vec = pl.BlockSpec((1, w), lambda i: (0, 0))
    return pl.pallas_call(
        functools.partial(_ssd_gate_norm_kernel, groups=groups),
        grid=(n // tr,),
        in_specs=[rows, rows, rows, rows, vec, vec],
        out_specs=rows,
        out_shape=jax.ShapeDtypeStruct((n, w), BF16),
        compiler_params=_params("parallel"),
        name="ssd_gate_norm",
    )(y_f, y_b, xbc, zx, d_cols.reshape(1, w), norm_w.astype(F32).reshape(1, w))


def _na_kernel(q_ref, k_ref, v_ref, kc_ref, vc_ref, bias_ref, o_ref, *, grid_rows, win_h, rows_per_block, scale):
    blk = pl.program_id(1)
    kc = kc_ref[...]
    vc = vc_ref[...]
    nt = (((1,), (1,)), ((), ()))

    def one_row(i, carry):
        r = blk * rows_per_block + i
        rs = jnp.clip(r - win_h // 2, 0, grid_rows - win_h)
        q = q_ref[pl.ds(pl.multiple_of(i * GRID_W, GRID_W), GRID_W), :]
        start = pl.multiple_of(rs * GRID_W, GRID_W)
        kw = k_ref[pl.ds(start, win_h * GRID_W), :]
        vw = v_ref[pl.ds(start, win_h * GRID_W), :]
        s = lax.dot_general(q, kw, nt, preferred_element_type=F32) * scale + bias_ref[r - rs]
        sc = lax.dot_general(q, kc, nt, preferred_element_type=F32) * scale
        m = jnp.maximum(jnp.max(s, axis=-1, keepdims=True), jnp.max(sc, axis=-1, keepdims=True))
        p = jnp.exp(s - m)
        pc = jnp.exp(sc - m)
        denom = jnp.sum(p, axis=-1, keepdims=True) + jnp.sum(pc, axis=-1, keepdims=True)
        o = (jnp.dot(p.astype(BF16), vw, preferred_element_type=F32)
             + jnp.dot(pc.astype(BF16), vc, preferred_element_type=F32))
        o_ref[pl.ds(pl.multiple_of(i * GRID_W, GRID_W), GRID_W), :] = (o / denom).astype(o_ref.dtype)
        return carry

    lax.fori_loop(0, rows_per_block, one_row, 0, unroll=True)


def _na_bias_table(rpb, win_h):
    n_dcol = 2 * NA_WIN_W - 1
    p = np.arange(win_h)[:, None]
    a = np.arange(win_h)[None, :]
    rows = rpb.astype(F32)[:, a - p + NA_WIN_H - 1, :]
    qc = np.arange(GRID_W)[:, None]
    kc = np.arange(GRID_W)[None, :]
    dcol = np.clip(kc - qc + NA_WIN_W - 1, 0, n_dcol - 1)
    onehot = (np.arange(n_dcol)[:, None, None] == dcol[None]).astype(np.float32)
    bias = jnp.einsum('hpad,dqk->hpqak', rows, jnp.asarray(onehot), precision=lax.Precision.HIGHEST)
    cstart = np.clip(qc - NA_WIN_W // 2, 0, GRID_W - NA_WIN_W)
    ok = (kc >= cstart) & (kc < cstart + NA_WIN_W)
    bias = jnp.where(ok[None, None, :, None, :], bias, -jnp.inf)
    return bias.reshape(rpb.shape[0], win_h, GRID_W, win_h * GRID_W)


def _neighbourhood_attention(qkv, qkv_c, rpb, *, heads):
    n = qkv.shape[0]
    c = qkv_c.shape[0]
    grid_rows = n // GRID_W
    win_h = min(NA_WIN_H, grid_rows)
    rows_per_block = _tile(grid_rows, 8, 1)
    tq = rows_per_block * GRID_W
    bias = _na_bias_table(rpb, win_h)
    return pl.pallas_call(
        functools.partial(_na_kernel, grid_rows=grid_rows, win_h=win_h, rows_per_block=rows_per_block,
                          scale=NA_DH ** -0.5),
        grid=(heads, n // tq),
        in_specs=[pl.BlockSpec((tq, NA_DH), lambda h, i: (i, h)),
                  pl.BlockSpec((n, NA_DH), lambda h, i: (0, heads + h)),
                  pl.BlockSpec((n, NA_DH), lambda h, i: (0, 2 * heads + h)),
                  pl.BlockSpec((c, NA_DH), lambda h, i: (0, heads + h)),
                  pl.BlockSpec((c, NA_DH), lambda h, i: (0, 2 * heads + h)),
                  pl.BlockSpec((None, win_h, GRID_W, win_h * GRID_W), lambda h, i: (h, 0, 0, 0))],
        out_specs=pl.BlockSpec((tq, NA_DH), lambda h, i: (i, h)),
        out_shape=jax.ShapeDtypeStruct((n, heads * NA_DH), BF16),
        compiler_params=_params("parallel", "arbitrary"),
        name="neighbourhood_attention",
    )(qkv, qkv, qkv, qkv_c, qkv_c, bias)


def _ctx_attn_kernel(q_ref, k_ref, v_ref, o_ref, *, scale):
    s = lax.dot_general(q_ref[...], k_ref[...], (((1,), (1,)), ((), ())), preferred_element_type=F32) * scale
    p = jnp.exp(s - jnp.max(s, axis=-1, keepdims=True))
    o = jnp.dot(p.astype(BF16), v_ref[...], preferred_element_type=F32)
    o_ref[...] = (o / jnp.sum(p, axis=-1, keepdims=True)).astype(o_ref.dtype)


def _context_attention(qkv_c, *, heads):
    c = qkv_c.shape[0]
    return pl.pallas_call(
        functools.partial(_ctx_attn_kernel, scale=NA_DH ** -0.5),
        grid=(heads,),
        in_specs=[pl.BlockSpec((c, NA_DH), lambda h: (0, h)),
                  pl.BlockSpec((c, NA_DH), lambda h: (0, heads + h)),
                  pl.BlockSpec((c, NA_DH), lambda h: (0, 2 * heads + h))],
        out_specs=pl.BlockSpec((c, NA_DH), lambda h: (0, h)),
        out_shape=jax.ShapeDtypeStruct((c, heads * NA_DH), BF16),
        compiler_params=_params("parallel"),
        name="context_attention",
    )(qkv_c, qkv_c, qkv_c)


def _rglru_kernel(*refs, chunk, emit_y):
    if emit_y:
        u_ref, wg_ref, bg_ref, lam_ref, h0_ref, gate_ref, y_ref, ht_ref, hsum = refs
    else:
        u_ref, wg_ref, bg_ref, lam_ref, h0_ref, ht_ref = refs
    n, w = u_ref.shape
    n_chunks = n // chunk
    tiles = chunk // SUBLANES
    sub = lax.broadcasted_iota(jnp.int32, (chunk, w), 0) % SUBLANES
    log_sig = jax.nn.log_sigmoid(lam_ref[...])

    def chunk_scan(d, c0, carry):
        u = u_ref[pl.ds(c0, chunk), :]
        ub = u.astype(BF16)
        r = jax.nn.sigmoid(jnp.dot(ub, wg_ref[d, 0], preferred_element_type=F32) + bg_ref[d, 0])
        i = jax.nn.sigmoid(jnp.dot(ub, wg_ref[d, 1], preferred_element_type=F32) + bg_ref[d, 1])
        log_a = RG_C * r * log_sig[d]
        a = jnp.exp(log_a)
        b = jnp.sqrt(jnp.maximum(1.0 - a * a, 0.0)) * i * u
        for sh in (1, 2, 4):
            if d == 0:
                a_s, b_s, edge = pltpu.roll(a, sh, axis=0), pltpu.roll(b, sh, axis=0), sub < sh
            else:
                a_s, b_s = pltpu.roll(a, chunk - sh, axis=0), pltpu.roll(b, chunk - sh, axis=0)
                edge = sub >= SUBLANES - sh
            b = jnp.where(edge, b, a * b_s + b)
            a = jnp.where(edge, a, a * a_s)
        hs = [None] * tiles
        order = range(tiles) if d == 0 else range(tiles - 1, -1, -1)
        for k in order:
            sl = slice(k * SUBLANES, (k + 1) * SUBLANES)
            h = a[sl] * carry + b[sl]
            hs[k] = h
            carry = h[SUBLANES - 1:SUBLANES] if d == 0 else h[0:1]
        return jnp.concatenate(hs, axis=0), carry

    def step(j, carries, second_pass):
        cf, cb = carries
        rows_f = pl.ds(pl.multiple_of(j * chunk, chunk), chunk)
        rows_b = pl.ds(pl.multiple_of((n_chunks - 1 - j) * chunk, chunk), chunk)
        hf, cf = chunk_scan(0, pl.multiple_of(j * chunk, chunk), cf)
        hb, cb = chunk_scan(1, pl.multiple_of((n_chunks - 1 - j) * chunk, chunk), cb)
        if emit_y:
            if second_pass:
                y_ref[rows_f, :] = (jax.nn.gelu(gate_ref[rows_f, :]) * (hsum[rows_f, :] + hf)).astype(y_ref.dtype)
                y_ref[rows_b, :] = (jax.nn.gelu(gate_ref[rows_b, :]) * (hsum[rows_b, :] + hb)).astype(y_ref.dtype)
            else:
                hsum[rows_f, :] = hf
                hsum[rows_b, :] = hb
        return cf, cb

    carries = (h0_ref[0], h0_ref[1])
    half = n_chunks // 2
    carries = lax.fori_loop(0, half, lambda j, c: step(j, c, False), carries)
    carries = lax.fori_loop(half, n_chunks, lambda j, c: step(j, c, True), carries)
    ht_ref[0] = carries[0]
    ht_ref[1] = carries[1]


def _rglru(u, gate_src, gate_w, gate_b, lam, h0):
    n, w = u.shape
    nb, bw = gate_w.shape[2], gate_w.shape[3]
    chunk = _tile(n // 2, 256, SUBLANES)
    assert (n // chunk) % 2 == 0, "the two sweeps hand over at the middle chunk boundary"
    emit_y = gate_src is not None
    col = lambda j: (0, j)
    in_specs = [pl.BlockSpec((n, bw), col),
                pl.BlockSpec((2, 2, None, bw, bw), lambda j: (0, 0, j, 0, 0)),
                pl.BlockSpec((2, 2, None, 1, bw), lambda j: (0, 0, j, 0, 0)),
                pl.BlockSpec((2, 1, bw), lambda j: (0, 0, j)),
                pl.BlockSpec((2, 1, bw), lambda j: (0, 0, j))]
    args = [u, gate_w.astype(BF16), gate_b.astype(F32).reshape(2, 2, nb, 1, bw), lam.astype(F32).reshape(2, 1, w),
            h0.reshape(2, 1, w)]
    ht_spec = pl.BlockSpec((2, 1, bw), lambda j: (0, 0, j))
    ht_shape = jax.ShapeDtypeStruct((2, 1, w), F32)
    if emit_y:
        in_specs.append(pl.BlockSpec((n, bw), col))
        args.append(gate_src)
        out_specs = [pl.BlockSpec((n, bw), col), ht_spec]
        out_shape = [jax.ShapeDtypeStruct((n, w), BF16), ht_shape]
        scratch = [pltpu.VMEM((n, bw), F32)]
    else:
        out_specs, out_shape, scratch = ht_spec, ht_shape, []
    out = pl.pallas_call(
        functools.partial(_rglru_kernel, chunk=chunk, emit_y=emit_y),
        grid=(nb,),
        in_specs=in_specs,
        out_specs=out_specs,
        out_shape=out_shape,
        scratch_shapes=scratch,
        compiler_params=_params("parallel"),
        name="rglru",
    )(*args)
    if emit_y:
        return out[0], out[1].reshape(2, w)
    return None, out.reshape(2, w)


def _dft_tables(n):
    n1 = 1 << (int(math.log2(n)) // 2)
    n2 = n // n1
    assert n1 * n2 == n and n2 % 2 == 0
    i1, i2 = np.arange(n1), np.arange(n2)
    f1 = np.exp(-2j * np.pi * np.outer(i1, i1) / n1)
    f2 = np.exp(-2j * np.pi * np.outer(i2, i2) / n2)
    tw = np.exp(-2j * np.pi * np.outer(i2, i1) / n)
    return n1, n2, f1, f2, tw


def _stack_ri(m, sign=1.0):
    return jnp.asarray(np.concatenate([m.real, sign * m.imag], axis=0), BF16)


def _hy_hidden_kernel(feat_ref, w_in_ref, b_in_ref, w_mid_ref, b_mid_ref, freq_ref, o_ref):
    hp = lax.Precision.HIGHEST
    freq = freq_ref[...]
    hid = jnp.sin(freq * (jnp.dot(feat_ref[...], w_in_ref[...], preferred_element_type=F32, precision=hp)
                          + b_in_ref[...]))
    for m in range(w_mid_ref.shape[0]):
        hid = jnp.sin(freq * (jnp.dot(hid, w_mid_ref[m], preferred_element_type=F32, precision=hp)
                              + b_mid_ref[m]))
    o_ref[...] = hid


def _hy_hidden(L, w_in, b_in, w_mid, b_mid, freq):
    t = jnp.linspace(0.0, 1.0, L, dtype=F32)[:, None]
    ang = ((2.0 * math.pi / L) * jnp.arange(L, dtype=F32)[:, None]
           * jnp.linspace(1e-4, HY_BANDS - 1, HY_BANDS, dtype=F32)[None])
    emb, ffn = w_in.shape
    assert ffn <= LANES
    feats = jnp.zeros((L, LANES), F32).at[:, :emb].set(jnp.concatenate([t, jnp.cos(ang), -jnp.sin(ang)], axis=-1))

    def padded(v, shape):
        return jnp.zeros(shape, F32).at[tuple(slice(0, n) for n in v.shape)].set(v.astype(F32))

    tr = _tile(L, 512, SUBLANES)
    full = lambda *shape: pl.BlockSpec(shape, lambda *_: (0,) * len(shape))
    return pl.pallas_call(
        _hy_hidden_kernel,
        grid=(L // tr,),
        in_specs=[pl.BlockSpec((tr, LANES), lambda i: (i, 0)), full(LANES, LANES), full(1, LANES),
                  full(HY_INNER, LANES, LANES), full(HY_INNER, 1, LANES), full(1, LANES)],
        out_specs=pl.BlockSpec((tr, LANES), lambda i: (i, 0)),
        out_shape=jax.ShapeDtypeStruct((L, LANES), F32),
        compiler_params=_params("parallel"),
        name="hyena_filter_hidden",
    )(feats, padded(w_in, (LANES, LANES)), padded(b_in[None], (1, LANES)), padded(w_mid, (HY_INNER, LANES, LANES)),
      padded(b_mid[:, None], (HY_INNER, 1, LANES)), padded(freq[None], (1, LANES)))


def _hy_filter_stage1_kernel(hf_ref, hb_ref, wf_ref, wb_ref, df_ref, db_ref, f2f_ref, f2b_ref, twr_ref, twi_ref,
                             o_ref, *, n1, length):
    i = pl.program_id(0)
    half = hf_ref.shape[0]
    n2 = o_ref.shape[1]
    row = lax.broadcasted_iota(jnp.int32, (half, 1), 0)
    scale = 1.0 / (length - 1)

    def taps(h_ref, w_ref, d_ref, first):
        t = (first + n1 * row).astype(F32) * scale
        y = jnp.dot(h_ref[...].astype(BF16), w_ref[...], preferred_element_type=F32)
        return (y * jnp.exp(-t * jnp.abs(d_ref[...]))).astype(BF16)

    f2b = jnp.where(i == 0, f2b_ref[0], f2b_ref[1])
    p = (jnp.dot(f2f_ref[...], taps(hf_ref, wf_ref, df_ref, i), preferred_element_type=F32)
         + jnp.dot(f2b, taps(hb_ref, wb_ref, db_ref, (n1 - i) % n1), preferred_element_type=F32))
    ar, ai = p[:n2], p[n2:]
    tr, ti = twr_ref[...], twi_ref[...]
    o_ref[0] = (ar * tr - ai * ti).astype(o_ref.dtype)
    o_ref[1] = (ar * ti + ai * tr).astype(o_ref.dtype)


def _hy_filter_stage1(hid, w_out, deltas, c, tabs):
    n1, n2, _, f2, tw = tabs
    half = n2 // 2
    L = hid.shape[0]
    cols = HY_ORDER * c

    def split(v):
        v4 = v.reshape(v.shape[0], HY_ORDER, 2, c)
        return v4[:, :, 0].reshape(v.shape[0], cols), v4[:, :, 1].reshape(v.shape[0], cols)

    w_f, w_b = split(jnp.zeros((LANES, w_out.shape[1]), BF16).at[:w_out.shape[0]].set(w_out.astype(BF16)))
    d_f, d_b = split(deltas.astype(F32).reshape(1, -1))
    j = np.arange(half)
    f2b = jnp.stack([_stack_ri(f2[:, (n2 - j) % n2]), _stack_ri(f2[:, n2 - 1 - j])])
    hid2d = hid.reshape(half, n1 * LANES)
    twspec = pl.BlockSpec((None, n2, 1), lambda i: (i, 0, 0))
    full = lambda *shape: pl.BlockSpec(shape, lambda i: (0,) * len(shape))
    return pl.pallas_call(
        functools.partial(_hy_filter_stage1_kernel, n1=n1, length=L),
        grid=(n1,),
        in_specs=[pl.BlockSpec((half, LANES), lambda i: (0, i)),
                  pl.BlockSpec((half, LANES), lambda i: (0, (n1 - i) % n1)),
                  full(LANES, cols), full(LANES, cols), full(1, cols), full(1, cols),
                  full(2 * n2, half), full(2, 2 * n2, half), twspec, twspec],
        out_specs=pl.BlockSpec((2, n2, cols), lambda i: (0, 0, i)),
        out_shape=jax.ShapeDtypeStruct((2, n2, n1 * cols), BF16),
        compiler_params=_params("parallel"),
        name="hyena_filter_stage1",
    )(hid2d, hid2d, w_f, w_b, d_f, d_b, _stack_ri(f2[:, :half]), f2b,
      jnp.asarray(tw.real.T.reshape(n1, n2, 1), F32), jnp.asarray(tw.imag.T.reshape(n1, n2, 1), F32)
      ).reshape(2, n2, n1, cols)


def _hy_filter_spectrum_kernel(a_ref, f1_ref, h_ref):
    n1 = h_ref.shape[1]
    f1 = f1_ref[...]
    p = jnp.dot(f1, a_ref[0], preferred_element_type=F32)
    q = jnp.dot(f1, a_ref[1], preferred_element_type=F32)
    h_ref[0] = p[:n1] - q[n1:]
    h_ref[1] = p[n1:] + q[:n1]


def _hy_filter_spectrum(fa, c, tabs):
    n1, n2, f1, _, _ = tabs
    return pl.pallas_call(
        _hy_filter_spectrum_kernel,
        grid=(HY_ORDER, n2),
        in_specs=[pl.BlockSpec((2, None, n1, c), lambda o, k: (0, k, 0, o)),
                  pl.BlockSpec((2 * n1, n1), lambda o, k: (0, 0))],
        out_specs=pl.BlockSpec((None, 2, None, n1, c), lambda o, k: (o, 0, k, 0, 0)),
        out_shape=jax.ShapeDtypeStruct((HY_ORDER, 2, n2, n1, c), F32),
        compiler_params=_params("parallel", "parallel"),
        name="hyena_filter_spectrum",
    )(fa, _stack_ri(f1))


def _hy_stage1_kernel(x_ref, f2_ref, twr_ref, twi_ref, o_ref):
    n2 = o_ref.shape[1]
    for j in range(x_ref.shape[1]):
        p = jnp.dot(f2_ref[...], x_ref[:, j, :].astype(BF16), preferred_element_type=F32)
        ar, ai = p[:n2], p[n2:]
        tr, ti = twr_ref[j], twi_ref[j]
        o_ref[0, :, j, :] = ar * tr - ai * ti
        o_ref[1, :, j, :] = ar * ti + ai * tr


def _hy_stage1(src, part, c, tabs):
    n1, n2, _, f2, tw = tabs
    half = n2 // 2
    r = SUBLANES
    cb = _tile(c, 1024, LANES)
    per = c // cb
    twspec = pl.BlockSpec((r, n2, 1), lambda i, k: (i, 0, 0))
    return pl.pallas_call(
        _hy_stage1_kernel,
        grid=(n1 // r, per),
        in_specs=[pl.BlockSpec((half, r, cb), lambda i, k: (0, i, part * per + k)),
                  pl.BlockSpec((2 * n2, half), lambda i, k: (0, 0)), twspec, twspec],
        out_specs=pl.BlockSpec((2, n2, r, cb), lambda i, k: (0, 0, i, k)),
        out_shape=jax.ShapeDtypeStruct((2, n2, n1, c), F32),
        compiler_params=_params("parallel", "parallel"),
        name="hyena_dft_stage1",
    )(src.reshape(half, n1, src.shape[1]), _stack_ri(f2[:, :half]),
      jnp.asarray(tw.real.T.reshape(n1, n2, 1), F32), jnp.asarray(tw.imag.T.reshape(n1, n2, 1), F32))


def _hy_mid_kernel(a_ref, h_ref, f1_ref, twr_ref, twi_ref, o_ref):
    n1 = o_ref.shape[1]
    f1 = f1_ref[...]
    p = jnp.dot(f1, a_ref[0].astype(BF16), preferred_element_type=F32)
    q = jnp.dot(f1, a_ref[1].astype(BF16), preferred_element_type=F32)
    xr = p[:n1] - q[n1:]
    xi = p[n1:] + q[:n1]
    hr, hi = h_ref[0], h_ref[1]
    yr = (xr * hr - xi * hi).astype(BF16)
    yi = (xr * hi + xi * hr).astype(BF16)
    p = jnp.dot(f1, yr, preferred_element_type=F32)
    q = jnp.dot(f1, yi, preferred_element_type=F32)
    br = p[:n1] + q[n1:]
    bi = q[:n1] - p[n1:]
    tr, ti = twr_ref[...], twi_ref[...]
    o_ref[0] = br * tr + bi * ti
    o_ref[1] = bi * tr - br * ti


def _hy_stage3_kernel(b_ref, f2_ref, z_ref, gate_ref, skip_ref, o_ref, *, inv_n):
    _, n2, r, c = b_ref.shape
    for j in range(r):
        b = b_ref[:, :, j, :].reshape(2 * n2, c).astype(BF16)
        acc = jnp.dot(f2_ref[...], b, preferred_element_type=F32)
        o_ref[:, j, :] = (gate_ref[:, j, :] * (acc * inv_n + z_ref[:, j, :] * skip_ref[...])).astype(o_ref.dtype)


def _hy_long_conv(z_src, z_part, gate_src, gate_part, spec, order, skip, tabs, out_dtype):
    n1, n2, f1, f2, tw = tabs
    L = z_src.shape[0]
    c = skip.shape[0]
    half = n2 // 2
    r = SUBLANES
    cb = _tile(c, 1024, LANES)
    per = c // cb
    slab = pl.BlockSpec((2, None, n1, c), lambda k: (0, k, 0, 0))
    a = _hy_stage1(z_src, z_part, c, tabs)
    twspec = pl.BlockSpec((None, n1, 1), lambda k: (k, 0, 0))
    b = pl.pallas_call(
        _hy_mid_kernel,
        grid=(n2,),
        in_specs=[slab, pl.BlockSpec((None, 2, None, n1, c), lambda k: (order, 0, k, 0, 0)),
                  pl.BlockSpec((2 * n1, n1), lambda k: (0, 0)), twspec, twspec],
        out_specs=slab,
        out_shape=jax.ShapeDtypeStruct((2, n2, n1, c), F32),
        compiler_params=_params("parallel"),
        name="hyena_spectral_product",
    )(a, spec, _stack_ri(f1), jnp.asarray(tw.real.reshape(n2, n1, 1), F32),
      jnp.asarray(tw.imag.reshape(n2, n1, 1), F32))
    f2c = jnp.asarray(np.concatenate([f2.real[:half], f2.imag[:half]], axis=1), BF16)
    out = pl.pallas_call(
        functools.partial(_hy_stage3_kernel, inv_n=1.0 / (n1 * n2)),
        grid=(n1 // r, per),
        in_specs=[pl.BlockSpec((2, n2, r, cb), lambda i, k: (0, 0, i, k)),
                  pl.BlockSpec((half, 2 * n2), lambda i, k: (0, 0)),
                  pl.BlockSpec((half, r, cb), lambda i, k: (0, i, z_part * per + k)),
                  pl.BlockSpec((half, r, cb), lambda i, k: (0, i, gate_part * per + k)),
                  pl.BlockSpec((1, cb), lambda i, k: (0, k))],
        out_specs=pl.BlockSpec((half, r, cb), lambda i, k: (0, i, k)),
        out_shape=jax.ShapeDtypeStruct((half, n1, c), out_dtype),
        compiler_params=_params("parallel", "parallel"),
        name="hyena_dft_stage3",
    )(b, f2c, z_src.reshape(half, n1, z_src.shape[1]), gate_src.reshape(half, n1, gate_src.shape[1]),
      skip.astype(F32).reshape(1, c))
    return out.reshape(L, c)


def _hyena(p, col0, conv_w, conv_b, w_in, b_in, w_mid, b_mid, w_out, freq, deltas, skip):
    assert HY_ORDER == 2
    L = p.shape[0]
    c = skip.shape[1]
    tabs = _dft_tables(2 * L)
    u = _dwconv(p, conv_w, conv_b, (HY_CONV - 1) // 2, col0=col0)
    hid = _hy_hidden(L, w_in, b_in, w_mid, b_mid, freq)
    spec = _hy_filter_spectrum(_hy_filter_stage1(hid, w_out, deltas, c, tabs), c, tabs)
    z = _hy_long_conv(u, 2, u, 0, spec, 0, skip[0], tabs, F32)
    return _hy_long_conv(z, 0, u, 1, spec, 1, skip[1], tabs, BF16)


def _even_layer_mix(h, hc, in_w, conv_w, conv_b, dt_bias, a_log, d_skip, norm_w, rpb):
    d = h.shape[1]
    ssd_w = norm_w.shape[0]
    heads = ssd_w // SSD_P
    e = heads // SSD_G
    na_heads = (d - ssd_w) // NA_DH
    o1 = ssd_w
    o2 = o1 + conv_w.shape[1]
    o3 = o2 + 2 * heads
    w_qkv = in_w[:, o3:].astype(BF16)
    src = np.arange(2 * heads)
    direction, head = src // heads, src % heads
    place = np.zeros((2 * heads, SSD_G * LANES), np.float32)
    place[src, (head // e) * LANES + direction * e + head % e] = 1.0
    w_dt = jnp.dot(in_w[:, o2:o3].astype(F32), jnp.asarray(place), precision=lax.Precision.HIGHEST).astype(BF16)

    def grouped(v):
        vg = v.astype(F32).reshape(2, SSD_G, e).transpose(1, 0, 2).reshape(SSD_G, 2 * e)
        return jnp.zeros((SSD_G, LANES), F32).at[:, :2 * e].set(vg).reshape(SSD_G, 1, LANES)

    bias_g = grouped(dt_bias)
    a_g = grouped(-jnp.exp(a_log.astype(F32)))
    d_cols = jnp.repeat(d_skip.astype(F32), SSD_P)

    def ssd(t, s0):
        zx = _mm(t, in_w, n=o2)
        xbc = _dwconv(zx, conv_w, conv_b, SSD_CONV // 2, col0=o1, silu=True)
        y_f, y_b, s_t = _ssd_scan(xbc, _mm(t, w_dt), bias_g, a_g, s0, ssd_w=ssd_w, groups=SSD_G)
        return _ssd_gate_norm(y_f, y_b, xbc, zx, d_cols, norm_w, groups=SSD_G), s_t

    zero = jnp.zeros((2, SSD_G, e * SSD_P, SSD_N), F32)
    qkv_c = _mm(hc, w_qkv, out_dtype=BF16)
    y_ssd_c, s_c = ssd(hc, zero)
    qkv = _mm(h, w_qkv, out_dtype=BF16)
    y_ssd, _ = ssd(h, s_c)
    y_na = _neighbourhood_attention(qkv, qkv_c, rpb, heads=na_heads)
    y_na_c = _context_attention(qkv_c, heads=na_heads)
    return (y_ssd, y_na), (y_ssd_c, y_na_c)


def _odd_layer_mix(h, hc, in_w, conv_w, conv_b, gate_w, gate_b, lam, hy_params):
    rg_w = lam.shape[1]
    pad = RG_CONV // 2
    u_c = _dwconv(_mm(hc, in_w, col0=rg_w, n=rg_w), conv_w, conv_b, pad)
    _, h_c = _rglru(u_c, None, gate_w, gate_b, lam, jnp.zeros((2, rg_w), F32))
    proj = _mm(h, in_w)
    u = _dwconv(proj, conv_w, conv_b, pad, col0=rg_w)
    y_rg, _ = _rglru(u, proj, gate_w, gate_b, lam, h_c)
    return y_rg, _hyena(proj, 2 * rg_w, *hy_params)


def kernel(x, c, ctx, c_ctx, ada_w, ada_b, norm_mix_w, norm_ffn_w, norm_out_w, ev_in_w, ev_ssd_conv_w, ev_ssd_conv_b, ev_ssd_dt_bias, ev_ssd_a_log, ev_ssd_d, ev_ssd_norm_w, ev_na_rpb, ev_out_w, ev_ffn_w1, ev_ffn_w3, ev_ffn_w2, od_in_w, od_rg_conv_w, od_rg_conv_b, od_rg_gate_w, od_rg_gate_b, od_rg_lambda, od_hy_conv_w, od_hy_conv_b, od_hy_w_in, od_hy_b_in, od_hy_w_mid, od_hy_b_mid, od_hy_w_out, od_hy_freq, od_hy_deltas, od_hy_skip, od_out_w, od_router_w, od_moe_w1, od_moe_w3, od_moe_w2):
    batch, seq, d = x.shape
    assert batch == 1 and ada_w.shape[0] == 2, "kernel is written for one sequence and the even/odd layer pair"
    xs = x[0]
    cs = ctx[0]
    cvecs = jnp.concatenate([jax.nn.silu(c), jax.nn.silu(c_ctx)[None]], axis=0)
    mods = _ada(cvecs, ada_w, ada_b).reshape(2, 2, ADA_CHUNKS, d)

    mod, mod_c = mods[0, 0], mods[0, 1]
    h = _mod_norm(xs, norm_mix_w[0], mod[0], mod[1])
    hc = _mod_norm(cs, norm_mix_w[0], mod_c[0], mod_c[1])
    y, y_c = _even_layer_mix(h, hc, ev_in_w[0], ev_ssd_conv_w[0], ev_ssd_conv_b[0], ev_ssd_dt_bias[0],
                             ev_ssd_a_log[0], ev_ssd_d[0], ev_ssd_norm_w[0], ev_na_rpb[0])
    xs = _mm_residual2(*y, ev_out_w[0], xs, mod[2])
    cs = _mm_residual2(*y_c, ev_out_w[0], cs, mod_c[2])
    w1, w3, w2 = ev_ffn_w1[0], ev_ffn_w3[0], ev_ffn_w2[0].astype(BF16)
    h2 = _mod_norm(xs, norm_ffn_w[0], mod[3], mod[4])
    xs = _mm_residual(_mm_swiglu(h2, w1, w3), w2, xs, mod[5], tm=256, tn=512)
    h2c = _mod_norm(cs, norm_ffn_w[0], mod_c[3], mod_c[4])
    cs = _mm_residual(_mm_swiglu(h2c, w1, w3), w2, cs, mod_c[5], tm=256, tn=512)

    mod, mod_c = mods[1, 0], mods[1, 1]
    h = _mod_norm(xs, norm_mix_w[1], mod[0], mod[1])
    hc = _mod_norm(cs, norm_mix_w[1], mod_c[0], mod_c[1])
    hy_params = (od_hy_conv_w[0], od_hy_conv_b[0], od_hy_w_in[0], od_hy_b_in[0], od_hy_w_mid[0],
                 od_hy_b_mid[0], od_hy_w_out[0], od_hy_freq[0], od_hy_deltas[0], od_hy_skip[0])
    y = _odd_layer_mix(h, hc, od_in_w[0], od_rg_conv_w[0], od_rg_conv_b[0], od_rg_gate_w[0],
                       od_rg_gate_b[0], od_rg_lambda[0], hy_params)
    xs = _mm_residual2(*y, od_out_w[0], xs, mod[2])
    h2, route = _mod_norm_router(xs, norm_ffn_w[1], mod[3], mod[4], od_router_w[0])
    y1, y2 = _moe(h2, route, od_moe_w1[0], od_moe_w3[0], od_moe_w2[0])
    return _final_norm(xs, y1, y2, mod[5], norm_out_w)[None]
```

```python
import functools
import math

import numpy as np
import jax
import jax.numpy as jnp
from jax import lax
from jax.experimental import pallas as pl
from jax.experimental.pallas import tpu as pltpu

GRID_W = 64
ADA_CHUNKS = 6
NORM_EPS = 1e-6
SSD_P = 64
SSD_G = 4
SSD_N = 128
SSD_CONV = 4
SSD_CHUNK = 128
NA_DH = 128
NA_WIN_H = 8
NA_WIN_W = 16
NA_QCOLS = 16
NA_KCOLS = NA_QCOLS + NA_WIN_W
RG_BLOCKS = 16
RG_CONV = 4
RG_C = 8.0
HY_ORDER = 2
HY_CONV = 3
HY_EMB = 33
HY_BANDS = (HY_EMB - 1) // 2
HY_INNER = 2
TOP_K = 2

LANES = 128
SUBLANES = 8
VMEM_LIMIT_BYTES = 56 * 1024 * 1024
MOE_TILE_ROWS = 512

F32 = jnp.float32
BF16 = jnp.bfloat16


def _tile(n, target, align):
    for t in range(min(n, target), 0, -1):
        if n % t == 0 and t % align == 0:
            return t
    return n


def _params(*semantics):
    return pltpu.CompilerParams(dimension_semantics=semantics, vmem_limit_bytes=VMEM_LIMIT_BYTES)


def _ada_kernel(cb_ref, w_ref, b_ref, o_ref, *, n_rows):
    k, tn = w_ref.shape
    for l in range(tn // LANES):
        w = w_ref[:, l * LANES:(l + 1) * LANES].reshape(k // SUBLANES, SUBLANES, LANES)
        for r in range(n_rows):
            cb = cb_ref[r].reshape(k // SUBLANES, SUBLANES, LANES)
            part = jnp.sum(w * cb, axis=0)
            o_ref[r:r + 1, l * LANES:(l + 1) * LANES] = (
                jnp.sum(part, axis=0, keepdims=True) + b_ref[:, l * LANES:(l + 1) * LANES])


def _ada(cvecs, w, b):
    r, k = cvecs.shape
    layers, _, n = w.shape
    tn = _tile(n, 512, LANES)
    cb = jnp.broadcast_to(cvecs[:, :, None], (r, k, LANES))
    return pl.pallas_call(
        functools.partial(_ada_kernel, n_rows=r),
        grid=(layers, n // tn),
        in_specs=[pl.BlockSpec((r, k, LANES), lambda l, j: (0, 0, 0)),
                  pl.BlockSpec((None, k, tn), lambda l, j: (l, 0, j)),
                  pl.BlockSpec((None, 1, tn), lambda l, j: (l, 0, j))],
        out_specs=pl.BlockSpec((None, r, tn), lambda l, j: (l, 0, j)),
        out_shape=jax.ShapeDtypeStruct((layers, r, n), F32),
        compiler_params=_params("parallel", "parallel"),
        name="ada_matvec",
    )(cb, w, b.reshape(layers, 1, n))


def _norm_rows(x, w, shift, scale):
    y = x * lax.rsqrt(jnp.mean(x * x, axis=-1, keepdims=True) + NORM_EPS)
    y = y * w
    if scale is not None:
        y = y * (1.0 + scale) + shift
    return y


def _norm_kernel(x_ref, w_ref, sh_ref, sc_ref, o_ref):
    o_ref[...] = _norm_rows(x_ref[...], w_ref[...], sh_ref[...], sc_ref[...]).astype(o_ref.dtype)


def _final_norm_kernel(x_ref, y1_ref, y2_ref, gate_ref, w_ref, o_ref):
    x = x_ref[...] + gate_ref[...] * (y1_ref[...] + y2_ref[...])
    o_ref[...] = _norm_rows(x, w_ref[...], None, None).astype(o_ref.dtype)


def _norm_router_kernel(x_ref, w_ref, sh_ref, sc_ref, rw_ref, o_ref, route_ref, *, n_experts):
    h = _norm_rows(x_ref[...], w_ref[...], sh_ref[...], sc_ref[...])
    o_ref[...] = h.astype(o_ref.dtype)
    logits = jnp.dot(h, rw_ref[...], preferred_element_type=F32, precision=lax.Precision.HIGHEST)
    col = lax.broadcasted_iota(jnp.int32, logits.shape, 1)
    neg = jnp.float32(-jnp.inf)
    l1 = jnp.where(col < n_experts, logits, neg)
    m1 = jnp.max(l1, axis=-1, keepdims=True)
    i1 = jnp.min(jnp.where(l1 == m1, col, LANES), axis=-1, keepdims=True)
    l2 = jnp.where(col == i1, neg, l1)
    m2 = jnp.max(l2, axis=-1, keepdims=True)
    i2 = jnp.min(jnp.where(l2 == m2, col, LANES), axis=-1, keepdims=True)
    e2 = jnp.exp(m2 - m1)
    g1 = 1.0 / (1.0 + e2)
    g2 = e2 * g1
    route = jnp.where(col == 0, i1.astype(F32),
                      jnp.where(col == 1, i2.astype(F32),
                                jnp.where(col == 2, g1, jnp.where(col == 3, g2, 0.0))))
    route_ref[...] = route


def _mod_norm(x, w, shift, scale, out_dtype=BF16):
    m, d = x.shape
    tr = _tile(m, 256, SUBLANES)
    vec = pl.BlockSpec((1, d), lambda i: (0, 0))
    return pl.pallas_call(
        _norm_kernel,
        grid=(m // tr,),
        in_specs=[pl.BlockSpec((tr, d), lambda i: (i, 0)), vec, vec, vec],
        out_specs=pl.BlockSpec((tr, d), lambda i: (i, 0)),
        out_shape=jax.ShapeDtypeStruct((m, d), out_dtype),
        compiler_params=_params("parallel"),
        name="mod_norm",
    )(x, w.reshape(1, d), shift.reshape(1, d), scale.reshape(1, d))


def _final_norm(x, y1, y2, gate, w):
    m, d = x.shape
    tr = _tile(m, 256, SUBLANES)
    rows = pl.BlockSpec((tr, d), lambda i: (i, 0))
    vec = pl.BlockSpec((1, d), lambda i: (0, 0))
    return pl.pallas_call(
        _final_norm_kernel,
        grid=(m // tr,),
        in_specs=[rows, rows, rows, vec, vec],
        out_specs=rows,
        out_shape=jax.ShapeDtypeStruct((m, d), x.dtype),
        compiler_params=_params("parallel"),
        name="final_norm",
    )(x, y1, y2, gate.reshape(1, d), w.reshape(1, d))


def _mod_norm_router(x, w, shift, scale, router_w):
    m, d = x.shape
    n_experts = router_w.shape[1]
    tr = _tile(m, 256, SUBLANES)
    rw = jnp.zeros((d, LANES), F32).at[:, :n_experts].set(router_w.astype(F32))
    vec = pl.BlockSpec((1, d), lambda i: (0, 0))
    return pl.pallas_call(
        functools.partial(_norm_router_kernel, n_experts=n_experts),
        grid=(m // tr,),
        in_specs=[pl.BlockSpec((tr, d), lambda i: (i, 0)), vec, vec, vec,
                  pl.BlockSpec((d, LANES), lambda i: (0, 0))],
        out_specs=[pl.BlockSpec((tr, d), lambda i: (i, 0)),
                   pl.BlockSpec((tr, LANES), lambda i: (i, 0))],
        out_shape=[jax.ShapeDtypeStruct((m, d), BF16), jax.ShapeDtypeStruct((m, LANES), F32)],
        compiler_params=_params("parallel"),
        name="mod_norm_router",
    )(x, w.reshape(1, d), shift.reshape(1, d), scale.reshape(1, d), rw)


def _bf16(ref):
    return ref[...].astype(BF16)


def _resident_bf16(b_ref, scratch):
    if not scratch:
        return b_ref
    (w_ref,) = scratch

    @pl.when(pl.program_id(1) == 0)
    def _():
        w_ref[...] = b_ref[...].astype(BF16)

    return w_ref


def _resident_scratch(b, k, tn):
    return [] if b.dtype == BF16 else [pltpu.VMEM((k, tn), BF16)]


def _mm_kernel(a_ref, b_ref, o_ref, *scratch):
    w_ref = _resident_bf16(b_ref, scratch)
    o_ref[...] = jnp.dot(a_ref[...], w_ref[...], preferred_element_type=F32).astype(o_ref.dtype)


def _mm_residual_kernel(a_ref, b_ref, res_ref, gate_ref, o_ref):
    acc = jnp.dot(a_ref[...], _bf16(b_ref), preferred_element_type=F32)
    o_ref[...] = res_ref[...] + gate_ref[...] * acc


def _mm_swiglu_kernel(a_ref, b1_ref, b3_ref, o_ref):
    a = a_ref[...]
    g = jnp.dot(a, _bf16(b1_ref), preferred_element_type=F32)
    u = jnp.dot(a, _bf16(b3_ref), preferred_element_type=F32)
    o_ref[...] = (g * jax.nn.sigmoid(g) * u).astype(o_ref.dtype)


def _mm_tiles(m, k, n, tm, tn):
    tm = _tile(m, tm, SUBLANES)
    tn = _tile(n, tn, LANES)
    return tm, tn


def _mm(a, b, *, out_dtype=F32, tm=1024, tn=512, col0=0, n=None):
    m, k = a.shape
    n = b.shape[1] - col0 if n is None else n
    tm, tn = _mm_tiles(m, k, math.gcd(n, col0) if col0 else n, tm, tn)
    off = col0 // tn
    return pl.pallas_call(
        _mm_kernel,
        grid=(n // tn, m // tm),
        in_specs=[pl.BlockSpec((tm, k), lambda j, i: (i, 0)),
                  pl.BlockSpec((k, tn), lambda j, i: (0, j + off))],
        out_specs=pl.BlockSpec((tm, tn), lambda j, i: (i, j)),
        out_shape=jax.ShapeDtypeStruct((m, n), out_dtype),
        scratch_shapes=_resident_scratch(b, k, tn),
        compiler_params=_params("parallel", "arbitrary"),
        name="matmul",
    )(a, b)


def _mm_residual(a, b, res, gate, *, tm=512, tn=1024):
    m, k = a.shape
    n = b.shape[1]
    tm, tn = _mm_tiles(m, k, n, tm, tn)
    return pl.pallas_call(
        _mm_residual_kernel,
        grid=(n // tn, m // tm),
        in_specs=[pl.BlockSpec((tm, k), lambda j, i: (i, 0)),
                  pl.BlockSpec((k, tn), lambda j, i: (0, j)),
                  pl.BlockSpec((tm, tn), lambda j, i: (i, j)),
                  pl.BlockSpec((1, tn), lambda j, i: (0, j))],
        out_specs=pl.BlockSpec((tm, tn), lambda j, i: (i, j)),
        out_shape=jax.ShapeDtypeStruct((m, n), F32),
        compiler_params=_params("parallel", "parallel"),
        name="matmul_residual",
    )(a, b, res, gate.reshape(1, n))


def _mm_residual2_kernel(a1_ref, a2_ref, b_ref, res_ref, gate_ref, o_ref, *scratch):
    k1 = a1_ref.shape[1]
    w_ref = _resident_bf16(b_ref, scratch)
    acc = jnp.dot(a1_ref[...], w_ref[:k1, :], preferred_element_type=F32)
    acc = acc + jnp.dot(a2_ref[...], w_ref[k1:, :], preferred_element_type=F32)
    o_ref[...] = res_ref[...] + gate_ref[...] * acc


def _mm_residual2(a1, a2, b, res, gate, *, tm=1024, tn=512):
    m, k1 = a1.shape
    k2 = a2.shape[1]
    n = b.shape[1]
    tm, tn = _mm_tiles(m, k1 + k2, n, tm, tn)
    return pl.pallas_call(
        _mm_residual2_kernel,
        grid=(n // tn, m // tm),
        in_specs=[pl.BlockSpec((tm, k1), lambda j, i: (i, 0)),
                  pl.BlockSpec((tm, k2), lambda j, i: (i, 0)),
                  pl.BlockSpec((k1 + k2, tn), lambda j, i: (0, j)),
                  pl.BlockSpec((tm, tn), lambda j, i: (i, j)),
                  pl.BlockSpec((1, tn), lambda j, i: (0, j))],
        out_specs=pl.BlockSpec((tm, tn), lambda j, i: (i, j)),
        out_shape=jax.ShapeDtypeStruct((m, n), F32),
        scratch_shapes=_resident_scratch(b, k1 + k2, tn),
        compiler_params=_params("parallel", "arbitrary"),
        name="matmul_residual2",
    )(a1, a2, b, res, gate.reshape(1, n))


def _mm_swiglu(a, b1, b3, *, tm=1024, tn=256):
    m, k = a.shape
    n = b1.shape[1]
    tm, tn = _mm_tiles(m, k, n, tm, tn)
    return pl.pallas_call(
        _mm_swiglu_kernel,
        grid=(m // tm, n // tn),
        in_specs=[pl.BlockSpec((tm, k), lambda i, j: (i, 0)),
                  pl.BlockSpec((k, tn), lambda i, j: (0, j)),
                  pl.BlockSpec((k, tn), lambda i, j: (0, j))],
        out_specs=pl.BlockSpec((tm, tn), lambda i, j: (i, j)),
        out_shape=jax.ShapeDtypeStruct((m, n), BF16),
        compiler_params=_params("parallel", "parallel"),
        name="matmul_swiglu",
    )(a, b1, b3)


def _expert_changed(te_ref):
    i = pl.program_id(1)
    return (i == 0) | (te_ref[i] != te_ref[jnp.maximum(i - 1, 0)])


def _moe_up_kernel(te_ref, nt_ref, a_ref, b1_ref, b3_ref, o_ref, w1_ref, w3_ref):
    @pl.when(pl.program_id(1) < nt_ref[0])
    def _():
        @pl.when(_expert_changed(te_ref))
        def _():
            w1_ref[...] = _bf16(b1_ref)
            w3_ref[...] = _bf16(b3_ref)

        a = a_ref[...]
        g = jnp.dot(a, w1_ref[...], preferred_element_type=F32)
        u = jnp.dot(a, w3_ref[...], preferred_element_type=F32)
        o_ref[...] = (g * jax.nn.sigmoid(g) * u).astype(o_ref.dtype)


def _moe_down_kernel(te_ref, nt_ref, a_ref, b_ref, gate_ref, o_ref, w_ref):
    @pl.when(pl.program_id(1) < nt_ref[0])
    def _():
        @pl.when(_expert_changed(te_ref))
        def _():
            w_ref[...] = _bf16(b_ref)

        acc = jnp.dot(a_ref[...], w_ref[...], preferred_element_type=F32)
        o_ref[...] = gate_ref[...] * acc


def _moe_grouped(xg, gate_slot, tile_expert, n_tiles, w1, w3, w2, *, tm):
    s, d = xg.shape
    f = w1.shape[2]
    n_t = s // tm
    tf = _tile(f, 512, LANES)
    td = _tile(d, 1024, LANES)

    def row(i, nt):
        return jnp.minimum(i, nt[0] - 1)

    up = pl.pallas_call(
        _moe_up_kernel,
        grid_spec=pltpu.PrefetchScalarGridSpec(
            num_scalar_prefetch=2,
            grid=(f // tf, n_t),
            in_specs=[
                pl.BlockSpec((tm, d), lambda j, i, te, nt: (row(i, nt), 0)),
                pl.BlockSpec((None, d, tf), lambda j, i, te, nt: (te[row(i, nt)], 0, j)),
                pl.BlockSpec((None, d, tf), lambda j, i, te, nt: (te[row(i, nt)], 0, j)),
            ],
            out_specs=pl.BlockSpec((tm, tf), lambda j, i, te, nt: (row(i, nt), j)),
            scratch_shapes=[pltpu.VMEM((d, tf), BF16), pltpu.VMEM((d, tf), BF16)],
        ),
        out_shape=jax.ShapeDtypeStruct((s, f), BF16),
        compiler_params=_params("arbitrary", "arbitrary"),
        name="moe_up",
    )(tile_expert, n_tiles, xg, w1, w3)

    return pl.pallas_call(
        _moe_down_kernel,
        grid_spec=pltpu.PrefetchScalarGridSpec(
            num_scalar_prefetch=2,
            grid=(d // td, n_t),
            in_specs=[
                pl.BlockSpec((tm, f), lambda j, i, te, nt: (row(i, nt), 0)),
                pl.BlockSpec((None, f, td), lambda j, i, te, nt: (te[row(i, nt)], 0, j)),
                pl.BlockSpec((tm, 1), lambda j, i, te, nt: (row(i, nt), 0)),
            ],
            out_specs=pl.BlockSpec((tm, td), lambda j, i, te, nt: (row(i, nt), j)),
            scratch_shapes=[pltpu.VMEM((f, td), BF16)],
        ),
        out_shape=jax.ShapeDtypeStruct((s, d), F32),
        compiler_params=_params("arbitrary", "arbitrary"),
        name="moe_down",
    )(tile_expert, n_tiles, up, w2, gate_slot)


def _moe(h2, route, w1, w3, w2):
    n, d = h2.shape
    n_e = w1.shape[0]
    tm = min(MOE_TILE_ROWS, n)
    n_t = -(-(TOP_K * n + n_e * (tm - 1)) // tm)
    s = n_t * tm
    expert = route[:, :TOP_K].astype(jnp.int32).T.reshape(-1)
    gate = route[:, TOP_K:2 * TOP_K].T.reshape(-1)
    token = jnp.tile(jnp.arange(n, dtype=jnp.int32), TOP_K)
    onehot = (expert[:, None] == jnp.arange(n_e, dtype=jnp.int32)[None]).astype(jnp.int32)
    csum = jnp.cumsum(onehot, axis=0)
    rank = jnp.sum(onehot * csum, axis=1) - 1
    counts = csum[-1]
    padded = -(-counts // tm) * tm
    ends = jnp.cumsum(padded)
    starts = ends - padded
    slot = starts[expert] + rank
    token_of_slot = jnp.zeros((s,), jnp.int32).at[slot].set(token)
    gate_of_slot = jnp.zeros((s,), F32).at[slot].set(gate)
    n_tiles = (ends[-1] // tm).astype(jnp.int32).reshape(1)
    tile_start = jnp.arange(n_t, dtype=jnp.int32) * tm
    tile_expert = jnp.minimum(jnp.sum((tile_start[:, None] >= ends[None]).astype(jnp.int32), axis=1), n_e - 1)
    xg = h2.at[token_of_slot].get(mode="promise_in_bounds")
    yo = _moe_grouped(xg, gate_of_slot.reshape(s, 1), tile_expert, n_tiles, w1, w3, w2, tm=tm)
    slots = slot.reshape(TOP_K, n)
    return [yo.at[slots[kk]].get(mode="promise_in_bounds") for kk in range(TOP_K)]


def _dwconv_kernel(prev_ref, cur_ref, next_ref, w_ref, b_ref, o_ref, *, pad_left, silu):
    i = pl.program_id(1)
    cur = cur_ref[...]
    rows = cur.shape[0]
    prev = jnp.where(i > 0, prev_ref[...], 0.0)
    nxt = jnp.where(i < pl.num_programs(1) - 1, next_ref[...], 0.0)
    ext = jnp.concatenate([prev, cur, nxt], axis=0)
    acc = jnp.zeros_like(cur) + b_ref[...]
    for j in range(w_ref.shape[0]):
        s = j - pad_left
        us = cur if s == 0 else pltpu.roll(ext, (-s) % ext.shape[0], axis=0)[SUBLANES:SUBLANES + rows]
        acc = acc + w_ref[j:j + 1, :] * us
    if silu:
        acc = acc * jax.nn.sigmoid(acc)
    o_ref[...] = acc


def _dwconv(u, w, b, pad_left, *, col0=0, silu=False):
    n = u.shape[0]
    k, c = w.shape
    assert k - 1 <= SUBLANES
    tc = _tile(math.gcd(c, col0) if col0 else c, 512, LANES)
    off = col0 // tc
    tr = _tile(n, 1024, SUBLANES)
    per = tr // SUBLANES
    last = n // SUBLANES - 1
    return pl.pallas_call(
        functools.partial(_dwconv_kernel, pad_left=pad_left, silu=silu),
        grid=(c // tc, n // tr),
        in_specs=[pl.BlockSpec((SUBLANES, tc), lambda j, i: (jnp.maximum(i * per - 1, 0), j + off)),
                  pl.BlockSpec((tr, tc), lambda j, i: (i, j + off)),
                  pl.BlockSpec((SUBLANES, tc), lambda j, i: (jnp.minimum((i + 1) * per, last), j + off)),
                  pl.BlockSpec((k, tc), lambda j, i: (0, j)),
                  pl.BlockSpec((1, tc), lambda j, i: (0, j))],
        out_specs=pl.BlockSpec((tr, tc), lambda j, i: (i, j)),
        out_shape=jax.ShapeDtypeStruct((n, c), F32),
        compiler_params=_params("parallel", "parallel"),
        name="dwconv",
    )(u, u, u, w.astype(F32), b.astype(F32).reshape(1, c))


def _ssd_kernel(xf_ref, bf_ref, cf_ref, dtf_ref, xb_ref, bb_ref, cb_ref, dtb_ref, bias_ref, a_ref, s0_ref,
                yf_ref, yb_ref, st_ref, state, *, heads_per_group, head_dim):
    step = pl.program_id(1)

    @pl.when(step == 0)
    def _():
        state[...] = s0_ref[...]

    q, gw = xf_ref.shape
    hp = lax.Precision.HIGHEST
    nt = (((1,), (1,)), ((), ()))
    row = lax.broadcasted_iota(jnp.int32, (q, q), 0)
    col = lax.broadcasted_iota(jnp.int32, (q, q), 1)
    tri = (row >= col).astype(F32)
    neg = jnp.float32(-jnp.inf)
    head_of_col = lax.broadcasted_iota(jnp.int32, (LANES, gw), 1) // head_dim
    lane_id = lax.broadcasted_iota(jnp.int32, (LANES, gw), 0)
    low_half = lax.broadcasted_iota(jnp.int32, (q, LANES), 1) < head_dim
    heads_per_tile = LANES // head_dim
    streams = ((xf_ref, bf_ref, cf_ref, dtf_ref, yf_ref), (xb_ref, bb_ref, cb_ref, dtb_ref, yb_ref))
    for d, (x_ref, b_ref, c_ref, dt_ref, y_ref) in enumerate(streams):
        xs = x_ref[...]
        bm = b_ref[...].astype(BF16)
        cm = c_ref[...].astype(BF16)
        dt = jax.nn.softplus(dt_ref[...] + bias_ref[...])
        da = dt * a_ref[...]
        cum = jnp.dot(tri, da, preferred_element_type=F32, precision=hp)
        total = cum[q - 1:q, :]
        pos = cum if d == 0 else cum - da
        pos_t = pos.T
        if d == 0:
            carry_in, carry_out = jnp.exp(pos), jnp.exp(total - pos)
        else:
            carry_in, carry_out = jnp.exp(total - pos), jnp.exp(pos)
        spread = (lane_id == head_of_col + d * heads_per_group).astype(F32)
        xd = xs * jnp.dot(dt, spread, preferred_element_type=F32, precision=hp)
        st = state[d]
        y_off = (lax.dot_general(cm, st.astype(BF16), nt, preferred_element_type=F32)
                 * jnp.dot(carry_in, spread, preferred_element_type=F32, precision=hp))
        xd_out = (xd * jnp.dot(carry_out, spread, preferred_element_type=F32, precision=hp)).astype(BF16)
        upd = lax.dot_general(xd_out, bm, (((0,), (0,)), ((), ())), preferred_element_type=F32)
        keep = lax.dot_general(spread, jnp.exp(total), (((0,), (1,)), ((), ())),
                               preferred_element_type=F32, precision=hp)
        state[d] = keep * st + upd
        cb = lax.dot_general(cm, bm, nt, preferred_element_type=F32)
        mask = (row >= col) if d == 0 else (col >= row)
        xdb = xd.astype(BF16)
        for tile in range(gw // LANES):
            cols = slice(tile * LANES, (tile + 1) * LANES)
            parts = []
            for e in range(tile * heads_per_tile, (tile + 1) * heads_per_tile):
                lane = d * heads_per_group + e
                p_col = pos[:, lane:lane + 1]
                p_row = pos_t[lane:lane + 1, :]
                expo = (p_col - p_row) if d == 0 else (p_row - p_col)
                dec = jnp.exp(jnp.where(mask, expo, neg))
                parts.append(jnp.dot((cb * dec).astype(BF16), xdb[:, cols], preferred_element_type=F32))
            y_diag = parts[0] if len(parts) == 1 else jnp.where(low_half, parts[0], parts[1])
            y_ref[:, cols] = y_diag + y_off[:, cols]

    @pl.when(step == pl.num_programs(1) - 1)
    def _():
        st_ref[...] = state[...]


def _ssd_scan(xbc, dtg, bias_g, a_g, s0, *, ssd_w, groups):
    n = xbc.shape[0]
    e = ssd_w // (SSD_P * groups)
    q = min(SSD_CHUNK, n)
    steps = n // q
    gw = e * SSD_P
    nb = ssd_w // SSD_N
    assert LANES % SSD_P == 0 and LANES // SSD_P <= 2 and gw % LANES == 0 and 2 * e <= LANES

    def fwd(blk):
        return lambda g, s: (s, blk(g))

    def bwd(blk):
        return lambda g, s: (steps - 1 - s, blk(g))

    def stream(order):
        return [pl.BlockSpec((q, gw), order(lambda g: g)),
                pl.BlockSpec((q, SSD_N), order(lambda g: nb + g)),
                pl.BlockSpec((q, SSD_N), order(lambda g: nb + groups + g))]

    grp = pl.BlockSpec((None, 1, LANES), lambda g, s: (g, 0, 0))
    st_spec = pl.BlockSpec((2, None, gw, SSD_N), lambda g, s: (0, g, 0, 0))
    return pl.pallas_call(
        functools.partial(_ssd_kernel, heads_per_group=e, head_dim=SSD_P),
        grid=(groups, steps),
        in_specs=(stream(fwd) + [pl.BlockSpec((q, LANES), fwd(lambda g: g))]
                  + stream(bwd) + [pl.BlockSpec((q, LANES), bwd(lambda g: g))]
                  + [grp, grp, st_spec]),
        out_specs=[pl.BlockSpec((q, gw), fwd(lambda g: g)), pl.BlockSpec((q, gw), bwd(lambda g: g)), st_spec],
        out_shape=[jax.ShapeDtypeStruct((n, ssd_w), F32), jax.ShapeDtypeStruct((n, ssd_w), F32),
                   jax.ShapeDtypeStruct(s0.shape, F32)],
        scratch_shapes=[pltpu.VMEM((2, gw, SSD_N), F32)],
        compiler_params=_params("parallel", "arbitrary"),
        name="ssd_scan",
    )(xbc, xbc, xbc, dtg, xbc, xbc, xbc, dtg, bias_g, a_g, s0)


def _ssd_gate_norm_kernel(yf_ref, yb_ref, xs_ref, z_ref, d_ref, nw_ref, o_ref, *, groups):
    y = yf_ref[...] + yb_ref[...] + xs_ref[...] * d_ref[...]
    z = z_ref[...]
    y = y * (z * jax.nn.sigmoid(z))
    gw = y.shape[1] // groups
    for g in range(groups):
        yg = y[:, g * gw:(g + 1) * gw]
        yg = yg * lax.rsqrt(jnp.mean(yg * yg, axis=-1, keepdims=True) + NORM_EPS)
        o_ref[:, g * gw:(g + 1) * gw] = (yg * nw_ref[:, g * gw:(g + 1) * gw]).astype(o_ref.dtype)


def _ssd_gate_norm(y_f, y_b, xbc, zx, d_cols, norm_w, *, groups):
    n, w = y_f.shape
    tr = _tile(n, 256, SUBLANES)
    rows = pl.BlockSpec((tr, w), lambda i: (i, 0))
    vec = pl.BlockSpec((1, w), lambda i: (0, 0))
    return pl.pallas_call(
        functools.partial(_ssd_gate_norm_kernel, groups=groups),
        grid=(n // tr,),
        in_specs=[rows, rows, rows, rows, vec, vec],
        out_specs=rows,
        out_shape=jax.ShapeDtypeStruct((n, w), BF16),
        compiler_params=_params("parallel"),
        name="ssd_gate_norm",
    )(y_f, y_b, xbc, zx, d_cols.reshape(1, w), norm_w.astype(F32).reshape(1, w))


def _na_kernel(q_ref, k_ref, v_ref, kc_ref, vc_ref, bias_ref, o_ref, *, grid_rows, win_h, rows_per_block, scale):
    blk = pl.program_id(1)
    kc = kc_ref[...]
    vc = vc_ref[...]
    nt = (((1,), (1,)), ((), ()))

    def one_row(i, carry):
        r = blk * rows_per_block + i
        rs = jnp.clip(r - win_h // 2, 0, grid_rows - win_h)
        q = q_ref[pl.ds(pl.multiple_of(i * GRID_W, GRID_W), GRID_W), :]
        start = pl.multiple_of(rs * GRID_W, GRID_W)
        kw = k_ref[pl.ds(start, win_h * GRID_W), :]
        vw = v_ref[pl.ds(start, win_h * GRID_W), :]
        s = lax.dot_general(q, kw, nt, preferred_element_type=F32) * scale + bias_ref[r - rs]
        sc = lax.dot_general(q, kc, nt, preferred_element_type=F32) * scale
        m = jnp.maximum(jnp.max(s, axis=-1, keepdims=True), jnp.max(sc, axis=-1, keepdims=True))
        p = jnp.exp(s - m)
        pc = jnp.exp(sc - m)
        denom = jnp.sum(p, axis=-1, keepdims=True) + jnp.sum(pc, axis=-1, keepdims=True)
        o = (jnp.dot(p.astype(BF16), vw, preferred_element_type=F32)
             + jnp.dot(pc.astype(BF16), vc, preferred_element_type=F32))
        o_ref[pl.ds(pl.multiple_of(i * GRID_W, GRID_W), GRID_W), :] = (o / denom).astype(o_ref.dtype)
        return carry

    lax.fori_loop(0, rows_per_block, one_row, 0, unroll=True)


def _na_bias_table(rpb, win_h):
    n_dcol = 2 * NA_WIN_W - 1
    p = np.arange(win_h)[:, None]
    a = np.arange(win_h)[None, :]
    rows = rpb.astype(F32)[:, a - p + NA_WIN_H - 1, :]
    qc = np.arange(GRID_W)[:, None]
    kc = np.arange(GRID_W)[None, :]
    dcol = np.clip(kc - qc + NA_WIN_W - 1, 0, n_dcol - 1)
    onehot = (np.arange(n_dcol)[:, None, None] == dcol[None]).astype(np.float32)
    bias = jnp.einsum('hpad,dqk->hpqak', rows, jnp.asarray(onehot), precision=lax.Precision.HIGHEST)
    cstart = np.clip(qc - NA_WIN_W // 2, 0, GRID_W - NA_WIN_W)
    ok = (kc >= cstart) & (kc < cstart + NA_WIN_W)
    bias = jnp.where(ok[None, None, :, None, :], bias, -jnp.inf)
    return bias.reshape(rpb.shape[0], win_h, GRID_W, win_h * GRID_W)


def _neighbourhood_attention(qkv, qkv_c, rpb, *, heads):
    n = qkv.shape[0]
    c = qkv_c.shape[0]
    grid_rows = n // GRID_W
    win_h = min(NA_WIN_H, grid_rows)
    rows_per_block = _tile(grid_rows, 8, 1)
    tq = rows_per_block * GRID_W
    bias = _na_bias_table(rpb, win_h)
    return pl.pallas_call(
        functools.partial(_na_kernel, grid_rows=grid_rows, win_h=win_h, rows_per_block=rows_per_block,
                          scale=NA_DH ** -0.5),
        grid=(heads, n // tq),
        in_specs=[pl.BlockSpec((tq, NA_DH), lambda h, i: (i, h)),
                  pl.BlockSpec((n, NA_DH), lambda h, i: (0, heads + h)),
                  pl.BlockSpec((n, NA_DH), lambda h, i: (0, 2 * heads + h)),
                  pl.BlockSpec((c, NA_DH), lambda h, i: (0, heads + h)),
                  pl.BlockSpec((c, NA_DH), lambda h, i: (0, 2 * heads + h)),
                  pl.BlockSpec((None, win_h, GRID_W, win_h * GRID_W), lambda h, i: (h, 0, 0, 0))],
        out_specs=pl.BlockSpec((tq, NA_DH), lambda h, i: (i, h)),
        out_shape=jax.ShapeDtypeStruct((n, heads * NA_DH), BF16),
        compiler_params=_params("parallel", "arbitrary"),
        name="neighbourhood_attention",
    )(qkv, qkv, qkv, qkv_c, qkv_c, bias)


def _ctx_attn_kernel(q_ref, k_ref, v_ref, o_ref, *, scale):
    s = lax.dot_general(q_ref[...], k_ref[...], (((1,), (1,)), ((), ())), preferred_element_type=F32) * scale
    p = jnp.exp(s - jnp.max(s, axis=-1, keepdims=True))
    o = jnp.dot(p.astype(BF16), v_ref[...], preferred_element_type=F32)
    o_ref[...] = (o / jnp.sum(p, axis=-1, keepdims=True)).astype(o_ref.dtype)


def _context_attention(qkv_c, *, heads):
    c = qkv_c.shape[0]
    return pl.pallas_call(
        functools.partial(_ctx_attn_kernel, scale=NA_DH ** -0.5),
        grid=(heads,),
        in_specs=[pl.BlockSpec((c, NA_DH), lambda h: (0, h)),
                  pl.BlockSpec((c, NA_DH), lambda h: (0, heads + h)),
                  pl.BlockSpec((c, NA_DH), lambda h: (0, 2 * heads + h))],
        out_specs=pl.BlockSpec((c, NA_DH), lambda h: (0, h)),
        out_shape=jax.ShapeDtypeStruct((c, heads * NA_DH), BF16),
        compiler_params=_params("parallel"),
        name="context_attention",
    )(qkv_c, qkv_c, qkv_c)


def _rglru_kernel(*refs, chunk, emit_y):
    if emit_y:
        u_ref, wg_ref, bg_ref, lam_ref, h0_ref, gate_ref, y_ref, ht_ref, hsum = refs
    else:
        u_ref, wg_ref, bg_ref, lam_ref, h0_ref, ht_ref = refs
    n, w = u_ref.shape
    n_chunks = n // chunk
    tiles = chunk // SUBLANES
    sub = lax.broadcasted_iota(jnp.int32, (chunk, w), 0) % SUBLANES
    log_sig = jax.nn.log_sigmoid(lam_ref[...])

    def chunk_scan(d, c0, carry):
        u = u_ref[pl.ds(c0, chunk), :]
        ub = u.astype(BF16)
        r = jax.nn.sigmoid(jnp.dot(ub, wg_ref[d, 0], preferred_element_type=F32) + bg_ref[d, 0])
        i = jax.nn.sigmoid(jnp.dot(ub, wg_ref[d, 1], preferred_element_type=F32) + bg_ref[d, 1])
        log_a = RG_C * r * log_sig[d]
        a = jnp.exp(log_a)
        b = jnp.sqrt(jnp.maximum(1.0 - a * a, 0.0)) * i * u
        for sh in (1, 2, 4):
            if d == 0:
                a_s, b_s, edge = pltpu.roll(a, sh, axis=0), pltpu.roll(b, sh, axis=0), sub < sh
            else:
                a_s, b_s = pltpu.roll(a, chunk - sh, axis=0), pltpu.roll(b, chunk - sh, axis=0)
                edge = sub >= SUBLANES - sh
            b = jnp.where(edge, b, a * b_s + b)
            a = jnp.where(edge, a, a * a_s)
        hs = [None] * tiles
        order = range(tiles) if d == 0 else range(tiles - 1, -1, -1)
        for k in order:
            sl = slice(k * SUBLANES, (k + 1) * SUBLANES)
            h = a[sl] * carry + b[sl]
            hs[k] = h
            carry = h[SUBLANES - 1:SUBLANES] if d == 0 else h[0:1]
        return jnp.concatenate(hs, axis=0), carry

    def step(j, carries, second_pass):
        cf, cb = carries
        rows_f = pl.ds(pl.multiple_of(j * chunk, chunk), chunk)
        rows_b = pl.ds(pl.multiple_of((n_chunks - 1 - j) * chunk, chunk), chunk)
        hf, cf = chunk_scan(0, pl.multiple_of(j * chunk, chunk), cf)
        hb, cb = chunk_scan(1, pl.multiple_of((n_chunks - 1 - j) * chunk, chunk), cb)
        if emit_y:
            if second_pass:
                y_ref[rows_f, :] = (jax.nn.gelu(gate_ref[rows_f, :]) * (hsum[rows_f, :] + hf)).astype(y_ref.dtype)
                y_ref[rows_b, :] = (jax.nn.gelu(gate_ref[rows_b, :]) * (hsum[rows_b, :] + hb)).astype(y_ref.dtype)
            else:
                hsum[rows_f, :] = hf
                hsum[rows_b, :] = hb
        return cf, cb

    carries = (h0_ref[0], h0_ref[1])
    half = n_chunks // 2
    carries = lax.fori_loop(0, half, lambda j, c: step(j, c, False), carries)
    carries = lax.fori_loop(half, n_chunks, lambda j, c: step(j, c, True), carries)
    ht_ref[0] = carries[0]
    ht_ref[1] = carries[1]


def _rglru(u, gate_src, gate_w, gate_b, lam, h0):
    n, w = u.shape
    nb, bw = gate_w.shape[2], gate_w.shape[3]
    chunk = _tile(n // 2, 256, SUBLANES)
    assert (n // chunk) % 2 == 0, "the two sweeps hand over at the middle chunk boundary"
    emit_y = gate_src is not None
    col = lambda j: (0, j)
    in_specs = [pl.BlockSpec((n, bw), col),
                pl.BlockSpec((2, 2, None, bw, bw), lambda j: (0, 0, j, 0, 0)),
                pl.BlockSpec((2, 2, None, 1, bw), lambda j: (0, 0, j, 0, 0)),
                pl.BlockSpec((2, 1, bw), lambda j: (0, 0, j)),
                pl.BlockSpec((2, 1, bw), lambda j: (0, 0, j))]
    args = [u, gate_w.astype(BF16), gate_b.astype(F32).reshape(2, 2, nb, 1, bw), lam.astype(F32).reshape(2, 1, w),
            h0.reshape(2, 1, w)]
    ht_spec = pl.BlockSpec((2, 1, bw), lambda j: (0, 0, j))
    ht_shape = jax.ShapeDtypeStruct((2, 1, w), F32)
    if emit_y:
        in_specs.append(pl.BlockSpec((n, bw), col))
        args.append(gate_src)
        out_specs = [pl.BlockSpec((n, bw), col), ht_spec]
        out_shape = [jax.ShapeDtypeStruct((n, w), BF16), ht_shape]
        scratch = [pltpu.VMEM((n, bw), F32)]
    else:
        out_specs, out_shape, scratch = ht_spec, ht_shape, []
    out = pl.pallas_call(
        functools.partial(_rglru_kernel, chunk=chunk, emit_y=emit_y),
        grid=(nb,),
        in_specs=in_specs,
        out_specs=out_specs,
        out_shape=out_shape,
        scratch_shapes=scratch,
        compiler_params=_params("parallel"),
        name="rglru",
    )(*args)
    if emit_y:
        return out[0], out[1].reshape(2, w)
    return None, out.reshape(2, w)


def _dft_tables(n):
    n1 = 1 << (int(math.log2(n)) // 2)
    n2 = n // n1
    assert n1 * n2 == n and n2 % 2 == 0
    i1, i2 = np.arange(n1), np.arange(n2)
    f1 = np.exp(-2j * np.pi * np.outer(i1, i1) / n1)
    f2 = np.exp(-2j * np.pi * np.outer(i2, i2) / n2)
    tw = np.exp(-2j * np.pi * np.outer(i2, i1) / n)
    return n1, n2, f1, f2, tw


def _stack_ri(m, sign=1.0):
    return jnp.asarray(np.concatenate([m.real, sign * m.imag], axis=0), BF16)


def _hy_hidden_kernel(feat_ref, w_in_ref, b_in_ref, w_mid_ref, b_mid_ref, freq_ref, o_ref):
    hp = lax.Precision.HIGHEST
    freq = freq_ref[...]
    hid = jnp.sin(freq * (jnp.dot(feat_ref[...], w_in_ref[...], preferred_element_type=F32, precision=hp)
                          + b_in_ref[...]))
    for m in range(w_mid_ref.shape[0]):
        hid = jnp.sin(freq * (jnp.dot(hid, w_mid_ref[m], preferred_element_type=F32, precision=hp)
                              + b_mid_ref[m]))
    o_ref[...] = hid


def _hy_hidden(L, w_in, b_in, w_mid, b_mid, freq):
    t = jnp.linspace(0.0, 1.0, L, dtype=F32)[:, None]
    ang = ((2.0 * math.pi / L) * jnp.arange(L, dtype=F32)[:, None]
           * jnp.linspace(1e-4, HY_BANDS - 1, HY_BANDS, dtype=F32)[None])
    emb, ffn = w_in.shape
    assert ffn <= LANES
    feats = jnp.zeros((L, LANES), F32).at[:, :emb].set(jnp.concatenate([t, jnp.cos(ang), -jnp.sin(ang)], axis=-1))

    def padded(v, shape):
        return jnp.zeros(shape, F32).at[tuple(slice(0, n) for n in v.shape)].set(v.astype(F32))

    tr = _tile(L, 512, SUBLANES)
    full = lambda *shape: pl.BlockSpec(shape, lambda *_: (0,) * len(shape))
    return pl.pallas_call(
        _hy_hidden_kernel,
        grid=(L // tr,),
        in_specs=[pl.BlockSpec((tr, LANES), lambda i: (i, 0)), full(LANES, LANES), full(1, LANES),
                  full(HY_INNER, LANES, LANES), full(HY_INNER, 1, LANES), full(1, LANES)],
        out_specs=pl.BlockSpec((tr, LANES), lambda i: (i, 0)),
        out_shape=jax.ShapeDtypeStruct((L, LANES), F32),
        compiler_params=_params("parallel"),
        name="hyena_filter_hidden",
    )(feats, padded(w_in, (LANES, LANES)), padded(b_in[None], (1, LANES)), padded(w_mid, (HY_INNER, LANES, LANES)),
      padded(b_mid[:, None], (HY_INNER, 1, LANES)), padded(freq[None], (1, LANES)))


def _hy_filter_stage1_kernel(hf_ref, hb_ref, wf_ref, wb_ref, df_ref, db_ref, f2f_ref, f2b_ref, twr_ref, twi_ref,
                             o_ref, *, n1, length):
    i = pl.program_id(0)
    half = hf_ref.shape[0]
    n2 = o_ref.shape[1]
    row = lax.broadcasted_iota(jnp.int32, (half, 1), 0)
    scale = 1.0 / (length - 1)

    def taps(h_ref, w_ref, d_ref, first):
        t = (first + n1 * row).astype(F32) * scale
        y = jnp.dot(h_ref[...].astype(BF16), w_ref[...], preferred_element_type=F32)
        return (y * jnp.exp(-t * jnp.abs(d_ref[...]))).astype(BF16)

    f2b = jnp.where(i == 0, f2b_ref[0], f2b_ref[1])
    p = (jnp.dot(f2f_ref[...], taps(hf_ref, wf_ref, df_ref, i), preferred_element_type=F32)
         + jnp.dot(f2b, taps(hb_ref, wb_ref, db_ref, (n1 - i) % n1), preferred_element_type=F32))
    ar, ai = p[:n2], p[n2:]
    tr, ti = twr_ref[...], twi_ref[...]
    o_ref[0] = (ar * tr - ai * ti).astype(o_ref.dtype)
    o_ref[1] = (ar * ti + ai * tr).astype(o_ref.dtype)


def _hy_filter_stage1(hid, w_out, deltas, c, tabs):
    n1, n2, _, f2, tw = tabs
    half = n2 // 2
    L = hid.shape[0]
    cols = HY_ORDER * c

    def split(v):
        v4 = v.reshape(v.shape[0], HY_ORDER, 2, c)
        return v4[:, :, 0].reshape(v.shape[0], cols), v4[:, :, 1].reshape(v.shape[0], cols)

    w_f, w_b = split(jnp.zeros((LANES, w_out.shape[1]), BF16).at[:w_out.shape[0]].set(w_out.astype(BF16)))
    d_f, d_b = split(deltas.astype(F32).reshape(1, -1))
    j = np.arange(half)
    f2b = jnp.stack([_stack_ri(f2[:, (n2 - j) % n2]), _stack_ri(f2[:, n2 - 1 - j])])
    hid2d = hid.reshape(half, n1 * LANES)
    twspec = pl.BlockSpec((None, n2, 1), lambda i: (i, 0, 0))
    full = lambda *shape: pl.BlockSpec(shape, lambda i: (0,) * len(shape))
    return pl.pallas_call(
        functools.partial(_hy_filter_stage1_kernel, n1=n1, length=L),
        grid=(n1,),
        in_specs=[pl.BlockSpec((half, LANES), lambda i: (0, i)),
                  pl.BlockSpec((half, LANES), lambda i: (0, (n1 - i) % n1)),
                  full(LANES, cols), full(LANES, cols), full(1, cols), full(1, cols),
                  full(2 * n2, half), full(2, 2 * n2, half), twspec, twspec],
        out_specs=pl.BlockSpec((2, n2, cols), lambda i: (0, 0, i)),
        out_shape=jax.ShapeDtypeStruct((2, n2, n1 * cols), BF16),
        compiler_params=_params("parallel"),
        name="hyena_filter_stage1",
    )(hid2d, hid2d, w_f, w_b, d_f, d_b, _stack_ri(f2[:, :half]), f2b,
      jnp.asarray(tw.real.T.reshape(n1, n2, 1), F32), jnp.asarray(tw.imag.T.reshape(n1, n2, 1), F32)
      ).reshape(2, n2, n1, cols)


def _hy_filter_spectrum_kernel(a_ref, f1_ref, h_ref):
    n1 = h_ref.shape[1]
    f1 = f1_ref[...]
    p = jnp.dot(f1, a_ref[0], preferred_element_type=F32)
    q = jnp.dot(f1, a_ref[1], preferred_element_type=F32)
    h_ref[0] = p[:n1] - q[n1:]
    h_ref[1] = p[n1:] + q[:n1]


def _hy_filter_spectrum(fa, c, tabs):
    n1, n2, f1, _, _ = tabs
    return pl.pallas_call(
        _hy_filter_spectrum_kernel,
        grid=(HY_ORDER, n2),
        in_specs=[pl.BlockSpec((2, None, n1, c), lambda o, k: (0, k, 0, o)),
                  pl.BlockSpec((2 * n1, n1), lambda o, k: (0, 0))],
        out_specs=pl.BlockSpec((None, 2, None, n1, c), lambda o, k: (o, 0, k, 0, 0)),
        out_shape=jax.ShapeDtypeStruct((HY_ORDER, 2, n2, n1, c), F32),
        compiler_params=_params("parallel", "parallel"),
        name="hyena_filter_spectrum",
    )(fa, _stack_ri(f1))


def _hy_stage1_kernel(x_ref, f2_ref, twr_ref, twi_ref, o_ref):
    n2 = o_ref.shape[1]
    for j in range(x_ref.shape[1]):
        p = jnp.dot(f2_ref[...], x_ref[:, j, :].astype(BF16), preferred_element_type=F32)
        ar, ai = p[:n2], p[n2:]
        tr, ti = twr_ref[j], twi_ref[j]
        o_ref[0, :, j, :] = ar * tr - ai * ti
        o_ref[1, :, j, :] = ar * ti + ai * tr


def _hy_stage1(src, part, c, tabs):
    n1, n2, _, f2, tw = tabs
    half = n2 // 2
    r = SUBLANES
    cb = _tile(c, 1024, LANES)
    per = c // cb
    twspec = pl.BlockSpec((r, n2, 1), lambda i, k: (i, 0, 0))
    return pl.pallas_call(
        _hy_stage1_kernel,
        grid=(n1 // r, per),
        in_specs=[pl.BlockSpec((half, r, cb), lambda i, k: (0, i, part * per + k)),
                  pl.BlockSpec((2 * n2, half), lambda i, k: (0, 0)), twspec, twspec],
        out_specs=pl.BlockSpec((2, n2, r, cb), lambda i, k: (0, 0, i, k)),
        out_shape=jax.ShapeDtypeStruct((2, n2, n1, c), F32),
        compiler_params=_params("parallel", "parallel"),
        name="hyena_dft_stage1",
    )(src.reshape(half, n1, src.shape[1]), _stack_ri(f2[:, :half]),
      jnp.asarray(tw.real.T.reshape(n1, n2, 1), F32), jnp.asarray(tw.imag.T.reshape(n1, n2, 1), F32))


def _hy_mid_kernel(a_ref, h_ref, f1_ref, twr_ref, twi_ref, o_ref):
    n1 = o_ref.shape[1]
    f1 = f1_ref[...]
    p = jnp.dot(f1, a_ref[0].astype(BF16), preferred_element_type=F32)
    q = jnp.dot(f1, a_ref[1].astype(BF16), preferred_element_type=F32)
    xr = p[:n1] - q[n1:]
    xi = p[n1:] + q[:n1]
    hr, hi = h_ref[0], h_ref[1]
    yr = (xr * hr - xi * hi).astype(BF16)
    yi = (xr * hi + xi * hr).astype(BF16)
    p = jnp.dot(f1, yr, preferred_element_type=F32)
    q = jnp.dot(f1, yi, preferred_element_type=F32)
    br = p[:n1] + q[n1:]
    bi = q[:n1] - p[n1:]
    tr, ti = twr_ref[...], twi_ref[...]
    o_ref[0] = br * tr + bi * ti
    o_ref[1] = bi * tr - br * ti


def _hy_stage3_kernel(b_ref, f2_ref, z_ref, gate_ref, skip_ref, o_ref, *, inv_n):
    _, n2, r, c = b_ref.shape
    for j in range(r):
        b = b_ref[:, :, j, :].reshape(2 * n2, c).astype(BF16)
        acc = jnp.dot(f2_ref[...], b, preferred_element_type=F32)
        o_ref[:, j, :] = (gate_ref[:, j, :] * (acc * inv_n + z_ref[:, j, :] * skip_ref[...])).astype(o_ref.dtype)


def _hy_long_conv(z_src, z_part, gate_src, gate_part, spec, order, skip, tabs, out_dtype):
    n1, n2, f1, f2, tw = tabs
    L = z_src.shape[0]
    c = skip.shape[0]
    half = n2 // 2
    r = SUBLANES
    cb = _tile(c, 1024, LANES)
    per = c // cb
    slab = pl.BlockSpec((2, None, n1, c), lambda k: (0, k, 0, 0))
    a = _hy_stage1(z_src, z_part, c, tabs)
    twspec = pl.BlockSpec((None, n1, 1), lambda k: (k, 0, 0))
    b = pl.pallas_call(
        _hy_mid_kernel,
        grid=(n2,),
        in_specs=[slab, pl.BlockSpec((None, 2, None, n1, c), lambda k: (order, 0, k, 0, 0)),
                  pl.BlockSpec((2 * n1, n1), lambda k: (0, 0)), twspec, twspec],
        out_specs=slab,
        out_shape=jax.ShapeDtypeStruct((2, n2, n1, c), F32),
        compiler_params=_params("parallel"),
        name="hyena_spectral_product",
    )(a, spec, _stack_ri(f1), jnp.asarray(tw.real.reshape(n2, n1, 1), F32),
      jnp.asarray(tw.imag.reshape(n2, n1, 1), F32))
    f2c = jnp.asarray(np.concatenate([f2.real[:half], f2.imag[:half]], axis=1), BF16)
    out = pl.pallas_call(
        functools.partial(_hy_stage3_kernel, inv_n=1.0 / (n1 * n2)),
        grid=(n1 // r, per),
        in_specs=[pl.BlockSpec((2, n2, r, cb), lambda i, k: (0, 0, i, k)),
                  pl.BlockSpec((half, 2 * n2), lambda i, k: (0, 0)),
                  pl.BlockSpec((half, r, cb), lambda i, k: (0, i, z_part * per + k)),
                  pl.BlockSpec((half, r, cb), lambda i, k: (0, i, gate_part * per + k)),
                  pl.BlockSpec((1, cb), lambda i, k: (0, k))],
        out_specs=pl.BlockSpec((half, r, cb), lambda i, k: (0, i, k)),
        out_shape=jax.ShapeDtypeStruct((half, n1, c), out_dtype),
        compiler_params=_params("parallel", "parallel"),
        name="hyena_dft_stage3",
    )(b, f2c, z_src.reshape(half, n1, z_src.shape[1]), gate_src.reshape(half, n1, gate_src.shape[1]),
      skip.astype(F32).reshape(1, c))
    return out.reshape(L, c)


def _hyena(p, col0, conv_w, conv_b, w_in, b_in, w_mid, b_mid, w_out, freq, deltas, skip):
    assert HY_ORDER == 2
    L = p.shape[0]
    c = skip.shape[1]
    tabs = _dft_tables(2 * L)
    u = _dwconv(p, conv_w, conv_b, (HY_CONV - 1) // 2, col0=col0)
    hid = _hy_hidden(L, w_in, b_in, w_mid, b_mid, freq)
    spec = _hy_filter_spectrum(_hy_filter_stage1(hid, w_out, deltas, c, tabs), c, tabs)
    z = _hy_long_conv(u, 2, u, 0, spec, 0, skip[0], tabs, F32)
    return _hy_long_conv(z, 0, u, 1, spec, 1, skip[1], tabs, BF16)


def _even_layer_mix(h, hc, in_w, conv_w, conv_b, dt_bias, a_log, d_skip, norm_w, rpb):
    d = h.shape[1]
    ssd_w = norm_w.shape[0]
    heads = ssd_w // SSD_P
    e = heads // SSD_G
    na_heads = (d - ssd_w) // NA_DH
    o1 = ssd_w
    o2 = o1 + conv_w.shape[1]
    o3 = o2 + 2 * heads
    w_qkv = in_w[:, o3:].astype(BF16)
    src = np.arange(2 * heads)
    direction, head = src // heads, src % heads
    place = np.zeros((2 * heads, SSD_G * LANES), np.float32)
    place[src, (head // e) * LANES + direction * e + head % e] = 1.0
    w_dt = jnp.dot(in_w[:, o2:o3].astype(F32), jnp.asarray(place), precision=lax.Precision.HIGHEST).astype(BF16)

    def grouped(v):
        vg = v.astype(F32).reshape(2, SSD_G, e).transpose(1, 0, 2).reshape(SSD_G, 2 * e)
        return jnp.zeros((SSD_G, LANES), F32).at[:, :2 * e].set(vg).reshape(SSD_G, 1, LANES)

    bias_g = grouped(dt_bias)
    a_g = grouped(-jnp.exp(a_log.astype(F32)))
    d_cols = jnp.repeat(d_skip.astype(F32), SSD_P)

    def ssd(t, s0):
        zx = _mm(t, in_w, n=o2)
        xbc = _dwconv(zx, conv_w, conv_b, SSD_CONV // 2, col0=o1, silu=True)
        y_f, y_b, s_t = _ssd_scan(xbc, _mm(t, w_dt), bias_g, a_g, s0, ssd_w=ssd_w, groups=SSD_G)
        return _ssd_gate_norm(y_f, y_b, xbc, zx, d_cols, norm_w, groups=SSD_G), s_t

    zero = jnp.zeros((2, SSD_G, e * SSD_P, SSD_N), F32)
    qkv_c = _mm(hc, w_qkv, out_dtype=BF16)
    y_ssd_c, s_c = ssd(hc, zero)
    qkv = _mm(h, w_qkv, out_dtype=BF16)
    y_ssd, _ = ssd(h, s_c)
    y_na = _neighbourhood_attention(qkv, qkv_c, rpb, heads=na_heads)
    y_na_c = _context_attention(qkv_c, heads=na_heads)
    return (y_ssd, y_na), (y_ssd_c, y_na_c)


def _odd_layer_mix(h, hc, in_w, conv_w, conv_b, gate_w, gate_b, lam, hy_params):
    rg_w = lam.shape[1]
    pad = RG_CONV // 2
    u_c = _dwconv(_mm(hc, in_w, col0=rg_w, n=rg_w), conv_w, conv_b, pad)
    _, h_c = _rglru(u_c, None, gate_w, gate_b, lam, jnp.zeros((2, rg_w), F32))
    proj = _mm(h, in_w)
    u = _dwconv(proj, conv_w, conv_b, pad, col0=rg_w)
    y_rg, _ = _rglru(u, proj, gate_w, gate_b, lam, h_c)
    return y_rg, _hyena(proj, 2 * rg_w, *hy_params)


def kernel(x, c, ctx, c_ctx, ada_w, ada_b, norm_mix_w, norm_ffn_w, norm_out_w, ev_in_w, ev_ssd_conv_w, ev_ssd_conv_b, ev_ssd_dt_bias, ev_ssd_a_log, ev_ssd_d, ev_ssd_norm_w, ev_na_rpb, ev_out_w, ev_ffn_w1, ev_ffn_w3, ev_ffn_w2, od_in_w, od_rg_conv_w, od_rg_conv_b, od_rg_gate_w, od_rg_gate_b, od_rg_lambda, od_hy_conv_w, od_hy_conv_b, od_hy_w_in, od_hy_b_in, od_hy_w_mid, od_hy_b_mid, od_hy_w_out, od_hy_freq, od_hy_deltas, od_hy_skip, od_out_w, od_router_w, od_moe_w1, od_moe_w3, od_moe_w2):
    batch, seq, d = x.shape
    assert batch == 1 and ada_w.shape[0] == 2, "kernel is written for one sequence and the even/odd layer pair"
    xs = x[0]
    cs = ctx[0]
    cvecs = jnp.concatenate([jax.nn.silu(c), jax.nn.silu(c_ctx)[None]], axis=0)
    mods = _ada(cvecs, ada_w, ada_b).reshape(2, 2, ADA_CHUNKS, d)

    mod, mod_c = mods[0, 0], mods[0, 1]
    h = _mod_norm(xs, norm_mix_w[0], mod[0], mod[1])
    hc = _mod_norm(cs, norm_mix_w[0], mod_c[0], mod_c[1])
    y, y_c = _even_layer_mix(h, hc, ev_in_w[0], ev_ssd_conv_w[0], ev_ssd_conv_b[0], ev_ssd_dt_bias[0],
                             ev_ssd_a_log[0], ev_ssd_d[0], ev_ssd_norm_w[0], ev_na_rpb[0])
    xs = _mm_residual2(*y, ev_out_w[0], xs, mod[2])
    cs = _mm_residual2(*y_c, ev_out_w[0], cs, mod_c[2])
    w1, w3, w2 = ev_ffn_w1[0], ev_ffn_w3[0], ev_ffn_w2[0].astype(BF16)
    h2 = _mod_norm(xs, norm_ffn_w[0], mod[3], mod[4])
    xs = _mm_residual(_mm_swiglu(h2, w1, w3), w2, xs, mod[5], tm=256, tn=512)
    h2c = _mod_norm(cs, norm_ffn_w[0], mod_c[3], mod_c[4])
    cs = _mm_residual(_mm_swiglu(h2c, w1, w3), w2, cs, mod_c[5], tm=256, tn=512)

    mod, mod_c = mods[1, 0], mods[1, 1]
    h = _mod_norm(xs, norm_mix_w[1], mod[0], mod[1])
    hc = _mod_norm(cs, norm_mix_w[1], mod_c[0], mod_c[1])
    hy_params = (od_hy_conv_w[0], od_hy_conv_b[0], od_hy_w_in[0], od_hy_b_in[0], od_hy_w_mid[0],
                 od_hy_b_mid[0], od_hy_w_out[0], od_hy_freq[0], od_hy_deltas[0], od_hy_skip[0])
    y = _odd_layer_mix(h, hc, od_in_w[0], od_rg_conv_w[0], od_rg_conv_b[0], od_rg_gate_w[0],
                       od_rg_gate_b[0], od_rg_lambda[0], hy_params)
    xs = _mm_residual2(*y, od_out_w[0], xs, mod[2])
    h2, route = _mod_norm_router(xs, norm_ffn_w[1], mod[3], mod[4], od_router_w[0])
    y1, y2 = _moe(h2, route, od_moe_w1[0], od_moe_w3[0], od_moe_w2[0])
    return _final_norm(xs, y1, y2, mod[5], norm_out_w)[None]
```

```python
import functools
import math

import numpy as np
import jax
import jax.numpy as jnp
from jax import lax
from jax.experimental import pallas as pl
from jax.experimental.pallas import tpu as pltpu

GRID_W = 64
ADA_CHUNKS = 6
NORM_EPS = 1e-6
SSD_P = 64
SSD_G = 4
SSD_N = 128
SSD_CONV = 4
SSD_CHUNK = 128
NA_DH = 128
NA_WIN_H = 8
NA_WIN_W = 16
NA_QCOLS = 16
NA_KCOLS = NA_QCOLS + NA_WIN_W
RG_BLOCKS = 16
RG_CONV = 4
RG_C = 8.0
HY_ORDER = 2
HY_CONV = 3
HY_EMB = 33
HY_BANDS = (HY_EMB - 1) // 2
HY_INNER = 2
TOP_K = 2

LANES = 128
SUBLANES = 8
VMEM_LIMIT_BYTES = 56 * 1024 * 1024
MOE_TILE_ROWS = 512

F32 = jnp.float32
BF16 = jnp.bfloat16


def _tile(n, target, align):
    for t in range(min(n, target), 0, -1):
        if n % t == 0 and t % align == 0:
            return t
    return n


def _params(*semantics):
    return pltpu.CompilerParams(dimension_semantics=semantics, vmem_limit_bytes=VMEM_LIMIT_BYTES)


def _ada_kernel(cb_ref, w_ref, b_ref, o_ref, *, n_rows):
    k, tn = w_ref.shape
    for l in range(tn // LANES):
        w = w_ref[:, l * LANES:(l + 1) * LANES].reshape(k // SUBLANES, SUBLANES, LANES)
        for r in range(n_rows):
            cb = cb_ref[r].reshape(k // SUBLANES, SUBLANES, LANES)
            part = jnp.sum(w * cb, axis=0)
            o_ref[r:r + 1, l * LANES:(l + 1) * LANES] = (
                jnp.sum(part, axis=0, keepdims=True) + b_ref[:, l * LANES:(l + 1) * LANES])


def _ada(cvecs, w, b):
    r, k = cvecs.shape
    layers, _, n = w.shape
    tn = _tile(n, 512, LANES)
    cb = jnp.broadcast_to(cvecs[:, :, None], (r, k, LANES))
    return pl.pallas_call(
        functools.partial(_ada_kernel, n_rows=r),
        grid=(layers, n // tn),
        in_specs=[pl.BlockSpec((r, k, LANES), lambda l, j: (0, 0, 0)),
                  pl.BlockSpec((None, k, tn), lambda l, j: (l, 0, j)),
                  pl.BlockSpec((None, 1, tn), lambda l, j: (l, 0, j))],
        out_specs=pl.BlockSpec((None, r, tn), lambda l, j: (l, 0, j)),
        out_shape=jax.ShapeDtypeStruct((layers, r, n), F32),
        compiler_params=_params("parallel", "parallel"),
        name="ada_matvec",
    )(cb, w, b.reshape(layers, 1, n))


def _norm_rows(x, w, shift, scale):
    y = x * lax.rsqrt(jnp.mean(x * x, axis=-1, keepdims=True) + NORM_EPS)
    y = y * w
    if scale is not None:
        y = y * (1.0 + scale) + shift
    return y


def _norm_kernel(x_ref, w_ref, sh_ref, sc_ref, o_ref):
    o_ref[...] = _norm_rows(x_ref[...], w_ref[...], sh_ref[...], sc_ref[...]).astype(o_ref.dtype)


def _final_norm_kernel(x_ref, y1_ref, y2_ref, gate_ref, w_ref, o_ref):
    x = x_ref[...] + gate_ref[...] * (y1_ref[...] + y2_ref[...])
    o_ref[...] = _norm_rows(x, w_ref[...], None, None).astype(o_ref.dtype)


def _norm_router_kernel(x_ref, w_ref, sh_ref, sc_ref, rw_ref, o_ref, route_ref, *, n_experts):
    h = _norm_rows(x_ref[...], w_ref[...], sh_ref[...], sc_ref[...])
    o_ref[...] = h.astype(o_ref.dtype)
    logits = jnp.dot(h, rw_ref[...], preferred_element_type=F32, precision=lax.Precision.HIGHEST)
    col = lax.broadcasted_iota(jnp.int32, logits.shape, 1)
    neg = jnp.float32(-jnp.inf)
    l1 = jnp.where(col < n_experts, logits, neg)
    m1 = jnp.max(l1, axis=-1, keepdims=True)
    i1 = jnp.min(jnp.where(l1 == m1, col, LANES), axis=-1, keepdims=True)
    l2 = jnp.where(col == i1, neg, l1)
    m2 = jnp.max(l2, axis=-1, keepdims=True)
    i2 = jnp.min(jnp.where(l2 == m2, col, LANES), axis=-1, keepdims=True)
    e2 = jnp.exp(m2 - m1)
    g1 = 1.0 / (1.0 + e2)
    g2 = e2 * g1
    route = jnp.where(col == 0, i1.astype(F32),
                      jnp.where(col == 1, i2.astype(F32),
                                jnp.where(col == 2, g1, jnp.where(col == 3, g2, 0.0))))
    route_ref[...] = route


def _mod_norm(x, w, shift, scale, out_dtype=BF16):
    m, d = x.shape
    tr = _tile(m, 256, SUBLANES)
    vec = pl.BlockSpec((1, d), lambda i: (0, 0))
    return pl.pallas_call(
        _norm_kernel,
        grid=(m // tr,),
        in_specs=[pl.BlockSpec((tr, d), lambda i: (i, 0)), vec, vec, vec],
        out_specs=pl.BlockSpec((tr, d), lambda i: (i, 0)),
        out_shape=jax.ShapeDtypeStruct((m, d), out_dtype),
        compiler_params=_params("parallel"),
        name="mod_norm",
    )(x, w.reshape(1, d), shift.reshape(1, d), scale.reshape(1, d))


def _final_norm(x, y1, y2, gate, w):
    m, d = x.shape
    tr = _tile(m, 256, SUBLANES)
    rows = pl.BlockSpec((tr, d), lambda i: (i, 0))
    vec = pl.BlockSpec((1, d), lambda i: (0, 0))
    return pl.pallas_call(
        _final_norm_kernel,
        grid=(m // tr,),
        in_specs=[rows, rows, rows, vec, vec],
        out_specs=rows,
        out_shape=jax.ShapeDtypeStruct((m, d), x.dtype),
        compiler_params=_params("parallel"),
        name="final_norm",
    )(x, y1, y2, gate.reshape(1, d), w.reshape(1, d))


def _mod_norm_router(x, w, shift, scale, router_w):
    m, d = x.shape
    n_experts = router_w.shape[1]
    tr = _tile(m, 256, SUBLANES)
    rw = jnp.zeros((d, LANES), F32).at[:, :n_experts].set(router_w.astype(F32))
    vec = pl.BlockSpec((1, d), lambda i: (0, 0))
    return pl.pallas_call(
        functools.partial(_norm_router_kernel, n_experts=n_experts),
        grid=(m // tr,),
        in_specs=[pl.BlockSpec((tr, d), lambda i: (i, 0)), vec, vec, vec,
                  pl.BlockSpec((d, LANES), lambda i: (0, 0))],
        out_specs=[pl.BlockSpec((tr, d), lambda i: (i, 0)),
                   pl.BlockSpec((tr, LANES), lambda i: (i, 0))],
        out_shape=[jax.ShapeDtypeStruct((m, d), BF16), jax.ShapeDtypeStruct((m, LANES), F32)],
        compiler_params=_params("parallel"),
        name="mod_norm_router",
    )(x, w.reshape(1, d), shift.reshape(1, d), scale.reshape(1, d), rw)


def _bf16(ref):
    return ref[...].astype(BF16)


def _mm_kernel(a_ref, b_ref, o_ref):
    o_ref[...] = jnp.dot(a_ref[...], _bf16(b_ref), preferred_element_type=F32).astype(o_ref.dtype)


def _mm_residual_kernel(a_ref, b_ref, res_ref, gate_ref, o_ref):
    acc = jnp.dot(a_ref[...], _bf16(b_ref), preferred_element_type=F32)
    o_ref[...] = res_ref[...] + gate_ref[...] * acc


def _mm_swiglu_kernel(a_ref, b1_ref, b3_ref, o_ref):
    a = a_ref[...]
    g = jnp.dot(a, _bf16(b1_ref), preferred_element_type=F32)
    u = jnp.dot(a, _bf16(b3_ref), preferred_element_type=F32)
    o_ref[...] = (g * jax.nn.sigmoid(g) * u).astype(o_ref.dtype)


def _mm_tiles(m, k, n, tm, tn):
    tm = _tile(m, tm, SUBLANES)
    tn = _tile(n, tn, LANES)
    return tm, tn


def _mm(a, b, *, out_dtype=F32, tm=1024, tn=512, col0=0, n=None):
    m, k = a.shape
    n = b.shape[1] - col0 if n is None else n
    tm, tn = _mm_tiles(m, k, math.gcd(n, col0) if col0 else n, tm, tn)
    off = col0 // tn
    return pl.pallas_call(
        _mm_kernel,
        grid=(n // tn, m // tm),
        in_specs=[pl.BlockSpec((tm, k), lambda j, i: (i, 0)),
                  pl.BlockSpec((k, tn), lambda j, i: (0, j + off))],
        out_specs=pl.BlockSpec((tm, tn), lambda j, i: (i, j)),
        out_shape=jax.ShapeDtypeStruct((m, n), out_dtype),
        compiler_params=_params("parallel", "parallel"),
        name="matmul",
    )(a, b)


def _mm_residual(a, b, res, gate, *, tm=512, tn=1024):
    m, k = a.shape
    n = b.shape[1]
    tm, tn = _mm_tiles(m, k, n, tm, tn)
    return pl.pallas_call(
        _mm_residual_kernel,
        grid=(n // tn, m // tm),
        in_specs=[pl.BlockSpec((tm, k), lambda j, i: (i, 0)),
                  pl.BlockSpec((k, tn), lambda j, i: (0, j)),
                  pl.BlockSpec((tm, tn), lambda j, i: (i, j)),
                  pl.BlockSpec((1, tn), lambda j, i: (0, j))],
        out_specs=pl.BlockSpec((tm, tn), lambda j, i: (i, j)),
        out_shape=jax.ShapeDtypeStruct((m, n), F32),
        compiler_params=_params("parallel", "parallel"),
        name="matmul_residual",
    )(a, b, res, gate.reshape(1, n))


def _mm_residual2_kernel(a1_ref, a2_ref, b_ref, res_ref, gate_ref, o_ref):
    k1 = a1_ref.shape[1]
    acc = jnp.dot(a1_ref[...], b_ref[:k1, :].astype(BF16), preferred_element_type=F32)
    acc = acc + jnp.dot(a2_ref[...], b_ref[k1:, :].astype(BF16), preferred_element_type=F32)
    o_ref[...] = res_ref[...] + gate_ref[...] * acc


def _mm_residual2(a1, a2, b, res, gate, *, tm=1024, tn=512):
    m, k1 = a1.shape
    k2 = a2.shape[1]
    n = b.shape[1]
    tm, tn = _mm_tiles(m, k1 + k2, n, tm, tn)
    return pl.pallas_call(
        _mm_residual2_kernel,
        grid=(n // tn, m // tm),
        in_specs=[pl.BlockSpec((tm, k1), lambda j, i: (i, 0)),
                  pl.BlockSpec((tm, k2), lambda j, i: (i, 0)),
                  pl.BlockSpec((k1 + k2, tn), lambda j, i: (0, j)),
                  pl.BlockSpec((tm, tn), lambda j, i: (i, j)),
                  pl.BlockSpec((1, tn), lambda j, i: (0, j))],
        out_specs=pl.BlockSpec((tm, tn), lambda j, i: (i, j)),
        out_shape=jax.ShapeDtypeStruct((m, n), F32),
        compiler_params=_params("parallel", "parallel"),
        name="matmul_residual2",
    )(a1, a2, b, res, gate.reshape(1, n))


def _mm_swiglu(a, b1, b3, *, tm=1024, tn=256):
    m, k = a.shape
    n = b1.shape[1]
    tm, tn = _mm_tiles(m, k, n, tm, tn)
    return pl.pallas_call(
        _mm_swiglu_kernel,
        grid=(m // tm, n // tn),
        in_specs=[pl.BlockSpec((tm, k), lambda i, j: (i, 0)),
                  pl.BlockSpec((k, tn), lambda i, j: (0, j)),
                  pl.BlockSpec((k, tn), lambda i, j: (0, j))],
        out_specs=pl.BlockSpec((tm, tn), lambda i, j: (i, j)),
        out_shape=jax.ShapeDtypeStruct((m, n), BF16),
        compiler_params=_params("parallel", "parallel"),
        name="matmul_swiglu",
    )(a, b1, b3)


def _moe_up_kernel(te_ref, nt_ref, a_ref, b1_ref, b3_ref, o_ref):
    @pl.when(pl.program_id(1) < nt_ref[0])
    def _():
        a = a_ref[...]
        g = jnp.dot(a, _bf16(b1_ref), preferred_element_type=F32)
        u = jnp.dot(a, _bf16(b3_ref), preferred_element_type=F32)
        o_ref[...] = (g * jax.nn.sigmoid(g) * u).astype(o_ref.dtype)


def _moe_down_kernel(te_ref, nt_ref, a_ref, b_ref, gate_ref, o_ref):
    @pl.when(pl.program_id(1) < nt_ref[0])
    def _():
        acc = jnp.dot(a_ref[...], _bf16(b_ref), preferred_element_type=F32)
        o_ref[...] = gate_ref[...] * acc


def _moe_grouped(xg, gate_slot, tile_expert, n_tiles, w1, w3, w2, *, tm):
    s, d = xg.shape
    f = w1.shape[2]
    n_t = s // tm
    tf = _tile(f, 512, LANES)
    td = _tile(d, 1024, LANES)

    def row(i, nt):
        return jnp.minimum(i, nt[0] - 1)

    up = pl.pallas_call(
        _moe_up_kernel,
        grid_spec=pltpu.PrefetchScalarGridSpec(
            num_scalar_prefetch=2,
            grid=(f // tf, n_t),
            in_specs=[
                pl.BlockSpec((tm, d), lambda j, i, te, nt: (row(i, nt), 0)),
                pl.BlockSpec((None, d, tf), lambda j, i, te, nt: (te[row(i, nt)], 0, j)),
                pl.BlockSpec((None, d, tf), lambda j, i, te, nt: (te[row(i, nt)], 0, j)),
            ],
            out_specs=pl.BlockSpec((tm, tf), lambda j, i, te, nt: (row(i, nt), j)),
        ),
        out_shape=jax.ShapeDtypeStruct((s, f), BF16),
        compiler_params=_params("arbitrary", "arbitrary"),
        name="moe_up",
    )(tile_expert, n_tiles, xg, w1, w3)

    return pl.pallas_call(
        _moe_down_kernel,
        grid_spec=pltpu.PrefetchScalarGridSpec(
            num_scalar_prefetch=2,
            grid=(d // td, n_t),
            in_specs=[
                pl.BlockSpec((tm, f), lambda j, i, te, nt: (row(i, nt), 0)),
                pl.BlockSpec((None, f, td), lambda j, i, te, nt: (te[row(i, nt)], 0, j)),
                pl.BlockSpec((tm, 1), lambda j, i, te, nt: (row(i, nt), 0)),
            ],
            out_specs=pl.BlockSpec((tm, td), lambda j, i, te, nt: (row(i, nt), j)),
        ),
        out_shape=jax.ShapeDtypeStruct((s, d), F32),
        compiler_params=_params("arbitrary", "arbitrary"),
        name="moe_down",
    )(tile_expert, n_tiles, up, w2, gate_slot)


def _moe(h2, route, w1, w3, w2):
    n, d = h2.shape
    n_e = w1.shape[0]
    tm = min(MOE_TILE_ROWS, n)
    n_t = -(-(TOP_K * n + n_e * (tm - 1)) // tm)
    s = n_t * tm
    expert = route[:, :TOP_K].astype(jnp.int32).T.reshape(-1)
    gate = route[:, TOP_K:2 * TOP_K].T.reshape(-1)
    token = jnp.tile(jnp.arange(n, dtype=jnp.int32), TOP_K)
    onehot = (expert[:, None] == jnp.arange(n_e, dtype=jnp.int32)[None]).astype(jnp.int32)
    csum = jnp.cumsum(onehot, axis=0)
    rank = jnp.sum(onehot * csum, axis=1) - 1
    counts = csum[-1]
    padded = -(-counts // tm) * tm
    ends = jnp.cumsum(padded)
    starts = ends - padded
    slot = starts[expert] + rank
    token_of_slot = jnp.zeros((s,), jnp.int32).at[slot].set(token)
    gate_of_slot = jnp.zeros((s,), F32).at[slot].set(gate)
    n_tiles = (ends[-1] // tm).astype(jnp.int32).reshape(1)
    tile_start = jnp.arange(n_t, dtype=jnp.int32) * tm
    tile_expert = jnp.minimum(jnp.sum((tile_start[:, None] >= ends[None]).astype(jnp.int32), axis=1), n_e - 1)
    xg = h2.at[token_of_slot].get(mode="promise_in_bounds")
    yo = _moe_grouped(xg, gate_of_slot.reshape(s, 1), tile_expert, n_tiles, w1, w3, w2, tm=tm)
    slots = slot.reshape(TOP_K, n)
    return [yo.at[slots[kk]].get(mode="promise_in_bounds") for kk in range(TOP_K)]


def _dwconv_kernel(prev_ref, cur_ref, next_ref, w_ref, b_ref, o_ref, *, pad_left, silu):
    i = pl.program_id(1)
    cur = cur_ref[...]
    rows = cur.shape[0]
    prev = jnp.where(i > 0, prev_ref[...], 0.0)
    nxt = jnp.where(i < pl.num_programs(1) - 1, next_ref[...], 0.0)
    ext = jnp.concatenate([prev, cur, nxt], axis=0)
    acc = jnp.zeros_like(cur) + b_ref[...]
    for j in range(w_ref.shape[0]):
        s = j - pad_left
        us = cur if s == 0 else pltpu.roll(ext, (-s) % ext.shape[0], axis=0)[SUBLANES:SUBLANES + rows]
        acc = acc + w_ref[j:j + 1, :] * us
    if silu:
        acc = acc * jax.nn.sigmoid(acc)
    o_ref[...] = acc


def _dwconv(u, w, b, pad_left, *, col0=0, silu=False):
    n = u.shape[0]
    k, c = w.shape
    assert k - 1 <= SUBLANES
    tc = _tile(math.gcd(c, col0) if col0 else c, 512, LANES)
    off = col0 // tc
    tr = _tile(n, 1024, SUBLANES)
    per = tr // SUBLANES
    last = n // SUBLANES - 1
    return pl.pallas_call(
        functools.partial(_dwconv_kernel, pad_left=pad_left, silu=silu),
        grid=(c // tc, n // tr),
        in_specs=[pl.BlockSpec((SUBLANES, tc), lambda j, i: (jnp.maximum(i * per - 1, 0), j + off)),
                  pl.BlockSpec((tr, tc), lambda j, i: (i, j + off)),
                  pl.BlockSpec((SUBLANES, tc), lambda j, i: (jnp.minimum((i + 1) * per, last), j + off)),
                  pl.BlockSpec((k, tc), lambda j, i: (0, j)),
                  pl.BlockSpec((1, tc), lambda j, i: (0, j))],
        out_specs=pl.BlockSpec((tr, tc), lambda j, i: (i, j)),
        out_shape=jax.ShapeDtypeStruct((n, c), F32),
        compiler_params=_params("parallel", "parallel"),
        name="dwconv",
    )(u, u, u, w.astype(F32), b.astype(F32).reshape(1, c))


def _ssd_kernel(xf_ref, bf_ref, cf_ref, dtf_ref, xb_ref, bb_ref, cb_ref, dtb_ref, bias_ref, a_ref, s0_ref,
                yf_ref, yb_ref, st_ref, state, *, heads_per_group, head_dim):
    step = pl.program_id(1)

    @pl.when(step == 0)
    def _():
        state[...] = s0_ref[...]

    q, gw = xf_ref.shape
    hp = lax.Precision.HIGHEST
    nt = (((1,), (1,)), ((), ()))
    row = lax.broadcasted_iota(jnp.int32, (q, q), 0)
    col = lax.broadcasted_iota(jnp.int32, (q, q), 1)
    tri = (row >= col).astype(F32)
    neg = jnp.float32(-jnp.inf)
    head_of_col = lax.broadcasted_iota(jnp.int32, (LANES, gw), 1) // head_dim
    lane_id = lax.broadcasted_iota(jnp.int32, (LANES, gw), 0)
    low_half = lax.broadcasted_iota(jnp.int32, (q, LANES), 1) < head_dim
    heads_per_tile = LANES // head_dim
    streams = ((xf_ref, bf_ref, cf_ref, dtf_ref, yf_ref), (xb_ref, bb_ref, cb_ref, dtb_ref, yb_ref))
    for d, (x_ref, b_ref, c_ref, dt_ref, y_ref) in enumerate(streams):
        xs = x_ref[...]
        bm = b_ref[...].astype(BF16)
        cm = c_ref[...].astype(BF16)
        dt = jax.nn.softplus(dt_ref[...] + bias_ref[...])
        da = dt * a_ref[...]
        cum = jnp.dot(tri, da, preferred_element_type=F32, precision=hp)
        total = cum[q - 1:q, :]
        pos = cum if d == 0 else cum - da
        pos_t = pos.T
        if d == 0:
            carry_in, carry_out = jnp.exp(pos), jnp.exp(total - pos)
        else:
            carry_in, carry_out = jnp.exp(total - pos), jnp.exp(pos)
        spread = (lane_id == head_of_col + d * heads_per_group).astype(F32)
        xd = xs * jnp.dot(dt, spread, preferred_element_type=F32, precision=hp)
        st = state[d]
        y_off = (lax.dot_general(cm, st.astype(BF16), nt, preferred_element_type=F32)
                 * jnp.dot(carry_in, spread, preferred_element_type=F32, precision=hp))
        xd_out = (xd * jnp.dot(carry_out, spread, preferred_element_type=F32, precision=hp)).astype(BF16)
        upd = lax.dot_general(xd_out, bm, (((0,), (0,)), ((), ())), preferred_element_type=F32)
        keep = lax.dot_general(spread, jnp.exp(total), (((0,), (1,)), ((), ())),
                               preferred_element_type=F32, precision=hp)
        state[d] = keep * st + upd
        cb = lax.dot_general(cm, bm, nt, preferred_element_type=F32)
        mask = (row >= col) if d == 0 else (col >= row)
        xdb = xd.astype(BF16)
        for tile in range(gw // LANES):
            cols = slice(tile * LANES, (tile + 1) * LANES)
            parts = []
            for e in range(tile * heads_per_tile, (tile + 1) * heads_per_tile):
                lane = d * heads_per_group + e
                p_col = pos[:, lane:lane + 1]
                p_row = pos_t[lane:lane + 1, :]
                expo = (p_col - p_row) if d == 0 else (p_row - p_col)
                dec = jnp.exp(jnp.where(mask, expo, neg))
                parts.append(jnp.dot((cb * dec).astype(BF16), xdb[:, cols], preferred_element_type=F32))
            y_diag = parts[0] if len(parts) == 1 else jnp.where(low_half, parts[0], parts[1])
            y_ref[:, cols] = y_diag + y_off[:, cols]

    @pl.when(step == pl.num_programs(1) - 1)
    def _():
        st_ref[...] = state[...]


def _ssd_scan(xbc, dtg, bias_g, a_g, s0, *, ssd_w, groups):
    n = xbc.shape[0]
    e = ssd_w // (SSD_P * groups)
    q = min(SSD_CHUNK, n)
    steps = n // q
    gw = e * SSD_P
    nb = ssd_w // SSD_N
    assert LANES % SSD_P == 0 and LANES // SSD_P <= 2 and gw % LANES == 0 and 2 * e <= LANES

    def fwd(blk):
        return lambda g, s: (s, blk(g))

    def bwd(blk):
        return lambda g, s: (steps - 1 - s, blk(g))

    def stream(order):
        return [pl.BlockSpec((q, gw), order(lambda g: g)),
                pl.BlockSpec((q, SSD_N), order(lambda g: nb + g)),
                pl.BlockSpec((q, SSD_N), order(lambda g: nb + groups + g))]

    grp = pl.BlockSpec((None, 1, LANES), lambda g, s: (g, 0, 0))
    st_spec = pl.BlockSpec((2, None, gw, SSD_N), lambda g, s: (0, g, 0, 0))
    return pl.pallas_call(
        functools.partial(_ssd_kernel, heads_per_group=e, head_dim=SSD_P),
        grid=(groups, steps),
        in_specs=(stream(fwd) + [pl.BlockSpec((q, LANES), fwd(lambda g: g))]
                  + stream(bwd) + [pl.BlockSpec((q, LANES), bwd(lambda g: g))]
                  + [grp, grp, st_spec]),
        out_specs=[pl.BlockSpec((q, gw), fwd(lambda g: g)), pl.BlockSpec((q, gw), bwd(lambda g: g)), st_spec],
        out_shape=[jax.ShapeDtypeStruct((n, ssd_w), F32), jax.ShapeDtypeStruct((n, ssd_w), F32),
                   jax.ShapeDtypeStruct(s0.shape, F32)],
        scratch_shapes=[pltpu.VMEM((2, gw, SSD_N), F32)],
        compiler_params=_params("parallel", "arbitrary"),
        name="ssd_scan",
    )(xbc, xbc, xbc, dtg, xbc, xbc, xbc, dtg, bias_g, a_g, s0)


def _ssd_gate_norm_kernel(yf_ref, yb_ref, xs_ref, z_ref, d_ref, nw_ref, o_ref, *, groups):
    y = yf_ref[...] + yb_ref[...] + xs_ref[...] * d_ref[...]
    z = z_ref[...]
    y = y * (z * jax.nn.sigmoid(z))
    gw = y.shape[1] // groups
    for g in range(groups):
        yg = y[:, g * gw:(g + 1) * gw]
        yg = yg * lax.rsqrt(jnp.mean(yg * yg, axis=-1, keepdims=True) + NORM_EPS)
        o_ref[:, g * gw:(g + 1) * gw] = (yg * nw_ref[:, g * gw:(g + 1) * gw]).astype(o_ref.dtype)


def _ssd_gate_norm(y_f, y_b, xbc, zx, d_cols, norm_w, *, groups):
    n, w = y_f.shape
    tr = _tile(n, 256, SUBLANES)
    rows = pl.BlockSpec((tr, w), lambda i: (i, 0))
    vec = pl.BlockSpec((1, w), lambda i: (0, 0))
    return pl.pallas_call(
        functools.partial(_ssd_gate_norm_kernel, groups=groups),
        grid=(n // tr,),
        in_specs=[rows, rows, rows, rows, vec, vec],
        out_specs=rows,
        out_shape=jax.ShapeDtypeStruct((n, w), BF16),
        compiler_params=_params("parallel"),
        name="ssd_gate_norm",
    )(y_f, y_b, xbc, zx, d_cols.reshape(1, w), norm_w.astype(F32).reshape(1, w))


def _na_kernel(q_ref, k_ref, v_ref, kc_ref, vc_ref, bias_ref, o_ref, *, grid_rows, win_h, rows_per_block, scale):
    blk = pl.program_id(1)
    kc = kc_ref[...]
    vc = vc_ref[...]
    nt = (((1,), (1,)), ((), ()))

    def one_row(i, carry):
        r = blk * rows_per_block + i
        rs = jnp.clip(r - win_h // 2, 0, grid_rows - win_h)
        q = q_ref[pl.ds(pl.multiple_of(i * GRID_W, GRID_W), GRID_W), :]
        start = pl.multiple_of(rs * GRID_W, GRID_W)
        kw = k_ref[pl.ds(start, win_h * GRID_W), :]
        vw = v_ref[pl.ds(start, win_h * GRID_W), :]
        s = lax.dot_general(q, kw, nt, preferred_element_type=F32) * scale + bias_ref[r - rs]
        sc = lax.dot_general(q, kc, nt, preferred_element_type=F32) * scale
        m = jnp.maximum(jnp.max(s, axis=-1, keepdims=True), jnp.max(sc, axis=-1, keepdims=True))
        p = jnp.exp(s - m)
        pc = jnp.exp(sc - m)
        denom = jnp.sum(p, axis=-1, keepdims=True) + jnp.sum(pc, axis=-1, keepdims=True)
        o = (jnp.dot(p.astype(BF16), vw, preferred_element_type=F32)
             + jnp.dot(pc.astype(BF16), vc, preferred_element_type=F32))
        o_ref[pl.ds(pl.multiple_of(i * GRID_W, GRID_W), GRID_W), :] = (o / denom).astype(o_ref.dtype)
        return carry

    lax.fori_loop(0, rows_per_block, one_row, 0, unroll=True)


def _na_bias_table(rpb, win_h):
    n_dcol = 2 * NA_WIN_W - 1
    p = np.arange(win_h)[:, None]
    a = np.arange(win_h)[None, :]
    rows = rpb.astype(F32)[:, a - p + NA_WIN_H - 1, :]
    qc = np.arange(GRID_W)[:, None]
    kc = np.arange(GRID_W)[None, :]
    dcol = np.clip(kc - qc + NA_WIN_W - 1, 0, n_dcol - 1)
    onehot = (np.arange(n_dcol)[:, None, None] == dcol[None]).astype(np.float32)
    bias = jnp.einsum('hpad,dqk->hpqak', rows, jnp.asarray(onehot), precision=lax.Precision.HIGHEST)
    cstart = np.clip(qc - NA_WIN_W // 2, 0, GRID_W - NA_WIN_W)
    ok = (kc >= cstart) & (kc < cstart + NA_WIN_W)
    bias = jnp.where(ok[None, None, :, None, :], bias, -jnp.inf)
    return bias.reshape(rpb.shape[0], win_h, GRID_W, win_h * GRID_W)


def _neighbourhood_attention(qkv, qkv_c, rpb, *, heads):
    n = qkv.shape[0]
    c = qkv_c.shape[0]
    grid_rows = n // GRID_W
    win_h = min(NA_WIN_H, grid_rows)
    rows_per_block = _tile(grid_rows, 8, 1)
    tq = rows_per_block * GRID_W
    bias = _na_bias_table(rpb, win_h)
    return pl.pallas_call(
        functools.partial(_na_kernel, grid_rows=grid_rows, win_h=win_h, rows_per_block=rows_per_block,
                          scale=NA_DH ** -0.5),
        grid=(heads, n // tq),
        in_specs=[pl.BlockSpec((tq, NA_DH), lambda h, i: (i, h)),
                  pl.BlockSpec((n, NA_DH), lambda h, i: (0, heads + h)),
                  pl.BlockSpec((n, NA_DH), lambda h, i: (0, 2 * heads + h)),
                  pl.BlockSpec((c, NA_DH), lambda h, i: (0, heads + h)),
                  pl.BlockSpec((c, NA_DH), lambda h, i: (0, 2 * heads + h)),
                  pl.BlockSpec((None, win_h, GRID_W, win_h * GRID_W), lambda h, i: (h, 0, 0, 0))],
        out_specs=pl.BlockSpec((tq, NA_DH), lambda h, i: (i, h)),
        out_shape=jax.ShapeDtypeStruct((n, heads * NA_DH), BF16),
        compiler_params=_params("parallel", "arbitrary"),
        name="neighbourhood_attention",
    )(qkv, qkv, qkv, qkv_c, qkv_c, bias)


def _ctx_attn_kernel(q_ref, k_ref, v_ref, o_ref, *, scale):
    s = lax.dot_general(q_ref[...], k_ref[...], (((1,), (1,)), ((), ())), preferred_element_type=F32) * scale
    p = jnp.exp(s - jnp.max(s, axis=-1, keepdims=True))
    o = jnp.dot(p.astype(BF16), v_ref[...], preferred_element_type=F32)
    o_ref[...] = (o / jnp.sum(p, axis=-1, keepdims=True)).astype(o_ref.dtype)


def _context_attention(qkv_c, *, heads):
    c = qkv_c.shape[0]
    return pl.pallas_call(
        functools.partial(_ctx_attn_kernel, scale=NA_DH ** -0.5),
        grid=(heads,),
        in_specs=[pl.BlockSpec((c, NA_DH), lambda h: (0, h)),
                  pl.BlockSpec((c, NA_DH), lambda h: (0, heads + h)),
                  pl.BlockSpec((c, NA_DH), lambda h: (0, 2 * heads + h))],
        out_specs=pl.BlockSpec((c, NA_DH), lambda h: (0, h)),
        out_shape=jax.ShapeDtypeStruct((c, heads * NA_DH), BF16),
        compiler_params=_params("parallel"),
        name="context_attention",
    )(qkv_c, qkv_c, qkv_c)


def _rglru_kernel(*refs, chunk, emit_y):
    if emit_y:
        u_ref, wg_ref, bg_ref, lam_ref, h0_ref, gate_ref, y_ref, ht_ref, hsum = refs
    else:
        u_ref, wg_ref, bg_ref, lam_ref, h0_ref, ht_ref = refs
    n, w = u_ref.shape
    n_chunks = n // chunk
    tiles = chunk // SUBLANES
    sub = lax.broadcasted_iota(jnp.int32, (chunk, w), 0) % SUBLANES
    log_sig = jax.nn.log_sigmoid(lam_ref[...])

    def chunk_scan(d, c0, carry):
        u = u_ref[pl.ds(c0, chunk), :]
        ub = u.astype(BF16)
        r = jax.nn.sigmoid(jnp.dot(ub, wg_ref[d, 0], preferred_element_type=F32) + bg_ref[d, 0])
        i = jax.nn.sigmoid(jnp.dot(ub, wg_ref[d, 1], preferred_element_type=F32) + bg_ref[d, 1])
        log_a = RG_C * r * log_sig[d]
        a = jnp.exp(log_a)
        b = jnp.sqrt(jnp.maximum(1.0 - a * a, 0.0)) * i * u
        for sh in (1, 2, 4):
            if d == 0:
                a_s, b_s, edge = pltpu.roll(a, sh, axis=0), pltpu.roll(b, sh, axis=0), sub < sh
            else:
                a_s, b_s = pltpu.roll(a, chunk - sh, axis=0), pltpu.roll(b, chunk - sh, axis=0)
                edge = sub >= SUBLANES - sh
            b = jnp.where(edge, b, a * b_s + b)
            a = jnp.where(edge, a, a * a_s)
        hs = [None] * tiles
        order = range(tiles) if d == 0 else range(tiles - 1, -1, -1)
        for k in order:
            sl = slice(k * SUBLANES, (k + 1) * SUBLANES)
            h = a[sl] * carry + b[sl]
            hs[k] = h
            carry = h[SUBLANES - 1:SUBLANES] if d == 0 else h[0:1]
        return jnp.concatenate(hs, axis=0), carry

    def step(j, carries, second_pass):
        cf, cb = carries
        rows_f = pl.ds(pl.multiple_of(j * chunk, chunk), chunk)
        rows_b = pl.ds(pl.multiple_of((n_chunks - 1 - j) * chunk, chunk), chunk)
        hf, cf = chunk_scan(0, pl.multiple_of(j * chunk, chunk), cf)
        hb, cb = chunk_scan(1, pl.multiple_of((n_chunks - 1 - j) * chunk, chunk), cb)
        if emit_y:
            if second_pass:
                y_ref[rows_f, :] = (jax.nn.gelu(gate_ref[rows_f, :]) * (hsum[rows_f, :] + hf)).astype(y_ref.dtype)
                y_ref[rows_b, :] = (jax.nn.gelu(gate_ref[rows_b, :]) * (hsum[rows_b, :] + hb)).astype(y_ref.dtype)
            else:
                hsum[rows_f, :] = hf
                hsum[rows_b, :] = hb
        return cf, cb

    carries = (h0_ref[0], h0_ref[1])
    half = n_chunks // 2
    carries = lax.fori_loop(0, half, lambda j, c: step(j, c, False), carries)
    carries = lax.fori_loop(half, n_chunks, lambda j, c: step(j, c, True), carries)
    ht_ref[0] = carries[0]
    ht_ref[1] = carries[1]


def _rglru(u, gate_src, gate_w, gate_b, lam, h0):
    n, w = u.shape
    nb, bw = gate_w.shape[2], gate_w.shape[3]
    chunk = _tile(n // 2, 256, SUBLANES)
    assert (n // chunk) % 2 == 0, "the two sweeps hand over at the middle chunk boundary"
    emit_y = gate_src is not None
    col = lambda j: (0, j)
    in_specs = [pl.BlockSpec((n, bw), col),
                pl.BlockSpec((2, 2, None, bw, bw), lambda j: (0, 0, j, 0, 0)),
                pl.BlockSpec((2, 2, None, 1, bw), lambda j: (0, 0, j, 0, 0)),
                pl.BlockSpec((2, 1, bw), lambda j: (0, 0, j)),
                pl.BlockSpec((2, 1, bw), lambda j: (0, 0, j))]
    args = [u, gate_w.astype(BF16), gate_b.astype(F32).reshape(2, 2, nb, 1, bw), lam.astype(F32).reshape(2, 1, w),
            h0.reshape(2, 1, w)]
    ht_spec = pl.BlockSpec((2, 1, bw), lambda j: (0, 0, j))
    ht_shape = jax.ShapeDtypeStruct((2, 1, w), F32)
    if emit_y:
        in_specs.append(pl.BlockSpec((n, bw), col))
        args.append(gate_src)
        out_specs = [pl.BlockSpec((n, bw), col), ht_spec]
        out_shape = [jax.ShapeDtypeStruct((n, w), BF16), ht_shape]
        scratch = [pltpu.VMEM((n, bw), F32)]
    else:
        out_specs, out_shape, scratch = ht_spec, ht_shape, []
    out = pl.pallas_call(
        functools.partial(_rglru_kernel, chunk=chunk, emit_y=emit_y),
        grid=(nb,),
        in_specs=in_specs,
        out_specs=out_specs,
        out_shape=out_shape,
        scratch_shapes=scratch,
        compiler_params=_params("parallel"),
        name="rglru",
    )(*args)
    if emit_y:
        return out[0], out[1].reshape(2, w)
    return None, out.reshape(2, w)


def _dft_tables(n):
    n1 = 1 << (int(math.log2(n)) // 2)
    n2 = n // n1
    assert n1 * n2 == n and n2 % 2 == 0
    i1, i2 = np.arange(n1), np.arange(n2)
    f1 = np.exp(-2j * np.pi * np.outer(i1, i1) / n1)
    f2 = np.exp(-2j * np.pi * np.outer(i2, i2) / n2)
    tw = np.exp(-2j * np.pi * np.outer(i2, i1) / n)
    return n1, n2, f1, f2, tw


def _stack_ri(m, sign=1.0):
    return jnp.asarray(np.concatenate([m.real, sign * m.imag], axis=0), BF16)


def _hy_hidden_kernel(feat_ref, w_in_ref, b_in_ref, w_mid_ref, b_mid_ref, freq_ref, o_ref):
    hp = lax.Precision.HIGHEST
    freq = freq_ref[...]
    hid = jnp.sin(freq * (jnp.dot(feat_ref[...], w_in_ref[...], preferred_element_type=F32, precision=hp)
                          + b_in_ref[...]))
    for m in range(w_mid_ref.shape[0]):
        hid = jnp.sin(freq * (jnp.dot(hid, w_mid_ref[m], preferred_element_type=F32, precision=hp)
                              + b_mid_ref[m]))
    o_ref[...] = hid


def _hy_hidden(L, w_in, b_in, w_mid, b_mid, freq):
    t = jnp.linspace(0.0, 1.0, L, dtype=F32)[:, None]
    ang = ((2.0 * math.pi / L) * jnp.arange(L, dtype=F32)[:, None]
           * jnp.linspace(1e-4, HY_BANDS - 1, HY_BANDS, dtype=F32)[None])
    emb, ffn = w_in.shape
    assert ffn <= LANES
    feats = jnp.zeros((L, LANES), F32).at[:, :emb].set(jnp.concatenate([t, jnp.cos(ang), -jnp.sin(ang)], axis=-1))

    def padded(v, shape):
        return jnp.zeros(shape, F32).at[tuple(slice(0, n) for n in v.shape)].set(v.astype(F32))

    tr = _tile(L, 512, SUBLANES)
    full = lambda *shape: pl.BlockSpec(shape, lambda *_: (0,) * len(shape))
    return pl.pallas_call(
        _hy_hidden_kernel,
        grid=(L // tr,),
        in_specs=[pl.BlockSpec((tr, LANES), lambda i: (i, 0)), full(LANES, LANES), full(1, LANES),
                  full(HY_INNER, LANES, LANES), full(HY_INNER, 1, LANES), full(1, LANES)],
        out_specs=pl.BlockSpec((tr, LANES), lambda i: (i, 0)),
        out_shape=jax.ShapeDtypeStruct((L, LANES), F32),
        compiler_params=_params("parallel"),
        name="hyena_filter_hidden",
    )(feats, padded(w_in, (LANES, LANES)), padded(b_in[None], (1, LANES)), padded(w_mid, (HY_INNER, LANES, LANES)),
      padded(b_mid[:, None], (HY_INNER, 1, LANES)), padded(freq[None], (1, LANES)))


def _hy_filter_stage1_kernel(hf_ref, hb_ref, wf_ref, wb_ref, df_ref, db_ref, f2f_ref, f2b_ref, twr_ref, twi_ref,
                             o_ref, *, n1, length):
    i = pl.program_id(0)
    half = hf_ref.shape[0]
    n2 = o_ref.shape[1]
    row = lax.broadcasted_iota(jnp.int32, (half, 1), 0)
    scale = 1.0 / (length - 1)

    def taps(h_ref, w_ref, d_ref, first):
        t = (first + n1 * row).astype(F32) * scale
        y = jnp.dot(h_ref[...].astype(BF16), w_ref[...], preferred_element_type=F32)
        return (y * jnp.exp(-t * jnp.abs(d_ref[...]))).astype(BF16)

    f2b = jnp.where(i == 0, f2b_ref[0], f2b_ref[1])
    p = (jnp.dot(f2f_ref[...], taps(hf_ref, wf_ref, df_ref, i), preferred_element_type=F32)
         + jnp.dot(f2b, taps(hb_ref, wb_ref, db_ref, (n1 - i) % n1), preferred_element_type=F32))
    ar, ai = p[:n2], p[n2:]
    tr, ti = twr_ref[...], twi_ref[...]
    o_ref[0] = (ar * tr - ai * ti).astype(o_ref.dtype)
    o_ref[1] = (ar * ti + ai * tr).astype(o_ref.dtype)


def _hy_filter_stage1(hid, w_out, deltas, c, tabs):
    n1, n2, _, f2, tw = tabs
    half = n2 // 2
    L = hid.shape[0]
    cols = HY_ORDER * c

    def split(v):
        v4 = v.reshape(v.shape[0], HY_ORDER, 2, c)
        return v4[:, :, 0].reshape(v.shape[0], cols), v4[:, :, 1].reshape(v.shape[0], cols)

    w_f, w_b = split(jnp.zeros((LANES, w_out.shape[1]), BF16).at[:w_out.shape[0]].set(w_out.astype(BF16)))
    d_f, d_b = split(deltas.astype(F32).reshape(1, -1))
    j = np.arange(half)
    f2b = jnp.stack([_stack_ri(f2[:, (n2 - j) % n2]), _stack_ri(f2[:, n2 - 1 - j])])
    hid2d = hid.reshape(half, n1 * LANES)
    twspec = pl.BlockSpec((None, n2, 1), lambda i: (i, 0, 0))
    full = lambda *shape: pl.BlockSpec(shape, lambda i: (0,) * len(shape))
    return pl.pallas_call(
        functools.partial(_hy_filter_stage1_kernel, n1=n1, length=L),
        grid=(n1,),
        in_specs=[pl.BlockSpec((half, LANES), lambda i: (0, i)),
                  pl.BlockSpec((half, LANES), lambda i: (0, (n1 - i) % n1)),
                  full(LANES, cols), full(LANES, cols), full(1, cols), full(1, cols),
                  full(2 * n2, half), full(2, 2 * n2, half), twspec, twspec],
        out_specs=pl.BlockSpec((2, n2, cols), lambda i: (0, 0, i)),
        out_shape=jax.ShapeDtypeStruct((2, n2, n1 * cols), BF16),
        compiler_params=_params("parallel"),
        name="hyena_filter_stage1",
    )(hid2d, hid2d, w_f, w_b, d_f, d_b, _stack_ri(f2[:, :half]), f2b,
      jnp.asarray(tw.real.T.reshape(n1, n2, 1), F32), jnp.asarray(tw.imag.T.reshape(n1, n2, 1), F32)
      ).reshape(2, n2, n1, cols)


def _hy_filter_spectrum_kernel(a_ref, f1_ref, h_ref):
    n1 = h_ref.shape[1]
    f1 = f1_ref[...]
    p = jnp.dot(f1, a_ref[0], preferred_element_type=F32)
    q = jnp.dot(f1, a_ref[1], preferred_element_type=F32)
    h_ref[0] = p[:n1] - q[n1:]
    h_ref[1] = p[n1:] + q[:n1]


def _hy_filter_spectrum(fa, c, tabs):
    n1, n2, f1, _, _ = tabs
    return pl.pallas_call(
        _hy_filter_spectrum_kernel,
        grid=(HY_ORDER, n2),
        in_specs=[pl.BlockSpec((2, None, n1, c), lambda o, k: (0, k, 0, o)),
                  pl.BlockSpec((2 * n1, n1), lambda o, k: (0, 0))],
        out_specs=pl.BlockSpec((None, 2, None, n1, c), lambda o, k: (o, 0, k, 0, 0)),
        out_shape=jax.ShapeDtypeStruct((HY_ORDER, 2, n2, n1, c), F32),
        compiler_params=_params("parallel", "parallel"),
        name="hyena_filter_spectrum",
    )(fa, _stack_ri(f1))


def _hy_stage1_kernel(x_ref, w_ref, twr_ref, twi_ref, o_ref):
    half, r, c = x_ref.shape
    n2 = o_ref.shape[1]
    p = jnp.dot(w_ref[...], x_ref[...].reshape(half * r, c).astype(BF16), preferred_element_type=F32)
    ar, ai = p[:n2 * r], p[n2 * r:]
    tr, ti = twr_ref[...], twi_ref[...]
    o_ref[0] = (ar * tr - ai * ti).reshape(n2, r, c)
    o_ref[1] = (ar * ti + ai * tr).reshape(n2, r, c)


def _hy_stage1(src, part, c, tabs):
    n1, n2, _, f2, tw = tabs
    half = n2 // 2
    r = SUBLANES
    cb = _tile(c, 1024, LANES)
    per = c // cb
    eye = np.eye(r)
    w = np.einsum('pkn,jJ->pkjnJ', np.stack([f2.real[:, :half], f2.imag[:, :half]]), eye)
    w = jnp.asarray(w.reshape(2 * n2 * r, half * r), BF16)

    def blocked(t):
        return jnp.asarray(t.reshape(n2, n1 // r, r).transpose(1, 0, 2).reshape(n1 // r, n2 * r, 1), F32)

    twspec = pl.BlockSpec((None, n2 * r, 1), lambda i, k: (i, 0, 0))
    return pl.pallas_call(
        _hy_stage1_kernel,
        grid=(n1 // r, per),
        in_specs=[pl.BlockSpec((half, r, cb), lambda i, k: (0, i, part * per + k)),
                  pl.BlockSpec((2 * n2 * r, half * r), lambda i, k: (0, 0)), twspec, twspec],
        out_specs=pl.BlockSpec((2, n2, r, cb), lambda i, k: (0, 0, i, k)),
        out_shape=jax.ShapeDtypeStruct((2, n2, n1, c), F32),
        compiler_params=_params("parallel", "parallel"),
        name="hyena_dft_stage1",
    )(src.reshape(half, n1, src.shape[1]), w, blocked(tw.real), blocked(tw.imag))


def _hy_mid_kernel(a_ref, h_ref, f1_ref, twr_ref, twi_ref, o_ref):
    n1 = o_ref.shape[1]
    f1 = f1_ref[...]
    p = jnp.dot(f1, a_ref[0].astype(BF16), preferred_element_type=F32)
    q = jnp.dot(f1, a_ref[1].astype(BF16), preferred_element_type=F32)
    xr = p[:n1] - q[n1:]
    xi = p[n1:] + q[:n1]
    hr, hi = h_ref[0], h_ref[1]
    yr = (xr * hr - xi * hi).astype(BF16)
    yi = (xr * hi + xi * hr).astype(BF16)
    p = jnp.dot(f1, yr, preferred_element_type=F32)
    q = jnp.dot(f1, yi, preferred_element_type=F32)
    br = p[:n1] + q[n1:]
    bi = q[:n1] - p[n1:]
    tr, ti = twr_ref[...], twi_ref[...]
    o_ref[0] = br * tr + bi * ti
    o_ref[1] = bi * tr - br * ti


def _hy_stage3_kernel(b_ref, w_ref, z_ref, gate_ref, skip_ref, o_ref, *, inv_n):
    _, n2, r, c = b_ref.shape
    half = z_ref.shape[0]
    acc = jnp.dot(w_ref[...], b_ref[...].reshape(2 * n2 * r, c).astype(BF16), preferred_element_type=F32)
    z = z_ref[...].reshape(half * r, c)
    gate = gate_ref[...].reshape(half * r, c)
    o_ref[...] = (gate * (acc * inv_n + z * skip_ref[...])).reshape(half, r, c).astype(o_ref.dtype)


def _hy_long_conv(z_src, z_part, gate_src, gate_part, spec, order, skip, tabs, out_dtype):
    n1, n2, f1, f2, tw = tabs
    L = z_src.shape[0]
    c = skip.shape[0]
    half = n2 // 2
    r = SUBLANES
    cb = _tile(c, 1024, LANES)
    per = c // cb
    slab = pl.BlockSpec((2, None, n1, c), lambda k: (0, k, 0, 0))
    a = _hy_stage1(z_src, z_part, c, tabs)
    twspec = pl.BlockSpec((None, n1, 1), lambda k: (k, 0, 0))
    b = pl.pallas_call(
        _hy_mid_kernel,
        grid=(n2,),
        in_specs=[slab, pl.BlockSpec((None, 2, None, n1, c), lambda k: (order, 0, k, 0, 0)),
                  pl.BlockSpec((2 * n1, n1), lambda k: (0, 0)), twspec, twspec],
        out_specs=slab,
        out_shape=jax.ShapeDtypeStruct((2, n2, n1, c), F32),
        compiler_params=_params("parallel"),
        name="hyena_spectral_product",
    )(a, spec, _stack_ri(f1), jnp.asarray(tw.real.reshape(n2, n1, 1), F32),
      jnp.asarray(tw.imag.reshape(n2, n1, 1), F32))
    w3 = np.einsum('pok,jJ->ojpkJ', np.stack([f2.real[:half], f2.imag[:half]]), np.eye(r))
    w3 = jnp.asarray(w3.reshape(half * r, 2 * n2 * r), BF16)
    out = pl.pallas_call(
        functools.partial(_hy_stage3_kernel, inv_n=1.0 / (n1 * n2)),
        grid=(n1 // r, per),
        in_specs=[pl.BlockSpec((2, n2, r, cb), lambda i, k: (0, 0, i, k)),
                  pl.BlockSpec((half * r, 2 * n2 * r), lambda i, k: (0, 0)),
                  pl.BlockSpec((half, r, cb), lambda i, k: (0, i, z_part * per + k)),
                  pl.BlockSpec((half, r, cb), lambda i, k: (0, i, gate_part * per + k)),
                  pl.BlockSpec((1, cb), lambda i, k: (0, k))],
        out_specs=pl.BlockSpec((half, r, cb), lambda i, k: (0, i, k)),
        out_shape=jax.ShapeDtypeStruct((half, n1, c), out_dtype),
        compiler_params=_params("parallel", "parallel"),
        name="hyena_dft_stage3",
    )(b, w3, z_src.reshape(half, n1, z_src.shape[1]), gate_src.reshape(half, n1, gate_src.shape[1]),
      skip.astype(F32).reshape(1, c))
    return out.reshape(L, c)


def _hyena(p, col0, conv_w, conv_b, w_in, b_in, w_mid, b_mid, w_out, freq, deltas, skip):
    assert HY_ORDER == 2
    L = p.shape[0]
    c = skip.shape[1]
    tabs = _dft_tables(2 * L)
    u = _dwconv(p, conv_w, conv_b, (HY_CONV - 1) // 2, col0=col0)
    hid = _hy_hidden(L, w_in, b_in, w_mid, b_mid, freq)
    spec = _hy_filter_spectrum(_hy_filter_stage1(hid, w_out, deltas, c, tabs), c, tabs)
    z = _hy_long_conv(u, 2, u, 0, spec, 0, skip[0], tabs, F32)
    return _hy_long_conv(z, 0, u, 1, spec, 1, skip[1], tabs, BF16)


def _even_layer_mix(h, hc, in_w, conv_w, conv_b, dt_bias, a_log, d_skip, norm_w, rpb):
    d = h.shape[1]
    ssd_w = norm_w.shape[0]
    heads = ssd_w // SSD_P
    e = heads // SSD_G
    na_heads = (d - ssd_w) // NA_DH
    o1 = ssd_w
    o2 = o1 + conv_w.shape[1]
    o3 = o2 + 2 * heads
    w_qkv = in_w[:, o3:].astype(BF16)
    src = np.arange(2 * heads)
    direction, head = src // heads, src % heads
    place = np.zeros((2 * heads, SSD_G * LANES), np.float32)
    place[src, (head // e) * LANES + direction * e + head % e] = 1.0
    w_dt = jnp.dot(in_w[:, o2:o3].astype(F32), jnp.asarray(place), precision=lax.Precision.HIGHEST).astype(BF16)

    def grouped(v):
        vg = v.astype(F32).reshape(2, SSD_G, e).transpose(1, 0, 2).reshape(SSD_G, 2 * e)
        return jnp.zeros((SSD_G, LANES), F32).at[:, :2 * e].set(vg).reshape(SSD_G, 1, LANES)

    bias_g = grouped(dt_bias)
    a_g = grouped(-jnp.exp(a_log.astype(F32)))
    d_cols = jnp.repeat(d_skip.astype(F32), SSD_P)

    def ssd(t, s0):
        zx = _mm(t, in_w, n=o2)
        xbc = _dwconv(zx, conv_w, conv_b, SSD_CONV // 2, col0=o1, silu=True)
        y_f, y_b, s_t = _ssd_scan(xbc, _mm(t, w_dt), bias_g, a_g, s0, ssd_w=ssd_w, groups=SSD_G)
        return _ssd_gate_norm(y_f, y_b, xbc, zx, d_cols, norm_w, groups=SSD_G), s_t

    zero = jnp.zeros((2, SSD_G, e * SSD_P, SSD_N), F32)
    qkv_c = _mm(hc, w_qkv, out_dtype=BF16)
    y_ssd_c, s_c = ssd(hc, zero)
    qkv = _mm(h, w_qkv, out_dtype=BF16)
    y_ssd, _ = ssd(h, s_c)
    y_na = _neighbourhood_attention(qkv, qkv_c, rpb, heads=na_heads)
    y_na_c = _context_attention(qkv_c, heads=na_heads)
    return (y_ssd, y_na), (y_ssd_c, y_na_c)


def _odd_layer_mix(h, hc, in_w, conv_w, conv_b, gate_w, gate_b, lam, hy_params):
    rg_w = lam.shape[1]
    pad = RG_CONV // 2
    u_c = _dwconv(_mm(hc, in_w, col0=rg_w, n=rg_w), conv_w, conv_b, pad)
    _, h_c = _rglru(u_c, None, gate_w, gate_b, lam, jnp.zeros((2, rg_w), F32))
    proj = _mm(h, in_w)
    u = _dwconv(proj, conv_w, conv_b, pad, col0=rg_w)
    y_rg, _ = _rglru(u, proj, gate_w, gate_b, lam, h_c)
    return y_rg, _hyena(proj, 2 * rg_w, *hy_params)


def kernel(x, c, ctx, c_ctx, ada_w, ada_b, norm_mix_w, norm_ffn_w, norm_out_w, ev_in_w, ev_ssd_conv_w, ev_ssd_conv_b, ev_ssd_dt_bias, ev_ssd_a_log, ev_ssd_d, ev_ssd_norm_w, ev_na_rpb, ev_out_w, ev_ffn_w1, ev_ffn_w3, ev_ffn_w2, od_in_w, od_rg_conv_w, od_rg_conv_b, od_rg_gate_w, od_rg_gate_b, od_rg_lambda, od_hy_conv_w, od_hy_conv_b, od_hy_w_in, od_hy_b_in, od_hy_w_mid, od_hy_b_mid, od_hy_w_out, od_hy_freq, od_hy_deltas, od_hy_skip, od_out_w, od_router_w, od_moe_w1, od_moe_w3, od_moe_w2):
    batch, seq, d = x.shape
    assert batch == 1 and ada_w.shape[0] == 2, "kernel is written for one sequence and the even/odd layer pair"
    xs = x[0]
    cs = ctx[0]
    cvecs = jnp.concatenate([jax.nn.silu(c), jax.nn.silu(c_ctx)[None]], axis=0)
    mods = _ada(cvecs, ada_w, ada_b).reshape(2, 2, ADA_CHUNKS, d)

    mod, mod_c = mods[0, 0], mods[0, 1]
    h = _mod_norm(xs, norm_mix_w[0], mod[0], mod[1])
    hc = _mod_norm(cs, norm_mix_w[0], mod_c[0], mod_c[1])
    y, y_c = _even_layer_mix(h, hc, ev_in_w[0], ev_ssd_conv_w[0], ev_ssd_conv_b[0], ev_ssd_dt_bias[0],
                             ev_ssd_a_log[0], ev_ssd_d[0], ev_ssd_norm_w[0], ev_na_rpb[0])
    xs = _mm_residual2(*y, ev_out_w[0], xs, mod[2])
    cs = _mm_residual2(*y_c, ev_out_w[0], cs, mod_c[2])
    w1, w3, w2 = ev_ffn_w1[0], ev_ffn_w3[0], ev_ffn_w2[0].astype(BF16)
    h2 = _mod_norm(xs, norm_ffn_w[0], mod[3], mod[4])
    xs = _mm_residual(_mm_swiglu(h2, w1, w3), w2, xs, mod[5], tm=256, tn=512)
    h2c = _mod_norm(cs, norm_ffn_w[0], mod_c[3], mod_c[4])
    cs = _mm_residual(_mm_swiglu(h2c, w1, w3), w2, cs, mod_c[5], tm=256, tn=512)

    mod, mod_c = mods[1, 0], mods[1, 1]
    h = _mod_norm(xs, norm_mix_w[1], mod[0], mod[1])
    hc = _mod_norm(cs, norm_mix_w[1], mod_c[0], mod_c[1])
    hy_params = (od_hy_conv_w[0], od_hy_conv_b[0], od_hy_w_in[0], od_hy_b_in[0], od_hy_w_mid[0],
                 od_hy_b_mid[0], od_hy_w_out[0], od_hy_freq[0], od_hy_deltas[0], od_hy_skip[0])
    y = _odd_layer_mix(h, hc, od_in_w[0], od_rg_conv_w[0], od_rg_conv_b[0], od_rg_gate_w[0],
                       od_rg_gate_b[0], od_rg_lambda[0], hy_params)
    xs = _mm_residual2(*y, od_out_w[0], xs, mod[2])
    h2, route = _mod_norm_router(xs, norm_ffn_w[1], mod[3], mod[4], od_router_w[0])
    y1, y2 = _moe(h2, route, od_moe_w1[0], od_moe_w3[0], od_moe_w2[0])
    return _final_norm(xs, y1, y2, mod[5], norm_out_w)[None]
```

```python
import functools
import math

import numpy as np
import jax
import jax.numpy as jnp
from jax import lax
from jax.experimental import pallas as pl
from jax.experimental.pallas import tpu as pltpu

GRID_W = 64
ADA_CHUNKS = 6
NORM_EPS = 1e-6
SSD_P = 64
SSD_G = 4
SSD_N = 128
SSD_CONV = 4
SSD_CHUNK = 128
NA_DH = 128
NA_WIN_H = 8
NA_WIN_W = 16
NA_QCOLS = 16
NA_KCOLS = NA_QCOLS + NA_WIN_W
RG_BLOCKS = 16
RG_CONV = 4
RG_C = 8.0
HY_ORDER = 2
HY_CONV = 3
HY_EMB = 33
HY_BANDS = (HY_EMB - 1) // 2
HY_INNER = 2
TOP_K = 2

LANES = 128
SUBLANES = 8
VMEM_LIMIT_BYTES = 56 * 1024 * 1024
MOE_TILE_ROWS = 512

F32 = jnp.float32
BF16 = jnp.bfloat16


def _tile(n, target, align):
    for t in range(min(n, target), 0, -1):
        if n % t == 0 and t % align == 0:
            return t
    return n


def _params(*semantics):
    return pltpu.CompilerParams(dimension_semantics=semantics, vmem_limit_bytes=VMEM_LIMIT_BYTES)


def _ada_kernel(cb_ref, w_ref, b_ref, o_ref, *, n_rows):
    k, tn = w_ref.shape
    for l in range(tn // LANES):
        w = w_ref[:, l * LANES:(l + 1) * LANES].reshape(k // SUBLANES, SUBLANES, LANES)
        for r in range(n_rows):
            cb = cb_ref[r].reshape(k // SUBLANES, SUBLANES, LANES)
            part = jnp.sum(w * cb, axis=0)
            o_ref[r:r + 1, l * LANES:(l + 1) * LANES] = (
                jnp.sum(part, axis=0, keepdims=True) + b_ref[:, l * LANES:(l + 1) * LANES])


def _ada(cvecs, w, b):
    r, k = cvecs.shape
    layers, _, n = w.shape
    tn = _tile(n, 512, LANES)
    cb = jnp.broadcast_to(cvecs[:, :, None], (r, k, LANES))
    return pl.pallas_call(
        functools.partial(_ada_kernel, n_rows=r),
        grid=(layers, n // tn),
        in_specs=[pl.BlockSpec((r, k, LANES), lambda l, j: (0, 0, 0)),
                  pl.BlockSpec((None, k, tn), lambda l, j: (l, 0, j)),
                  pl.BlockSpec((None, 1, tn), lambda l, j: (l, 0, j))],
        out_specs=pl.BlockSpec((None, r, tn), lambda l, j: (l, 0, j)),
        out_shape=jax.ShapeDtypeStruct((layers, r, n), F32),
        compiler_params=_params("parallel", "parallel"),
        name="ada_matvec",
    )(cb, w, b.reshape(layers, 1, n))


def _norm_rows(x, w, shift, scale):
    y = x * lax.rsqrt(jnp.mean(x * x, axis=-1, keepdims=True) + NORM_EPS)
    y = y * w
    if scale is not None:
        y = y * (1.0 + scale) + shift
    return y


def _norm_kernel(x_ref, w_ref, sh_ref, sc_ref, o_ref):
    o_ref[...] = _norm_rows(x_ref[...], w_ref[...], sh_ref[...], sc_ref[...]).astype(o_ref.dtype)


def _final_norm_kernel(x_ref, y1_ref, y2_ref, gate_ref, w_ref, o_ref):
    x = x_ref[...] + gate_ref[...] * (y1_ref[...] + y2_ref[...])
    o_ref[...] = _norm_rows(x, w_ref[...], None, None).astype(o_ref.dtype)


def _norm_router_kernel(x_ref, w_ref, sh_ref, sc_ref, rw_ref, o_ref, route_ref, *, n_experts):
    h = _norm_rows(x_ref[...], w_ref[...], sh_ref[...], sc_ref[...])
    o_ref[...] = h.astype(o_ref.dtype)
    logits = jnp.dot(h, rw_ref[...], preferred_element_type=F32, precision=lax.Precision.HIGHEST)
    col = lax.broadcasted_iota(jnp.int32, logits.shape, 1)
    neg = jnp.float32(-jnp.inf)
    l1 = jnp.where(col < n_experts, logits, neg)
    m1 = jnp.max(l1, axis=-1, keepdims=True)
    i1 = jnp.min(jnp.where(l1 == m1, col, LANES), axis=-1, keepdims=True)
    l2 = jnp.where(col == i1, neg, l1)
    m2 = jnp.max(l2, axis=-1, keepdims=True)
    i2 = jnp.min(jnp.where(l2 == m2, col, LANES), axis=-1, keepdims=True)
    e2 = jnp.exp(m2 - m1)
    g1 = 1.0 / (1.0 + e2)
    g2 = e2 * g1
    route = jnp.where(col == 0, i1.astype(F32),
                      jnp.where(col == 1, i2.astype(F32),
                                jnp.where(col == 2, g1, jnp.where(col == 3, g2, 0.0))))
    route_ref[...] = route


def _mod_norm(x, w, shift, scale, out_dtype=BF16):
    m, d = x.shape
    tr = _tile(m, 256, SUBLANES)
    vec = pl.BlockSpec((1, d), lambda i: (0, 0))
    return pl.pallas_call(
        _norm_kernel,
        grid=(m // tr,),
        in_specs=[pl.BlockSpec((tr, d), lambda i: (i, 0)), vec, vec, vec],
        out_specs=pl.BlockSpec((tr, d), lambda i: (i, 0)),
        out_shape=jax.ShapeDtypeStruct((m, d), out_dtype),
        compiler_params=_params("parallel"),
        name="mod_norm",
    )(x, w.reshape(1, d), shift.reshape(1, d), scale.reshape(1, d))


def _final_norm(x, y1, y2, gate, w):
    m, d = x.shape
    tr = _tile(m, 256, SUBLANES)
    rows = pl.BlockSpec((tr, d), lambda i: (i, 0))
    vec = pl.BlockSpec((1, d), lambda i: (0, 0))
    return pl.pallas_call(
        _final_norm_kernel,
        grid=(m // tr,),
        in_specs=[rows, rows, rows, vec, vec],
        out_specs=rows,
        out_shape=jax.ShapeDtypeStruct((m, d), x.dtype),
        compiler_params=_params("parallel"),
        name="final_norm",
    )(x, y1, y2, gate.reshape(1, d), w.reshape(1, d))


def _mod_norm_router(x, w, shift, scale, router_w):
    m, d = x.shape
    n_experts = router_w.shape[1]
    tr = _tile(m, 256, SUBLANES)
    rw = jnp.zeros((d, LANES), F32).at[:, :n_experts].set(router_w.astype(F32))
    vec = pl.BlockSpec((1, d), lambda i: (0, 0))
    return pl.pallas_call(
        functools.partial(_norm_router_kernel, n_experts=n_experts),
        grid=(m // tr,),
        in_specs=[pl.BlockSpec((tr, d), lambda i: (i, 0)), vec, vec, vec,
                  pl.BlockSpec((d, LANES), lambda i: (0, 0))],
        out_specs=[pl.BlockSpec((tr, d), lambda i: (i, 0)),
                   pl.BlockSpec((tr, LANES), lambda i: (i, 0))],
        out_shape=[jax.ShapeDtypeStruct((m, d), BF16), jax.ShapeDtypeStruct((m, LANES), F32)],
        compiler_params=_params("parallel"),
        name="mod_norm_router",
    )(x, w.reshape(1, d), shift.reshape(1, d), scale.reshape(1, d), rw)


def _bf16(ref):
    return ref[...].astype(BF16)


def _mm_kernel(a_ref, b_ref, o_ref):
    o_ref[...] = jnp.dot(a_ref[...], _bf16(b_ref), preferred_element_type=F32).astype(o_ref.dtype)


def _mm_residual_kernel(a_ref, b_ref, res_ref, gate_ref, o_ref):
    acc = jnp.dot(a_ref[...], _bf16(b_ref), preferred_element_type=F32)
    o_ref[...] = res_ref[...] + gate_ref[...] * acc


def _mm_swiglu_kernel(a_ref, b1_ref, b3_ref, o_ref):
    a = a_ref[...]
    g = jnp.dot(a, _bf16(b1_ref), preferred_element_type=F32)
    u = jnp.dot(a, _bf16(b3_ref), preferred_element_type=F32)
    o_ref[...] = (g * jax.nn.sigmoid(g) * u).astype(o_ref.dtype)


def _mm_tiles(m, k, n, tm, tn):
    tm = _tile(m, tm, SUBLANES)
    tn = _tile(n, tn, LANES)
    return tm, tn


def _mm(a, b, *, out_dtype=F32, tm=1024, tn=512, col0=0, n=None):
    m, k = a.shape
    n = b.shape[1] - col0 if n is None else n
    tm, tn = _mm_tiles(m, k, math.gcd(n, col0) if col0 else n, tm, tn)
    off = col0 // tn
    return pl.pallas_call(
        _mm_kernel,
        grid=(n // tn, m // tm),
        in_specs=[pl.BlockSpec((tm, k), lambda j, i: (i, 0)),
                  pl.BlockSpec((k, tn), lambda j, i: (0, j + off))],
        out_specs=pl.BlockSpec((tm, tn), lambda j, i: (i, j)),
        out_shape=jax.ShapeDtypeStruct((m, n), out_dtype),
        compiler_params=_params("parallel", "parallel"),
        name="matmul",
    )(a, b)


def _mm_residual(a, b, res, gate, *, tm=512, tn=1024):
    m, k = a.shape
    n = b.shape[1]
    tm, tn = _mm_tiles(m, k, n, tm, tn)
    return pl.pallas_call(
        _mm_residual_kernel,
        grid=(n // tn, m // tm),
        in_specs=[pl.BlockSpec((tm, k), lambda j, i: (i, 0)),
                  pl.BlockSpec((k, tn), lambda j, i: (0, j)),
                  pl.BlockSpec((tm, tn), lambda j, i: (i, j)),
                  pl.BlockSpec((1, tn), lambda j, i: (0, j))],
        out_specs=pl.BlockSpec((tm, tn), lambda j, i: (i, j)),
        out_shape=jax.ShapeDtypeStruct((m, n), F32),
        compiler_params=_params("parallel", "parallel"),
        name="matmul_residual",
    )(a, b, res, gate.reshape(1, n))


def _mm_residual2_kernel(a1_ref, a2_ref, b_ref, res_ref, gate_ref, o_ref):
    k1 = a1_ref.shape[1]
    acc = jnp.dot(a1_ref[...], b_ref[:k1, :].astype(BF16), preferred_element_type=F32)
    acc = acc + jnp.dot(a2_ref[...], b_ref[k1:, :].astype(BF16), preferred_element_type=F32)
    o_ref[...] = res_ref[...] + gate_ref[...] * acc


def _mm_residual2(a1, a2, b, res, gate, *, tm=1024, tn=512):
    m, k1 = a1.shape
    k2 = a2.shape[1]
    n = b.shape[1]
    tm, tn = _mm_tiles(m, k1 + k2, n, tm, tn)
    return pl.pallas_call(
        _mm_residual2_kernel,
        grid=(n // tn, m // tm),
        in_specs=[pl.BlockSpec((tm, k1), lambda j, i: (i, 0)),
                  pl.BlockSpec((tm, k2), lambda j, i: (i, 0)),
                  pl.BlockSpec((k1 + k2, tn), lambda j, i: (0, j)),
                  pl.BlockSpec((tm, tn), lambda j, i: (i, j)),
                  pl.BlockSpec((1, tn), lambda j, i: (0, j))],
        out_specs=pl.BlockSpec((tm, tn), lambda j, i: (i, j)),
        out_shape=jax.ShapeDtypeStruct((m, n), F32),
        compiler_params=_params("parallel", "parallel"),
        name="matmul_residual2",
    )(a1, a2, b, res, gate.reshape(1, n))


def _mm_swiglu(a, b1, b3, *, tm=1024, tn=256):
    m, k = a.shape
    n = b1.shape[1]
    tm, tn = _mm_tiles(m, k, n, tm, tn)
    return pl.pallas_call(
        _mm_swiglu_kernel,
        grid=(m // tm, n // tn),
        in_specs=[pl.BlockSpec((tm, k), lambda i, j: (i, 0)),
                  pl.BlockSpec((k, tn), lambda i, j: (0, j)),
                  pl.BlockSpec((k, tn), lambda i, j: (0, j))],
        out_specs=pl.BlockSpec((tm, tn), lambda i, j: (i, j)),
        out_shape=jax.ShapeDtypeStruct((m, n), BF16),
        compiler_params=_params("parallel", "parallel"),
        name="matmul_swiglu",
    )(a, b1, b3)


def _moe_up_kernel(te_ref, nt_ref, a_ref, b1_ref, b3_ref, o_ref):
    @pl.when(pl.program_id(1) < nt_ref[0])
    def _():
        a = a_ref[...]
        g = jnp.dot(a, _bf16(b1_ref), preferred_element_type=F32)
        u = jnp.dot(a, _bf16(b3_ref), preferred_element_type=F32)
        o_ref[...] = (g * jax.nn.sigmoid(g) * u).astype(o_ref.dtype)


def _moe_down_kernel(te_ref, nt_ref, a_ref, b_ref, gate_ref, o_ref):
    @pl.when(pl.program_id(1) < nt_ref[0])
    def _():
        acc = jnp.dot(a_ref[...], _bf16(b_ref), preferred_element_type=F32)
        o_ref[...] = gate_ref[...] * acc


def _moe_grouped(xg, gate_slot, tile_expert, n_tiles, w1, w3, w2, *, tm):
    s, d = xg.shape
    f = w1.shape[2]
    n_t = s // tm
    tf = _tile(f, 512, LANES)
    td = _tile(d, 1024, LANES)

    def row(i, nt):
        return jnp.minimum(i, nt[0] - 1)

    up = pl.pallas_call(
        _moe_up_kernel,
        grid_spec=pltpu.PrefetchScalarGridSpec(
            num_scalar_prefetch=2,
            grid=(f // tf, n_t),
            in_specs=[
                pl.BlockSpec((tm, d), lambda j, i, te, nt: (row(i, nt), 0)),
                pl.BlockSpec((None, d, tf), lambda j, i, te, nt: (te[row(i, nt)], 0, j)),
                pl.BlockSpec((None, d, tf), lambda j, i, te, nt: (te[row(i, nt)], 0, j)),
            ],
            out_specs=pl.BlockSpec((tm, tf), lambda j, i, te, nt: (row(i, nt), j)),
        ),
        out_shape=jax.ShapeDtypeStruct((s, f), BF16),
        compiler_params=_params("arbitrary", "arbitrary"),
        name="moe_up",
    )(tile_expert, n_tiles, xg, w1, w3)

    return pl.pallas_call(
        _moe_down_kernel,
        grid_spec=pltpu.PrefetchScalarGridSpec(
            num_scalar_prefetch=2,
            grid=(d // td, n_t),
            in_specs=[
                pl.BlockSpec((tm, f), lambda j, i, te, nt: (row(i, nt), 0)),
                pl.BlockSpec((None, f, td), lambda j, i, te, nt: (te[row(i, nt)], 0, j)),
                pl.BlockSpec((tm, 1), lambda j, i, te, nt: (row(i, nt), 0)),
            ],
            out_specs=pl.BlockSpec((tm, td), lambda j, i, te, nt: (row(i, nt), j)),
        ),
        out_shape=jax.ShapeDtypeStruct((s, d), F32),
        compiler_params=_params("arbitrary", "arbitrary"),
        name="moe_down",
    )(tile_expert, n_tiles, up, w2, gate_slot)


def _moe(h2, route, w1, w3, w2):
    n, d = h2.shape
    n_e = w1.shape[0]
    tm = min(MOE_TILE_ROWS, n)
    n_t = -(-(TOP_K * n + n_e * (tm - 1)) // tm)
    s = n_t * tm
    expert = route[:, :TOP_K].astype(jnp.int32).T.reshape(-1)
    gate = route[:, TOP_K:2 * TOP_K].T.reshape(-1)
    token = jnp.tile(jnp.arange(n, dtype=jnp.int32), TOP_K)
    onehot = (expert[:, None] == jnp.arange(n_e, dtype=jnp.int32)[None]).astype(jnp.int32)
    csum = jnp.cumsum(onehot, axis=0)
    rank = jnp.sum(onehot * csum, axis=1) - 1
    counts = csum[-1]
    padded = -(-counts // tm) * tm
    ends = jnp.cumsum(padded)
    starts = ends - padded
    slot = starts[expert] + rank
    token_of_slot = jnp.zeros((s,), jnp.int32).at[slot].set(token)
    gate_of_slot = jnp.zeros((s,), F32).at[slot].set(gate)
    n_tiles = (ends[-1] // tm).astype(jnp.int32).reshape(1)
    tile_start = jnp.arange(n_t, dtype=jnp.int32) * tm
    tile_expert = jnp.minimum(jnp.sum((tile_start[:, None] >= ends[None]).astype(jnp.int32), axis=1), n_e - 1)
    xg = h2.at[token_of_slot].get(mode="promise_in_bounds")
    yo = _moe_grouped(xg, gate_of_slot.reshape(s, 1), tile_expert, n_tiles, w1, w3, w2, tm=tm)
    slots = slot.reshape(TOP_K, n)
    return [yo.at[slots[kk]].get(mode="promise_in_bounds") for kk in range(TOP_K)]


def _dwconv_kernel(prev_ref, cur_ref, next_ref, w_ref, b_ref, o_ref, *, pad_left, silu):
    i = pl.program_id(1)
    cur = cur_ref[...]
    rows = cur.shape[0]
    prev = jnp.where(i > 0, prev_ref[...], 0.0)
    nxt = jnp.where(i < pl.num_programs(1) - 1, next_ref[...], 0.0)
    ext = jnp.concatenate([prev, cur, nxt], axis=0)
    acc = jnp.zeros_like(cur) + b_ref[...]
    for j in range(w_ref.shape[0]):
        s = j - pad_left
        us = cur if s == 0 else pltpu.roll(ext, (-s) % ext.shape[0], axis=0)[SUBLANES:SUBLANES + rows]
        acc = acc + w_ref[j:j + 1, :] * us
    if silu:
        acc = acc * jax.nn.sigmoid(acc)
    o_ref[...] = acc


def _dwconv(u, w, b, pad_left, *, col0=0, silu=False):
    n = u.shape[0]
    k, c = w.shape
    assert k - 1 <= SUBLANES
    tc = _tile(math.gcd(c, col0) if col0 else c, 512, LANES)
    off = col0 // tc
    tr = _tile(n, 1024, SUBLANES)
    per = tr // SUBLANES
    last = n // SUBLANES - 1
    return pl.pallas_call(
        functools.partial(_dwconv_kernel, pad_left=pad_left, silu=silu),
        grid=(c // tc, n // tr),
        in_specs=[pl.BlockSpec((SUBLANES, tc), lambda j, i: (jnp.maximum(i * per - 1, 0), j + off)),
                  pl.BlockSpec((tr, tc), lambda j, i: (i, j + off)),
                  pl.BlockSpec((SUBLANES, tc), lambda j, i: (jnp.minimum((i + 1) * per, last), j + off)),
                  pl.BlockSpec((k, tc), lambda j, i: (0, j)),
                  pl.BlockSpec((1, tc), lambda j, i: (0, j))],
        out_specs=pl.BlockSpec((tr, tc), lambda j, i: (i, j)),
        out_shape=jax.ShapeDtypeStruct((n, c), F32),
        compiler_params=_params("parallel", "parallel"),
        name="dwconv",
    )(u, u, u, w.astype(F32), b.astype(F32).reshape(1, c))


def _ssd_kernel(xf_ref, bf_ref, cf_ref, dtf_ref, xb_ref, bb_ref, cb_ref, dtb_ref, bias_ref, a_ref, s0_ref,
                yf_ref, yb_ref, st_ref, state, *, heads_per_group, head_dim):
    step = pl.program_id(1)

    @pl.when(step == 0)
    def _():
        state[...] = s0_ref[...]

    q, gw = xf_ref.shape
    hp = lax.Precision.HIGHEST
    nt = (((1,), (1,)), ((), ()))
    row = lax.broadcasted_iota(jnp.int32, (q, q), 0)
    col = lax.broadcasted_iota(jnp.int32, (q, q), 1)
    tri = (row >= col).astype(F32)
    neg = jnp.float32(-jnp.inf)
    head_of_col = lax.broadcasted_iota(jnp.int32, (LANES, gw), 1) // head_dim
    lane_id = lax.broadcasted_iota(jnp.int32, (LANES, gw), 0)
    low_half = lax.broadcasted_iota(jnp.int32, (q, LANES), 1) < head_dim
    heads_per_tile = LANES // head_dim
    streams = ((xf_ref, bf_ref, cf_ref, dtf_ref, yf_ref), (xb_ref, bb_ref, cb_ref, dtb_ref, yb_ref))
    for d, (x_ref, b_ref, c_ref, dt_ref, y_ref) in enumerate(streams):
        xs = x_ref[...]
        bm = b_ref[...].astype(BF16)
        cm = c_ref[...].astype(BF16)
        dt = jax.nn.softplus(dt_ref[...] + bias_ref[...])
        da = dt * a_ref[...]
        cum = jnp.dot(tri, da, preferred_element_type=F32, precision=hp)
        total = cum[q - 1:q, :]
        pos = cum if d == 0 else cum - da
        pos_t = pos.T
        if d == 0:
            carry_in, carry_out = jnp.exp(pos), jnp.exp(total - pos)
        else:
            carry_in, carry_out = jnp.exp(total - pos), jnp.exp(pos)
        spread = (lane_id == head_of_col + d * heads_per_group).astype(F32)
        xd = xs * jnp.dot(dt, spread, preferred_element_type=F32, precision=hp)
        st = state[d]
        y_off = (lax.dot_general(cm, st.astype(BF16), nt, preferred_element_type=F32)
                 * jnp.dot(carry_in, spread, preferred_element_type=F32, precision=hp))
        xd_out = (xd * jnp.dot(carry_out, spread, preferred_element_type=F32, precision=hp)).astype(BF16)
        upd = lax.dot_general(xd_out, bm, (((0,), (0,)), ((), ())), preferred_element_type=F32)
        keep = lax.dot_general(spread, jnp.exp(total), (((0,), (1,)), ((), ())),
                               preferred_element_type=F32, precision=hp)
        state[d] = keep * st + upd
        cb = lax.dot_general(cm, bm, nt, preferred_element_type=F32)
        mask = (row >= col) if d == 0 else (col >= row)
        xdb = xd.astype(BF16)
        for tile in range(gw // LANES):
            cols = slice(tile * LANES, (tile + 1) * LANES)
            parts = []
            for e in range(tile * heads_per_tile, (tile + 1) * heads_per_tile):
                lane = d * heads_per_group + e
                p_col = pos[:, lane:lane + 1]
                p_row = pos_t[lane:lane + 1, :]
                expo = (p_col - p_row) if d == 0 else (p_row - p_col)
                dec = jnp.exp(jnp.where(mask, expo, neg))
                parts.append(jnp.dot((cb * dec).astype(BF16), xdb[:, cols], preferred_element_type=F32))
            y_diag = parts[0] if len(parts) == 1 else jnp.where(low_half, parts[0], parts[1])
            y_ref[:, cols] = y_diag + y_off[:, cols]

    @pl.when(step == pl.num_programs(1) - 1)
    def _():
        st_ref[...] = state[...]


def _ssd_scan(xbc, dtg, bias_g, a_g, s0, *, ssd_w, groups):
    n = xbc.shape[0]
    e = ssd_w // (SSD_P * groups)
    q = min(SSD_CHUNK, n)
    steps = n // q
    gw = e * SSD_P
    nb = ssd_w // SSD_N
    assert LANES % SSD_P == 0 and LANES // SSD_P <= 2 and gw % LANES == 0 and 2 * e <= LANES

    def fwd(blk):
        return lambda g, s: (s, blk(g))

    def bwd(blk):
        return lambda g, s: (steps - 1 - s, blk(g))

    def stream(order):
        return [pl.BlockSpec((q, gw), order(lambda g: g)),
                pl.BlockSpec((q, SSD_N), order(lambda g: nb + g)),
                pl.BlockSpec((q, SSD_N), order(lambda g: nb + groups + g))]

    grp = pl.BlockSpec((None, 1, LANES), lambda g, s: (g, 0, 0))
    st_spec = pl.BlockSpec((2, None, gw, SSD_N), lambda g, s: (0, g, 0, 0))
    return pl.pallas_call(
        functools.partial(_ssd_kernel, heads_per_group=e, head_dim=SSD_P),
        grid=(groups, steps),
        in_specs=(stream(fwd) + [pl.BlockSpec((q, LANES), fwd(lambda g: g))]
                  + stream(bwd) + [pl.BlockSpec((q, LANES), bwd(lambda g: g))]
                  + [grp, grp, st_spec]),
        out_specs=[pl.BlockSpec((q, gw), fwd(lambda g: g)), pl.BlockSpec((q, gw), bwd(lambda g: g)), st_spec],
        out_shape=[jax.ShapeDtypeStruct((n, ssd_w), F32), jax.ShapeDtypeStruct((n, ssd_w), F32),
                   jax.ShapeDtypeStruct(s0.shape, F32)],
        scratch_shapes=[pltpu.VMEM((2, gw, SSD_N), F32)],
        compiler_params=_params("parallel", "arbitrary"),
        name="ssd_scan",
    )(xbc, xbc, xbc, dtg, xbc, xbc, xbc, dtg, bias_g, a_g, s0)


def _ssd_gate_norm_kernel(yf_ref, yb_ref, xs_ref, z_ref, d_ref, nw_ref, o_ref, *, groups):
    y = yf_ref[...] + yb_ref[...] + xs_ref[...] * d_ref[...]
    z = z_ref[...]
    y = y * (z * jax.nn.sigmoid(z))
    gw = y.shape[1] // groups
    for g in range(groups):
        yg = y[:, g * gw:(g + 1) * gw]
        yg = yg * lax.rsqrt(jnp.mean(yg * yg, axis=-1, keepdims=True) + NORM_EPS)
        o_ref[:, g * gw:(g + 1) * gw] = (yg * nw_ref[:, g * gw:(g + 1) * gw]).astype(o_ref.dtype)


def _ssd_gate_norm(y_f, y_b, xbc, zx, d_cols, norm_w, *, groups):
    n, w = y_f.shape
    tr = _tile(n, 256, SUBLANES)
    rows = pl.BlockSpec((tr, w), lambda i: (i, 0))
    vec = pl.BlockSpec((1, w), lambda i: (0, 0))
    return pl.pallas_call(
        functools.partial(_ssd_gate_norm_kernel, groups=groups),
        grid=(n // tr,),
        in_specs=[rows, rows, rows, rows, vec, vec],
        out_specs=rows,
        out_shape=jax.ShapeDtypeStruct((n, w), BF16),
        compiler_params=_params("parallel"),
        name="ssd_gate_norm",
    )(y_f, y_b, xbc, zx, d_cols.reshape(1, w), norm_w.astype(F32).reshape(1, w))


def _na_kernel(q_ref, k_ref, v_ref, kc_ref, vc_ref, bias_ref, o_ref, *, grid_rows, win_h, rows_per_block, scale):
    blk = pl.program_id(1)
    kc = kc_ref[...]
    vc = vc_ref[...]
    nt = (((1,), (1,)), ((), ()))

    def one_row(i, carry):
        r = blk * rows_per_block + i
        rs = jnp.clip(r - win_h // 2, 0, grid_rows - win_h)
        q = q_ref[pl.ds(pl.multiple_of(i * GRID_W, GRID_W), GRID_W), :]
        start = pl.multiple_of(rs * GRID_W, GRID_W)
        kw = k_ref[pl.ds(start, win_h * GRID_W), :]
        vw = v_ref[pl.ds(start, win_h * GRID_W), :]
        s = lax.dot_general(q, kw, nt, preferred_element_type=F32) * scale + bias_ref[r - rs]
        sc = lax.dot_general(q, kc, nt, preferred_element_type=F32) * scale
        m = jnp.maximum(jnp.max(s, axis=-1, keepdims=True), jnp.max(sc, axis=-1, keepdims=True))
        p = jnp.exp(s - m)
        pc = jnp.exp(sc - m)
        denom = jnp.sum(p, axis=-1, keepdims=True) + jnp.sum(pc, axis=-1, keepdims=True)
        o = (jnp.dot(p.astype(BF16), vw, preferred_element_type=F32)
             + jnp.dot(pc.astype(BF16), vc, preferred_element_type=F32))
        o_ref[pl.ds(pl.multiple_of(i * GRID_W, GRID_W), GRID_W), :] = (o / denom).astype(o_ref.dtype)
        return carry

    lax.fori_loop(0, rows_per_block, one_row, 0, unroll=True)


def _na_bias_table(rpb, win_h):
    n_dcol = 2 * NA_WIN_W - 1
    p = np.arange(win_h)[:, None]
    a = np.arange(win_h)[None, :]
    rows = rpb.astype(F32)[:, a - p + NA_WIN_H - 1, :]
    qc = np.arange(GRID_W)[:, None]
    kc = np.arange(GRID_W)[None, :]
    dcol = np.clip(kc - qc + NA_WIN_W - 1, 0, n_dcol - 1)
    onehot = (np.arange(n_dcol)[:, None, None] == dcol[None]).astype(np.float32)
    bias = jnp.einsum('hpad,dqk->hpqak', rows, jnp.asarray(onehot), precision=lax.Precision.HIGHEST)
    cstart = np.clip(qc - NA_WIN_W // 2, 0, GRID_W - NA_WIN_W)
    ok = (kc >= cstart) & (kc < cstart + NA_WIN_W)
    bias = jnp.where(ok[None, None, :, None, :], bias, -jnp.inf)
    return bias.reshape(rpb.shape[0], win_h, GRID_W, win_h * GRID_W)


def _neighbourhood_attention(qkv, qkv_c, rpb, *, heads):
    n = qkv.shape[0]
    c = qkv_c.shape[0]
    grid_rows = n // GRID_W
    win_h = min(NA_WIN_H, grid_rows)
    rows_per_block = _tile(grid_rows, 8, 1)
    tq = rows_per_block * GRID_W
    bias = _na_bias_table(rpb, win_h)
    return pl.pallas_call(
        functools.partial(_na_kernel, grid_rows=grid_rows, win_h=win_h, rows_per_block=rows_per_block,
                          scale=NA_DH ** -0.5),
        grid=(heads, n // tq),
        in_specs=[pl.BlockSpec((tq, NA_DH), lambda h, i: (i, h)),
                  pl.BlockSpec((n, NA_DH), lambda h, i: (0, heads + h)),
                  pl.BlockSpec((n, NA_DH), lambda h, i: (0, 2 * heads + h)),
                  pl.BlockSpec((c, NA_DH), lambda h, i: (0, heads + h)),
                  pl.BlockSpec((c, NA_DH), lambda h, i: (0, 2 * heads + h)),
                  pl.BlockSpec((None, win_h, GRID_W, win_h * GRID_W), lambda h, i: (h, 0, 0, 0))],
        out_specs=pl.BlockSpec((tq, NA_DH), lambda h, i: (i, h)),
        out_shape=jax.ShapeDtypeStruct((n, heads * NA_DH), BF16),
        compiler_params=_params("parallel", "arbitrary"),
        name="neighbourhood_attention",
    )(qkv, qkv, qkv, qkv_c, qkv_c, bias)


def _ctx_attn_kernel(q_ref, k_ref, v_ref, o_ref, *, scale):
    s = lax.dot_general(q_ref[...], k_ref[...], (((1,), (1,)), ((), ())), preferred_element_type=F32) * scale
    p = jnp.exp(s - jnp.max(s, axis=-1, keepdims=True))
    o = jnp.dot(p.astype(BF16), v_ref[...], preferred_element_type=F32)
    o_ref[...] = (o / jnp.sum(p, axis=-1, keepdims=True)).astype(o_ref.dtype)


def _context_attention(qkv_c, *, heads):
    c = qkv_c.shape[0]
    return pl.pallas_call(
        functools.partial(_ctx_attn_kernel, scale=NA_DH ** -0.5),
        grid=(heads,),
        in_specs=[pl.BlockSpec((c, NA_DH), lambda h: (0, h)),
                  pl.BlockSpec((c, NA_DH), lambda h: (0, heads + h)),
                  pl.BlockSpec((c, NA_DH), lambda h: (0, 2 * heads + h))],
        out_specs=pl.BlockSpec((c, NA_DH), lambda h: (0, h)),
        out_shape=jax.ShapeDtypeStruct((c, heads * NA_DH), BF16),
        compiler_params=_params("parallel"),
        name="context_attention",
    )(qkv_c, qkv_c, qkv_c)


def _rglru_kernel(*refs, chunk, emit_y):
    if emit_y:
        u_ref, wg_ref, bg_ref, lam_ref, h0_ref, gate_ref, y_ref, ht_ref, hsum = refs
    else:
        u_ref, wg_ref, bg_ref, lam_ref, h0_ref, ht_ref = refs
    n, w = u_ref.shape
    n_chunks = n // chunk
    tiles = chunk // SUBLANES
    sub = lax.broadcasted_iota(jnp.int32, (chunk, w), 0) % SUBLANES
    log_sig = jax.nn.log_sigmoid(lam_ref[...])

    def chunk_scan(d, c0, carry):
        u = u_ref[pl.ds(c0, chunk), :]
        ub = u.astype(BF16)
        r = jax.nn.sigmoid(jnp.dot(ub, wg_ref[d, 0], preferred_element_type=F32) + bg_ref[d, 0])
        i = jax.nn.sigmoid(jnp.dot(ub, wg_ref[d, 1], preferred_element_type=F32) + bg_ref[d, 1])
        log_a = RG_C * r * log_sig[d]
        a = jnp.exp(log_a)
        b = jnp.sqrt(jnp.maximum(1.0 - a * a, 0.0)) * i * u
        for sh in (1, 2, 4):
            if d == 0:
                a_s, b_s, edge = pltpu.roll(a, sh, axis=0), pltpu.roll(b, sh, axis=0), sub < sh
            else:
                a_s, b_s = pltpu.roll(a, chunk - sh, axis=0), pltpu.roll(b, chunk - sh, axis=0)
                edge = sub >= SUBLANES - sh
            b = jnp.where(edge, b, a * b_s + b)
            a = jnp.where(edge, a, a * a_s)
        hs = [None] * tiles
        order = range(tiles) if d == 0 else range(tiles - 1, -1, -1)
        for k in order:
            sl = slice(k * SUBLANES, (k + 1) * SUBLANES)
            h = a[sl] * carry + b[sl]
            hs[k] = h
            carry = h[SUBLANES - 1:SUBLANES] if d == 0 else h[0:1]
        return jnp.concatenate(hs, axis=0), carry

    def step(j, carries, second_pass):
        cf, cb = carries
        rows_f = pl.ds(pl.multiple_of(j * chunk, chunk), chunk)
        rows_b = pl.ds(pl.multiple_of((n_chunks - 1 - j) * chunk, chunk), chunk)
        hf, cf = chunk_scan(0, pl.multiple_of(j * chunk, chunk), cf)
        hb, cb = chunk_scan(1, pl.multiple_of((n_chunks - 1 - j) * chunk, chunk), cb)
        if emit_y:
            if second_pass:
                y_ref[rows_f, :] = (jax.nn.gelu(gate_ref[rows_f, :]) * (hsum[rows_f, :] + hf)).astype(y_ref.dtype)
                y_ref[rows_b, :] = (jax.nn.gelu(gate_ref[rows_b, :]) * (hsum[rows_b, :] + hb)).astype(y_ref.dtype)
            else:
                hsum[rows_f, :] = hf
                hsum[rows_b, :] = hb
        return cf, cb

    carries = (h0_ref[0], h0_ref[1])
    half = n_chunks // 2
    carries = lax.fori_loop(0, half, lambda j, c: step(j, c, False), carries)
    carries = lax.fori_loop(half, n_chunks, lambda j, c: step(j, c, True), carries)
    ht_ref[0] = carries[0]
    ht_ref[1] = carries[1]


def _rglru(u, gate_src, gate_w, gate_b, lam, h0):
    n, w = u.shape
    nb, bw = gate_w.shape[2], gate_w.shape[3]
    chunk = _tile(n // 2, 256, SUBLANES)
    assert (n // chunk) % 2 == 0, "the two sweeps hand over at the middle chunk boundary"
    emit_y = gate_src is not None
    col = lambda j: (0, j)
    in_specs = [pl.BlockSpec((n, bw), col),
                pl.BlockSpec((2, 2, None, bw, bw), lambda j: (0, 0, j, 0, 0)),
                pl.BlockSpec((2, 2, None, 1, bw), lambda j: (0, 0, j, 0, 0)),
                pl.BlockSpec((2, 1, bw), lambda j: (0, 0, j)),
                pl.BlockSpec((2, 1, bw), lambda j: (0, 0, j))]
    args = [u, gate_w.astype(BF16), gate_b.astype(F32).reshape(2, 2, nb, 1, bw), lam.astype(F32).reshape(2, 1, w),
            h0.reshape(2, 1, w)]
    ht_spec = pl.BlockSpec((2, 1, bw), lambda j: (0, 0, j))
    ht_shape = jax.ShapeDtypeStruct((2, 1, w), F32)
    if emit_y:
        in_specs.append(pl.BlockSpec((n, bw), col))
        args.append(gate_src)
        out_specs = [pl.BlockSpec((n, bw), col), ht_spec]
        out_shape = [jax.ShapeDtypeStruct((n, w), BF16), ht_shape]
        scratch = [pltpu.VMEM((n, bw), F32)]
    else:
        out_specs, out_shape, scratch = ht_spec, ht_shape, []
    out = pl.pallas_call(
        functools.partial(_rglru_kernel, chunk=chunk, emit_y=emit_y),
        grid=(nb,),
        in_specs=in_specs,
        out_specs=out_specs,
        out_shape=out_shape,
        scratch_shapes=scratch,
        compiler_params=_params("parallel"),
        name="rglru",
    )(*args)
    if emit_y:
        return out[0], out[1].reshape(2, w)
    return None, out.reshape(2, w)


def _dft_tables(n):
    n1 = 1 << (int(math.log2(n)) // 2)
    n2 = n // n1
    assert n1 * n2 == n and n2 % 2 == 0
    i1, i2 = np.arange(n1), np.arange(n2)
    f1 = np.exp(-2j * np.pi * np.outer(i1, i1) / n1)
    f2 = np.exp(-2j * np.pi * np.outer(i2, i2) / n2)
    tw = np.exp(-2j * np.pi * np.outer(i2, i1) / n)
    return n1, n2, f1, f2, tw


def _stack_ri(m, sign=1.0):
    return jnp.asarray(np.concatenate([m.real, sign * m.imag], axis=0), BF16)


def _hy_hidden_kernel(feat_ref, w_in_ref, b_in_ref, w_mid_ref, b_mid_ref, freq_ref, o_ref):
    hp = lax.Precision.HIGHEST
    freq = freq_ref[...]
    hid = jnp.sin(freq * (jnp.dot(feat_ref[...], w_in_ref[...], preferred_element_type=F32, precision=hp)
                          + b_in_ref[...]))
    for m in range(w_mid_ref.shape[0]):
        hid = jnp.sin(freq * (jnp.dot(hid, w_mid_ref[m], preferred_element_type=F32, precision=hp)
                              + b_mid_ref[m]))
    o_ref[...] = hid


def _hy_hidden(L, w_in, b_in, w_mid, b_mid, freq):
    t = jnp.linspace(0.0, 1.0, L, dtype=F32)[:, None]
    ang = ((2.0 * math.pi / L) * jnp.arange(L, dtype=F32)[:, None]
           * jnp.linspace(1e-4, HY_BANDS - 1, HY_BANDS, dtype=F32)[None])
    emb, ffn = w_in.shape
    assert ffn <= LANES
    feats = jnp.zeros((L, LANES), F32).at[:, :emb].set(jnp.concatenate([t, jnp.cos(ang), -jnp.sin(ang)], axis=-1))

    def padded(v, shape):
        return jnp.zeros(shape, F32).at[tuple(slice(0, n) for n in v.shape)].set(v.astype(F32))

    tr = _tile(L, 512, SUBLANES)
    full = lambda *shape: pl.BlockSpec(shape, lambda *_: (0,) * len(shape))
    return pl.pallas_call(
        _hy_hidden_kernel,
        grid=(L // tr,),
        in_specs=[pl.BlockSpec((tr, LANES), lambda i: (i, 0)), full(LANES, LANES), full(1, LANES),
                  full(HY_INNER, LANES, LANES), full(HY_INNER, 1, LANES), full(1, LANES)],
        out_specs=pl.BlockSpec((tr, LANES), lambda i: (i, 0)),
        out_shape=jax.ShapeDtypeStruct((L, LANES), F32),
        compiler_params=_params("parallel"),
        name="hyena_filter_hidden",
    )(feats, padded(w_in, (LANES, LANES)), padded(b_in[None], (1, LANES)), padded(w_mid, (HY_INNER, LANES, LANES)),
      padded(b_mid[:, None], (HY_INNER, 1, LANES)), padded(freq[None], (1, LANES)))


def _hy_filter_stage1_kernel(hf_ref, hb_ref, wf_ref, wb_ref, df_ref, db_ref, f2f_ref, f2b_ref, twr_ref, twi_ref,
                             o_ref, *, n1, length):
    i = pl.program_id(0)
    half = hf_ref.shape[0]
    n2 = o_ref.shape[1]
    row = lax.broadcasted_iota(jnp.int32, (half, 1), 0)
    scale = 1.0 / (length - 1)

    def taps(h_ref, w_ref, d_ref, first):
        t = (first + n1 * row).astype(F32) * scale
        y = jnp.dot(h_ref[...].astype(BF16), w_ref[...], preferred_element_type=F32)
        return (y * jnp.exp(-t * jnp.abs(d_ref[...]))).astype(BF16)

    f2b = jnp.where(i == 0, f2b_ref[0], f2b_ref[1])
    p = (jnp.dot(f2f_ref[...], taps(hf_ref, wf_ref, df_ref, i), preferred_element_type=F32)
         + jnp.dot(f2b, taps(hb_ref, wb_ref, db_ref, (n1 - i) % n1), preferred_element_type=F32))
    ar, ai = p[:n2], p[n2:]
    tr, ti = twr_ref[...], twi_ref[...]
    o_ref[0] = (ar * tr - ai * ti).astype(o_ref.dtype)
    o_ref[1] = (ar * ti + ai * tr).astype(o_ref.dtype)


def _hy_filter_stage1(hid, w_out, deltas, c, tabs):
    n1, n2, _, f2, tw = tabs
    half = n2 // 2
    L = hid.shape[0]
    cols = HY_ORDER * c

    def split(v):
        v4 = v.reshape(v.shape[0], HY_ORDER, 2, c)
        return v4[:, :, 0].reshape(v.shape[0], cols), v4[:, :, 1].reshape(v.shape[0], cols)

    w_f, w_b = split(jnp.zeros((LANES, w_out.shape[1]), BF16).at[:w_out.shape[0]].set(w_out.astype(BF16)))
    d_f, d_b = split(deltas.astype(F32).reshape(1, -1))
    j = np.arange(half)
    f2b = jnp.stack([_stack_ri(f2[:, (n2 - j) % n2]), _stack_ri(f2[:, n2 - 1 - j])])
    hid2d = hid.reshape(half, n1 * LANES)
    twspec = pl.BlockSpec((None, n2, 1), lambda i: (i, 0, 0))
    full = lambda *shape: pl.BlockSpec(shape, lambda i: (0,) * len(shape))
    return pl.pallas_call(
        functools.partial(_hy_filter_stage1_kernel, n1=n1, length=L),
        grid=(n1,),
        in_specs=[pl.BlockSpec((half, LANES), lambda i: (0, i)),
                  pl.BlockSpec((half, LANES), lambda i: (0, (n1 - i) % n1)),
                  full(LANES, cols), full(LANES, cols), full(1, cols), full(1, cols),
                  full(2 * n2, half), full(2, 2 * n2, half), twspec, twspec],
        out_specs=pl.BlockSpec((2, n2, cols), lambda i: (0, 0, i)),
        out_shape=jax.ShapeDtypeStruct((2, n2, n1 * cols), BF16),
        compiler_params=_params("parallel"),
        name="hyena_filter_stage1",
    )(hid2d, hid2d, w_f, w_b, d_f, d_b, _stack_ri(f2[:, :half]), f2b,
      jnp.asarray(tw.real.T.reshape(n1, n2, 1), F32), jnp.asarray(tw.imag.T.reshape(n1, n2, 1), F32)
      ).reshape(2, n2, n1, cols)


def _hy_filter_spectrum_kernel(a_ref, f1_ref, h_ref):
    n1 = h_ref.shape[1]
    f1 = f1_ref[...]
    p = jnp.dot(f1, a_ref[0], preferred_element_type=F32)
    q = jnp.dot(f1, a_ref[1], preferred_element_type=F32)
    h_ref[0] = (p[:n1] - q[n1:]).astype(h_ref.dtype)
    h_ref[1] = (p[n1:] + q[:n1]).astype(h_ref.dtype)


def _hy_filter_spectrum(fa, c, tabs):
    n1, n2, f1, _, _ = tabs
    return pl.pallas_call(
        _hy_filter_spectrum_kernel,
        grid=(HY_ORDER, n2),
        in_specs=[pl.BlockSpec((2, None, n1, c), lambda o, k: (0, k, 0, o)),
                  pl.BlockSpec((2 * n1, n1), lambda o, k: (0, 0))],
        out_specs=pl.BlockSpec((None, 2, None, n1, c), lambda o, k: (o, 0, k, 0, 0)),
        out_shape=jax.ShapeDtypeStruct((HY_ORDER, 2, n2, n1, c), BF16),
        compiler_params=_params("parallel", "parallel"),
        name="hyena_filter_spectrum",
    )(fa, _stack_ri(f1))


def _hy_stage1_kernel(x_ref, w_ref, twr_ref, twi_ref, o_ref):
    half, r, c = x_ref.shape
    n2 = o_ref.shape[1]
    p = jnp.dot(w_ref[...], x_ref[...].reshape(half * r, c).astype(BF16), preferred_element_type=F32)
    ar, ai = p[:n2 * r], p[n2 * r:]
    tr, ti = twr_ref[...], twi_ref[...]
    o_ref[0] = (ar * tr - ai * ti).reshape(n2, r, c)
    o_ref[1] = (ar * ti + ai * tr).reshape(n2, r, c)


def _hy_stage1(src, part, c, tabs):
    n1, n2, _, f2, tw = tabs
    half = n2 // 2
    r = SUBLANES
    cb = _tile(c, 1024, LANES)
    per = c // cb
    eye = np.eye(r)
    w = np.einsum('pkn,jJ->pkjnJ', np.stack([f2.real[:, :half], f2.imag[:, :half]]), eye)
    w = jnp.asarray(w.reshape(2 * n2 * r, half * r), BF16)

    def blocked(t):
        return jnp.asarray(t.reshape(n2, n1 // r, r).transpose(1, 0, 2).reshape(n1 // r, n2 * r, 1), F32)

    twspec = pl.BlockSpec((None, n2 * r, 1), lambda i, k: (i, 0, 0))
    return pl.pallas_call(
        _hy_stage1_kernel,
        grid=(n1 // r, per),
        in_specs=[pl.BlockSpec((half, r, cb), lambda i, k: (0, i, part * per + k)),
                  pl.BlockSpec((2 * n2 * r, half * r), lambda i, k: (0, 0)), twspec, twspec],
        out_specs=pl.BlockSpec((2, n2, r, cb), lambda i, k: (0, 0, i, k)),
        out_shape=jax.ShapeDtypeStruct((2, n2, n1, c), F32),
        compiler_params=_params("parallel", "parallel"),
        name="hyena_dft_stage1",
    )(src.reshape(half, n1, src.shape[1]), w, blocked(tw.real), blocked(tw.imag))


def _hy_mid_kernel(a_ref, h_ref, f1_ref, twr_ref, twi_ref, o_ref):
    n1 = o_ref.shape[1]
    f1 = f1_ref[...]
    p = jnp.dot(f1, a_ref[0].astype(BF16), preferred_element_type=F32)
    q = jnp.dot(f1, a_ref[1].astype(BF16), preferred_element_type=F32)
    xr = p[:n1] - q[n1:]
    xi = p[n1:] + q[:n1]
    hr, hi = h_ref[0].astype(F32), h_ref[1].astype(F32)
    yr = (xr * hr - xi * hi).astype(BF16)
    yi = (xr * hi + xi * hr).astype(BF16)
    p = jnp.dot(f1, yr, preferred_element_type=F32)
    q = jnp.dot(f1, yi, preferred_element_type=F32)
    br = p[:n1] + q[n1:]
    bi = q[:n1] - p[n1:]
    tr, ti = twr_ref[...], twi_ref[...]
    o_ref[0] = br * tr + bi * ti
    o_ref[1] = bi * tr - br * ti


def _hy_stage3_kernel(b_ref, w_ref, z_ref, gate_ref, skip_ref, o_ref, *, inv_n):
    _, n2, r, c = b_ref.shape
    half = z_ref.shape[0]
    acc = jnp.dot(w_ref[...], b_ref[...].reshape(2 * n2 * r, c).astype(BF16), preferred_element_type=F32)
    z = z_ref[...].reshape(half * r, c)
    gate = gate_ref[...].reshape(half * r, c)
    o_ref[...] = (gate * (acc * inv_n + z * skip_ref[...])).reshape(half, r, c).astype(o_ref.dtype)


def _hy_long_conv(z_src, z_part, gate_src, gate_part, spec, order, skip, tabs, out_dtype):
    n1, n2, f1, f2, tw = tabs
    L = z_src.shape[0]
    c = skip.shape[0]
    half = n2 // 2
    r = SUBLANES
    cb = _tile(c, 1024, LANES)
    per = c // cb
    slab = pl.BlockSpec((2, None, n1, c), lambda k: (0, k, 0, 0))
    a = _hy_stage1(z_src, z_part, c, tabs)
    twspec = pl.BlockSpec((None, n1, 1), lambda k: (k, 0, 0))
    b = pl.pallas_call(
        _hy_mid_kernel,
        grid=(n2,),
        in_specs=[slab, pl.BlockSpec((None, 2, None, n1, c), lambda k: (order, 0, k, 0, 0)),
                  pl.BlockSpec((2 * n1, n1), lambda k: (0, 0)), twspec, twspec],
        out_specs=slab,
        out_shape=jax.ShapeDtypeStruct((2, n2, n1, c), F32),
        compiler_params=_params("parallel"),
        name="hyena_spectral_product",
    )(a, spec, _stack_ri(f1), jnp.asarray(tw.real.reshape(n2, n1, 1), F32),
      jnp.asarray(tw.imag.reshape(n2, n1, 1), F32))
    w3 = np.einsum('pok,jJ->ojpkJ', np.stack([f2.real[:half], f2.imag[:half]]), np.eye(r))
    w3 = jnp.asarray(w3.reshape(half * r, 2 * n2 * r), BF16)
    out = pl.pallas_call(
        functools.partial(_hy_stage3_kernel, inv_n=1.0 / (n1 * n2)),
        grid=(n1 // r, per),
        in_specs=[pl.BlockSpec((2, n2, r, cb), lambda i, k: (0, 0, i, k)),
                  pl.BlockSpec((half * r, 2 * n2 * r), lambda i, k: (0, 0)),
                  pl.BlockSpec((half, r, cb), lambda i, k: (0, i, z_part * per + k)),
                  pl.BlockSpec((half, r, cb), lambda i, k: (0, i, gate_part * per + k)),
                  pl.BlockSpec((1, cb), lambda i, k: (0, k))],
        out_specs=pl.BlockSpec((half, r, cb), lambda i, k: (0, i, k)),
        out_shape=jax.ShapeDtypeStruct((half, n1, c), out_dtype),
        compiler_params=_params("parallel", "parallel"),
        name="hyena_dft_stage3",
    )(b, w3, z_src.reshape(half, n1, z_src.shape[1]), gate_src.reshape(half, n1, gate_src.shape[1]),
      skip.astype(F32).reshape(1, c))
    return out.reshape(L, c)


def _hyena(p, col0, conv_w, conv_b, w_in, b_in, w_mid, b_mid, w_out, freq, deltas, skip):
    assert HY_ORDER == 2
    L = p.shape[0]
    c = skip.shape[1]
    tabs = _dft_tables(2 * L)
    u = _dwconv(p, conv_w, conv_b, (HY_CONV - 1) // 2, col0=col0)
    hid = _hy_hidden(L, w_in, b_in, w_mid, b_mid, freq)
    spec = _hy_filter_spectrum(_hy_filter_stage1(hid, w_out, deltas, c, tabs), c, tabs)
    z = _hy_long_conv(u, 2, u, 0, spec, 0, skip[0], tabs, F32)
    return _hy_long_conv(z, 0, u, 1, spec, 1, skip[1], tabs, BF16)


def _even_layer_mix(h, hc, in_w, conv_w, conv_b, dt_bias, a_log, d_skip, norm_w, rpb):
    d = h.shape[1]
    ssd_w = norm_w.shape[0]
    heads = ssd_w // SSD_P
    e = heads // SSD_G
    na_heads = (d - ssd_w) // NA_DH
    o1 = ssd_w
    o2 = o1 + conv_w.shape[1]
    o3 = o2 + 2 * heads
    w_qkv = in_w[:, o3:].astype(BF16)
    src = np.arange(2 * heads)
    direction, head = src // heads, src % heads
    place = np.zeros((2 * heads, SSD_G * LANES), np.float32)
    place[src, (head // e) * LANES + direction * e + head % e] = 1.0
    w_dt = jnp.dot(in_w[:, o2:o3].astype(F32), jnp.asarray(place), precision=lax.Precision.HIGHEST).astype(BF16)

    def grouped(v):
        vg = v.astype(F32).reshape(2, SSD_G, e).transpose(1, 0, 2).reshape(SSD_G, 2 * e)
        return jnp.zeros((SSD_G, LANES), F32).at[:, :2 * e].set(vg).reshape(SSD_G, 1, LANES)

    bias_g = grouped(dt_bias)
    a_g = grouped(-jnp.exp(a_log.astype(F32)))
    d_cols = jnp.repeat(d_skip.astype(F32), SSD_P)

    def ssd(t, s0):
        zx = _mm(t, in_w, n=o2)
        xbc = _dwconv(zx, conv_w, conv_b, SSD_CONV // 2, col0=o1, silu=True)
        y_f, y_b, s_t = _ssd_scan(xbc, _mm(t, w_dt), bias_g, a_g, s0, ssd_w=ssd_w, groups=SSD_G)
        return _ssd_gate_norm(y_f, y_b, xbc, zx, d_cols, norm_w, groups=SSD_G), s_t

    zero = jnp.zeros((2, SSD_G, e * SSD_P, SSD_N), F32)
    qkv_c = _mm(hc, w_qkv, out_dtype=BF16)
    y_ssd_c, s_c = ssd(hc, zero)
    qkv = _mm(h, w_qkv, out_dtype=BF16)
    y_ssd, _ = ssd(h, s_c)
    y_na = _neighbourhood_attention(qkv, qkv_c, rpb, heads=na_heads)
    y_na_c = _context_attention(qkv_c, heads=na_heads)
    return (y_ssd, y_na), (y_ssd_c, y_na_c)


def _odd_layer_mix(h, hc, in_w, conv_w, conv_b, gate_w, gate_b, lam, hy_params):
    rg_w = lam.shape[1]
    pad = RG_CONV // 2
    u_c = _dwconv(_mm(hc, in_w, col0=rg_w, n=rg_w), conv_w, conv_b, pad)
    _, h_c = _rglru(u_c, None, gate_w, gate_b, lam, jnp.zeros((2, rg_w), F32))
    proj = _mm(h, in_w)
    u = _dwconv(proj, conv_w, conv_b, pad, col0=rg_w)
    y_rg, _ = _rglru(u, proj, gate_w, gate_b, lam, h_c)
    return y_rg, _hyena(proj, 2 * rg_w, *hy_params)


def kernel(x, c, ctx, c_ctx, ada_w, ada_b, norm_mix_w, norm_ffn_w, norm_out_w, ev_in_w, ev_ssd_conv_w, ev_ssd_conv_b, ev_ssd_dt_bias, ev_ssd_a_log, ev_ssd_d, ev_ssd_norm_w, ev_na_rpb, ev_out_w, ev_ffn_w1, ev_ffn_w3, ev_ffn_w2, od_in_w, od_rg_conv_w, od_rg_conv_b, od_rg_gate_w, od_rg_gate_b, od_rg_lambda, od_hy_conv_w, od_hy_conv_b, od_hy_w_in, od_hy_b_in, od_hy_w_mid, od_hy_b_mid, od_hy_w_out, od_hy_freq, od_hy_deltas, od_hy_skip, od_out_w, od_router_w, od_moe_w1, od_moe_w3, od_moe_w2):
    batch, seq, d = x.shape
    assert batch == 1 and ada_w.shape[0] == 2, "kernel is written for one sequence and the even/odd layer pair"
    xs = x[0]
    cs = ctx[0]
    cvecs = jnp.concatenate([jax.nn.silu(c), jax.nn.silu(c_ctx)[None]], axis=0)
    mods = _ada(cvecs, ada_w, ada_b).reshape(2, 2, ADA_CHUNKS, d)

    mod, mod_c = mods[0, 0], mods[0, 1]
    h = _mod_norm(xs, norm_mix_w[0], mod[0], mod[1])
    hc = _mod_norm(cs, norm_mix_w[0], mod_c[0], mod_c[1])
    y, y_c = _even_layer_mix(h, hc, ev_in_w[0], ev_ssd_conv_w[0], ev_ssd_conv_b[0], ev_ssd_dt_bias[0],
                             ev_ssd_a_log[0], ev_ssd_d[0], ev_ssd_norm_w[0], ev_na_rpb[0])
    xs = _mm_residual2(*y, ev_out_w[0], xs, mod[2])
    cs = _mm_residual2(*y_c, ev_out_w[0], cs, mod_c[2])
    w1, w3, w2 = ev_ffn_w1[0], ev_ffn_w3[0], ev_ffn_w2[0].astype(BF16)
    h2 = _mod_norm(xs, norm_ffn_w[0], mod[3], mod[4])
    xs = _mm_residual(_mm_swiglu(h2, w1, w3), w2, xs, mod[5], tm=256, tn=512)
    h2c = _mod_norm(cs, norm_ffn_w[0], mod_c[3], mod_c[4])
    cs = _mm_residual(_mm_swiglu(h2c, w1, w3), w2, cs, mod_c[5], tm=256, tn=512)

    mod, mod_c = mods[1, 0], mods[1, 1]
    h = _mod_norm(xs, norm_mix_w[1], mod[0], mod[1])
    hc = _mod_norm(cs, norm_mix_w[1], mod_c[0], mod_c[1])
    hy_params = (od_hy_conv_w[0], od_hy_conv_b[0], od_hy_w_in[0], od_hy_b_in[0], od_hy_w_mid[0],
                 od_hy_b_mid[0], od_hy_w_out[0], od_hy_freq[0], od_hy_deltas[0], od_hy_skip[0])
    y = _odd_layer_mix(h, hc, od_in_w[0], od_rg_conv_w[0], od_rg_conv_b[0], od_rg_gate_w[0],
                       od_rg_gate_b[0], od_rg_lambda[0], hy_params)
    xs = _mm_residual2(*y, od_out_w[0], xs, mod[2])
    h2, route = _mod_norm_router(xs, norm_ffn_w[1], mod[3], mod[4], od_router_w[0])
    y1, y2 = _moe(h2, route, od_moe_w1[0], od_moe_w3[0], od_moe_w2[0])
    return _final_norm(xs, y1, y2, mod[5], norm_out_w)[None]
```
